```python
import jax, jax.numpy as jnp
from jax import lax
import numpy as np

D_MODEL = 4096
BATCH = 4
SEQ = 2048
DEPTH = 2
DEC_BATCH = 32
DEC_SEQ = 8
PAST_LEN = 16384
PAGE_SIZE = 128

N_AB = (DEPTH + 1) // 2
N_FOX = DEPTH // 2

H_A = 32
HD_A = 64
C_A = H_A * HD_A
LORA_DECAY = 128
LORA_AAA = 128
LORA_GATE = 480
P_A = 3 * C_A + LORA_DECAY + LORA_AAA + LORA_GATE
RWKV_SPLIT = (C_A, 2 * C_A, 3 * C_A, 3 * C_A + LORA_DECAY, 3 * C_A + LORA_DECAY + LORA_AAA)
GN_EPS_A = 64e-5

H_B = 32
KVH_B = 4
G_B = H_B // KVH_B
HD_B = 64
C_B = H_B * HD_B
KV_B = KVH_B * HD_B
WINDOW = 128
P_AB = P_A + C_B + 2 * KV_B
AB_SPLIT = (P_A, P_A + C_B, P_A + C_B + KV_B)
ROPE_THETA = 10000.0

H_C = 32
KVH_C = 8
G_C = H_C // KVH_C
HD_C = 128
C_C = H_C * HD_C
KV_C = KVH_C * HD_C
P_C = C_C + 2 * KV_C + H_C
FOX_SPLIT = (C_C, C_C + KV_C, C_C + 2 * KV_C)
Q_BLOCK = 128

N_EXPERTS = 64
TOP_K = 6
N_GROUPS = 8
TOPK_GROUPS = 4
D_EXPERT = 512
D_SHARED = 512
ROUTED_SCALE = 2.5
MOE_BLOCK = 128

NORM_EPS = 1e-6

kernel_name = 'hybrid_rwkv7_swa_fox_moe_step'


def rmsnorm(x, w):
    xf = x.astype(jnp.float32)
    y = xf * lax.rsqrt(jnp.mean(xf * xf, axis=-1, keepdims=True) + NORM_EPS)
    return (y * w.astype(jnp.float32)).astype(x.dtype)


def ada_params(c, w, b):
    m = jax.nn.silu(c) @ w + b
    return jnp.split(m[:, None, :], 6, axis=-1)


def rope(x, pos):
    half = x.shape[-1] // 2
    inv = ROPE_THETA ** (-jnp.arange(half, dtype=jnp.float32) / half)
    ang = pos.astype(jnp.float32)[:, None] * inv[None, :]
    cos = jnp.cos(ang)[None, :, None, :]
    sin = jnp.sin(ang)[None, :, None, :]
    xf = x.astype(jnp.float32)
    x1, x2 = xf[..., :half], xf[..., half:]
    return jnp.concatenate([x1 * cos - x2 * sin, x2 * cos + x1 * sin], axis=-1).astype(x.dtype)


def rwkv7_scan(S0, r, decay, k, v, kk, a):
    def step(S, inp):
        r_t, d_t, k_t, v_t, kk_t, a_t = inp
        sa = jnp.einsum('bhvk,bhk->bhv', S, kk_t)
        S = (S * d_t[:, :, None, :] - sa[..., None] * (kk_t * a_t)[:, :, None, :]
             + v_t[..., None] * k_t[:, :, None, :])
        return S, jnp.einsum('bhvk,bhk->bhv', S, r_t)
    xs = tuple(jnp.swapaxes(t, 0, 1) for t in (r, decay, k, v, kk, a))
    S, ys = lax.scan(step, S0, xs)
    return jnp.swapaxes(ys, 0, 1), S


def rwkv7_mixer(pa, prev_row, S0, mu, w0, w_dec_up, a0, w_aaa_up, w_gate_up, k_k, k_a, r_k, lnx_w, lnx_b):
    B, T, _ = pa.shape
    shifted = jnp.concatenate([prev_row[:, None, :].astype(pa.dtype), pa[:, :-1]], axis=1)
    m = pa + (shifted - pa) * mu
    r, k, v, wd, ad, gd = jnp.split(m, RWKV_SPLIT, axis=-1)
    w = -jax.nn.softplus(-(w0 + jnp.tanh(wd) @ w_dec_up)) - 0.5
    a = jax.nn.sigmoid(a0 + ad @ w_aaa_up)
    g = jax.nn.sigmoid(gd) @ w_gate_up
    heads = lambda t: t.reshape(B, T, H_A, HD_A).astype(jnp.float32)
    kk = heads(k * k_k)
    kk = kk / jnp.maximum(jnp.sqrt(jnp.sum(kk * kk, axis=-1, keepdims=True)), 1e-12)
    k = k * (1 + (a - 1) * k_a)
    rh, kh, vh, ah = heads(r), heads(k), heads(v), heads(a)
    decay = jnp.exp(-jnp.exp(heads(w)))
    y, S = rwkv7_scan(S0.astype(jnp.float32), rh, decay, kh, vh, kk, ah)
    mean = jnp.mean(y, axis=-1, keepdims=True)
    var = jnp.mean(jnp.square(y - mean), axis=-1, keepdims=True)
    y = ((y - mean) * lax.rsqrt(var + GN_EPS_A)).reshape(B, T, C_A) * lnx_w + lnx_b
    bonus = (jnp.sum(rh * kh * r_k, axis=-1, keepdims=True) * vh).reshape(B, T, C_A)
    out = ((y + bonus) * g).astype(pa.dtype)
    return out, S.astype(S0.dtype), pa[:, -1]


def sink_softmax(s, sink):
    sk = jnp.broadcast_to(sink[:, :, None], s.shape[:-1])[..., None]
    return jax.nn.softmax(jnp.concatenate([s, sk], axis=-1), axis=-1)[..., :-1]


def swa_prompt(q, k, v, sinks):
    B, T = q.shape[:2]
    nb = T // WINDOW
    qb = q.reshape(B, nb, WINDOW, KVH_B, G_B, HD_B)
    kb = k.reshape(B, nb, WINDOW, KVH_B, HD_B)
    vb = v.reshape(B, nb, WINDOW, KVH_B, HD_B)
    pad = ((0, 0), (1, 0), (0, 0), (0, 0), (0, 0))
    kc = jnp.concatenate([jnp.pad(kb, pad)[:, :-1], kb], axis=2)
    vc = jnp.concatenate([jnp.pad(vb, pad)[:, :-1], vb], axis=2)
    s = jnp.einsum('bnqhgd,bnkhd->bnhgqk', qb, kc).astype(jnp.float32) * (HD_B ** -0.5)
    qi = jnp.arange(WINDOW)[:, None]
    ki = jnp.arange(2 * WINDOW)[None, :] - WINDOW
    band = (ki <= qi) & (qi - ki < WINDOW)
    valid = band[None] & ((jnp.arange(nb)[:, None, None] * WINDOW + ki[None]) >= 0)
    s = jnp.where(valid[None, :, None, None], s, -jnp.inf)
    p = sink_softmax(s, sinks.reshape(KVH_B, G_B).astype(jnp.float32))
    o = jnp.einsum('bnhgqk,bnkhd->bnqhgd', p.astype(v.dtype), vc)
    return o.reshape(B, T, C_B), k[:, -WINDOW:], v[:, -WINDOW:]


def swa_sample(q, k, v, kbuf, vbuf, sinks):
    B, T = q.shape[:2]
    kc = jnp.concatenate([kbuf.astype(k.dtype), k], axis=1)
    vc = jnp.concatenate([vbuf.astype(v.dtype), v], axis=1)
    kpos = jnp.arange(WINDOW + T) - WINDOW
    qpos = jnp.arange(T)
    mask = (kpos[None, :] <= qpos[:, None]) & (qpos[:, None] - kpos[None, :] < WINDOW)
    s = jnp.einsum('bqhgd,bkhd->bhgqk', q.reshape(B, T, KVH_B, G_B, HD_B), kc).astype(jnp.float32) * (HD_B ** -0.5)
    s = jnp.where(mask, s, -jnp.inf)
    p = sink_softmax(s, sinks.reshape(KVH_B, G_B).astype(jnp.float32))
    o = jnp.einsum('bhgqk,bkhd->bqhgd', p.astype(v.dtype), vc)
    return o.reshape(B, T, C_B), kc[:, -WINDOW:], vc[:, -WINDOW:]


def ab_mixer(h, pos, prev_row, S0, kbuf, vbuf, w_in, w_out, rwkv_params, sinks):
    B, T, _ = h.shape
    pa, q, k, v = jnp.split(h @ w_in, AB_SPLIT, axis=-1)
    o_a, S, last_row = rwkv7_mixer(pa, prev_row, S0, *rwkv_params)
    q = rope(q.reshape(B, T, H_B, HD_B), pos)
    k = rope(k.reshape(B, T, KVH_B, HD_B), pos)
    v = v.reshape(B, T, KVH_B, HD_B)
    if kbuf is None:
        o_b, kw, vw = swa_prompt(q, k, v, sinks)
    else:
        o_b, kw, vw = swa_sample(q, k, v, kbuf, vbuf, sinks)
    out = jnp.concatenate([o_a, o_b], axis=-1) @ w_out
    return out, S, last_row, kw, vw


def fox_project(h, w_in, b_f):
    B, T, _ = h.shape
    q, k, v, fl = jnp.split(h @ w_in, FOX_SPLIT, axis=-1)
    logf = jax.nn.log_sigmoid((fl + b_f).astype(jnp.float32))
    return (q.reshape(B, T, H_C, HD_C), k.reshape(B, T, KVH_C, HD_C),
            v.reshape(B, T, KVH_C, HD_C), logf)


def fox_prompt(q, k, v, logf):
    B, T = q.shape[:2]
    nb = T // Q_BLOCK
    F = jnp.cumsum(logf, axis=1)
    Fk = jnp.transpose(F.reshape(B, T, KVH_C, G_C), (0, 2, 3, 1))
    qb = jnp.moveaxis(q.reshape(B, nb, Q_BLOCK, KVH_C, G_C, HD_C), 1, 0)
    Fq = jnp.moveaxis(Fk.reshape(B, KVH_C, G_C, nb, Q_BLOCK), 3, 0)
    kpos = jnp.arange(T)

    def block(args):
        q_blk, fq, start = args
        s = jnp.einsum('bqhgd,bkhd->bhgqk', q_blk, k).astype(jnp.float32) * (HD_C ** -0.5)
        s = s + fq[..., None] - Fk[:, :, :, None, :]
        qpos = start + jnp.arange(Q_BLOCK)
        s = jnp.where(kpos[None, :] <= qpos[:, None], s, -jnp.inf)
        p = jax.nn.softmax(s, axis=-1)
        return jnp.einsum('bhgqk,bkhd->bqhgd', p.astype(v.dtype), v)

    o = lax.map(block, (qb, Fq, jnp.arange(nb) * Q_BLOCK))
    return jnp.moveaxis(o, 0, 1).reshape(B, T, C_C)


def fox_sample(q, k, v, logf, cache_k, cache_v, cache_lf, layer, page_table):
    B, T = q.shape[:2]
    past_len = page_table.shape[1] * PAGE_SIZE
    L = past_len + T
    kpos = jnp.arange(L)
    qpos = past_len + jnp.arange(T)
    causal = kpos[None, :] <= qpos[:, None]

    def one(args):
        pt, q_s, k_s, v_s, lf_s = args
        k_all = jnp.concatenate([cache_k[layer, pt].reshape(past_len, KVH_C, HD_C).astype(k_s.dtype), k_s], axis=0)
        v_all = jnp.concatenate([cache_v[layer, pt].reshape(past_len, KVH_C, HD_C).astype(v_s.dtype), v_s], axis=0)
        lf_all = jnp.concatenate([cache_lf[layer, pt].reshape(past_len, H_C).astype(jnp.float32), lf_s], axis=0)
        F = jnp.transpose(jnp.cumsum(lf_all, axis=0).reshape(L, KVH_C, G_C), (1, 2, 0))
        s = jnp.einsum('qhgd,khd->hgqk', q_s.reshape(T, KVH_C, G_C, HD_C), k_all).astype(jnp.float32) * (HD_C ** -0.5)
        s = s + F[:, :, past_len:, None] - F[:, :, None, :]
        s = jnp.where(causal, s, -jnp.inf)
        p = jax.nn.softmax(s, axis=-1)
        return jnp.einsum('hgqk,khd->qhgd', p.astype(v_all.dtype), v_all).reshape(T, C_C)

    return lax.map(one, (page_table, q, k, v, logf))


def moe_ffn(h, layer, w_router, b_router, w_gate, w_up, w_down, ws_gate, ws_up, ws_down):
    T = h.shape[0]
    per_group = N_EXPERTS // N_GROUPS
    scores = jax.nn.sigmoid((h @ w_router[layer]).astype(jnp.float32))
    biased = scores + b_router[layer].astype(jnp.float32)
    grp = jnp.sum(lax.top_k(biased.reshape(T, N_GROUPS, per_group), 2)[0], axis=-1)
    gidx = lax.top_k(grp, TOPK_GROUPS)[1]
    gmask = jnp.any(gidx[:, :, None] == jnp.arange(N_GROUPS)[None, None, :], axis=1)
    biased = jnp.where(jnp.repeat(gmask, per_group, axis=1), biased, -jnp.inf)
    eidx = lax.top_k(biased, TOP_K)[1]
    gates = jnp.take_along_axis(scores, eidx, axis=1)
    gates = gates / jnp.sum(gates, axis=-1, keepdims=True) * ROUTED_SCALE

    A = T * TOP_K
    e_flat = eidx.reshape(A).astype(jnp.int32)
    t_flat = jnp.repeat(jnp.arange(T, dtype=jnp.int32), TOP_K)
    g_flat = gates.reshape(A)
    order = jnp.argsort(e_flat)
    e_sorted = e_flat[order]
    counts = jnp.zeros((N_EXPERTS,), jnp.int32).at[e_flat].add(1)
    padded = (counts + MOE_BLOCK - 1) // MOE_BLOCK * MOE_BLOCK
    seg_start = jnp.cumsum(counts) - counts
    pad_end = jnp.cumsum(padded)
    pad_start = pad_end - padded
    dest = pad_start[e_sorted] + jnp.arange(A, dtype=jnp.int32) - seg_start[e_sorted]
    n_blocks = -(-(A + N_EXPERTS * (MOE_BLOCK - 1)) // MOE_BLOCK)
    n_rows = n_blocks * MOE_BLOCK
    row_tok = jnp.zeros((n_rows,), jnp.int32).at[dest].set(t_flat[order])
    row_w = jnp.zeros((n_rows,), jnp.float32).at[dest].set(g_flat[order])
    blk_exp = jnp.minimum(jnp.searchsorted(pad_end, jnp.arange(n_blocks, dtype=jnp.int32) * MOE_BLOCK, side='right'),
                          N_EXPERTS - 1).astype(jnp.int32)

    def expert_block(args):
        tok, e = args
        xb = h[tok]
        act = jax.nn.silu(xb @ w_gate[layer, e]) * (xb @ w_up[layer, e])
        return act @ w_down[layer, e]

    y_rows = lax.map(expert_block, (row_tok.reshape(n_blocks, MOE_BLOCK), blk_exp))
    y_rows = y_rows.reshape(n_rows, -1) * row_w[:, None].astype(h.dtype)
    routed = jax.ops.segment_sum(y_rows, row_tok, num_segments=T)
    shared = (jax.nn.silu(h @ ws_gate[layer]) * (h @ ws_up[layer])) @ ws_down[layer]
    return routed + shared


def setup_inputs(seed: int = 0) -> dict:
    key = jax.random.key(seed)
    ks = iter(jax.random.split(key, 64))
    f32 = jnp.float32
    D = D_MODEL

    def nrm(shape, scale):
        return jax.random.normal(next(ks), shape, f32) * scale

    n_pages = PAST_LEN // PAGE_SIZE
    n_used = DEC_BATCH * n_pages
    n_pool = n_used + n_used // 4
    d = {}
    d['x_prompt'] = nrm((BATCH, SEQ, D), 1.0)
    d['x_sample'] = nrm((DEC_BATCH, DEC_SEQ, D), 1.0)
    d['c_prompt'] = nrm((BATCH, D), 1.0)
    d['c_sample'] = nrm((DEC_BATCH, D), 1.0)
    d['state_rwkv'] = nrm((N_AB, DEC_BATCH, H_A, HD_A, HD_A), 0.3)
    d['state_rwkv_shift'] = nrm((N_AB, DEC_BATCH, P_A), 1.0)
    d['cache_swa_k'] = nrm((N_AB, DEC_BATCH, WINDOW, KVH_B, HD_B), 1.0)
    d['cache_swa_v'] = nrm((N_AB, DEC_BATCH, WINDOW, KVH_B, HD_B), 1.0)
    d['cache_fox_k'] = nrm((N_FOX, n_pool, PAGE_SIZE, KVH_C, HD_C), 1.0)
    d['cache_fox_v'] = nrm((N_FOX, n_pool, PAGE_SIZE, KVH_C, HD_C), 1.0)
    d['cache_fox_logf'] = jax.nn.log_sigmoid(2.0 + nrm((N_FOX, n_pool, PAGE_SIZE, H_C), 0.5))
    d['page_table'] = jax.random.permutation(next(ks), n_pool)[:n_used].reshape(DEC_BATCH, n_pages).astype(jnp.int32)
    d['ada_w'] = nrm((DEPTH, D, 6 * D), 0.5 * D ** -0.5)
    d['ada_b'] = nrm((DEPTH, 6 * D), 0.02)
    d['norm_w'] = 1.0 + nrm((DEPTH, 2, D), 0.02)
    d['final_norm_w'] = 1.0 + nrm((D,), 0.02)
    d['ab_w_in'] = nrm((N_AB, D, P_AB), D ** -0.5)
    d['ab_w_out'] = nrm((N_AB, C_A + C_B, D), (C_A + C_B) ** -0.5)
    d['rwkv_mu'] = jax.random.uniform(next(ks), (N_AB, P_A), f32)
    d['rwkv_w0'] = nrm((N_AB, C_A), 0.5)
    d['rwkv_w_decay_up'] = nrm((N_AB, LORA_DECAY, C_A), 0.5 * LORA_DECAY ** -0.5)
    d['rwkv_a0'] = nrm((N_AB, C_A), 0.5)
    d['rwkv_w_aaa_up'] = nrm((N_AB, LORA_AAA, C_A), 0.5 * LORA_AAA ** -0.5)
    d['rwkv_w_gate_up'] = nrm((N_AB, LORA_GATE, C_A), LORA_GATE ** -0.5)
    d['rwkv_k_k'] = 0.85 + nrm((N_AB, C_A), 0.05)
    d['rwkv_k_a'] = 1.0 + nrm((N_AB, C_A), 0.05)
    d['rwkv_r_k'] = nrm((N_AB, H_A, HD_A), 0.1)
    d['rwkv_lnx_w'] = 1.0 + nrm((N_AB, C_A), 0.02)
    d['rwkv_lnx_b'] = nrm((N_AB, C_A), 0.02)
    d['swa_sinks'] = nrm((N_AB, H_B), 0.5)
    d['fox_w_in'] = nrm((N_FOX, D, P_C), D ** -0.5)
    d['fox_b_f'] = 2.0 + nrm((N_FOX, H_C), 0.1)
    d['fox_w_out'] = nrm((N_FOX, C_C, D), C_C ** -0.5)
    d['moe_w_router'] = nrm((DEPTH, D, N_EXPERTS), D ** -0.5)
    d['moe_b_router'] = nrm((DEPTH, N_EXPERTS), 0.01)
    d['moe_w_gate'] = nrm((DEPTH, N_EXPERTS, D, D_EXPERT), D ** -0.5)
    d['moe_w_up'] = nrm((DEPTH, N_EXPERTS, D, D_EXPERT), D ** -0.5)
    d['moe_w_down'] = nrm((DEPTH, N_EXPERTS, D_EXPERT, D), D_EXPERT ** -0.5)
    d['shared_w_gate'] = nrm((DEPTH, D, D_SHARED), D ** -0.5)
    d['shared_w_up'] = nrm((DEPTH, D, D_SHARED), D ** -0.5)
    d['shared_w_down'] = nrm((DEPTH, D_SHARED, D), D_SHARED ** -0.5)
    return d


def reference(x_prompt, x_sample, c_prompt, c_sample, state_rwkv, state_rwkv_shift, cache_swa_k, cache_swa_v,
              cache_fox_k, cache_fox_v, cache_fox_logf, page_table, ada_w, ada_b, norm_w, final_norm_w,
              ab_w_in, ab_w_out, rwkv_mu, rwkv_w0, rwkv_w_decay_up, rwkv_a0, rwkv_w_aaa_up, rwkv_w_gate_up,
              rwkv_k_k, rwkv_k_a, rwkv_r_k, rwkv_lnx_w, rwkv_lnx_b, swa_sinks, fox_w_in, fox_b_f, fox_w_out,
              moe_w_router, moe_b_router, moe_w_gate, moe_w_up, moe_w_down, shared_w_gate, shared_w_up, shared_w_down):
    Bp, Tp, D = x_prompt.shape
    Bs, Ts, _ = x_sample.shape
    past_len = page_table.shape[1] * PAGE_SIZE
    pos_p = jnp.arange(Tp)
    pos_s = past_len + jnp.arange(Ts)
    xp, xs = x_prompt, x_sample
    rw_S_p, rw_sh_p, sw_k_p, sw_v_p, fk_p, fv_p, flf_p = [], [], [], [], [], [], []
    rw_S_s, rw_sh_s, sw_k_s, sw_v_s, fk_s, fv_s, flf_s = [], [], [], [], [], [], []
    for l in range(DEPTH):
        sh1_p, sc1_p, g1_p, sh2_p, sc2_p, g2_p = ada_params(c_prompt, ada_w[l], ada_b[l])
        sh1_s, sc1_s, g1_s, sh2_s, sc2_s, g2_s = ada_params(c_sample, ada_w[l], ada_b[l])
        hp = rmsnorm(xp, norm_w[l, 0]) * (1 + sc1_p) + sh1_p
        hs = rmsnorm(xs, norm_w[l, 0]) * (1 + sc1_s) + sh1_s
        if l % 2 == 0:
            i = l // 2
            rw = (rwkv_mu[i], rwkv_w0[i], rwkv_w_decay_up[i], rwkv_a0[i], rwkv_w_aaa_up[i], rwkv_w_gate_up[i],
                  rwkv_k_k[i], rwkv_k_a[i], rwkv_r_k[i], rwkv_lnx_w[i], rwkv_lnx_b[i])
            o_p, S_p, row_p, kw_p, vw_p = ab_mixer(
                hp, pos_p, jnp.zeros((Bp, P_A), hp.dtype), jnp.zeros((Bp, H_A, HD_A, HD_A), jnp.float32),
                None, None, ab_w_in[i], ab_w_out[i], rw, swa_sinks[i])
            o_s, S_s, row_s, kw_s, vw_s = ab_mixer(
                hs, pos_s, state_rwkv_shift[i], state_rwkv[i], cache_swa_k[i], cache_swa_v[i],
                ab_w_in[i], ab_w_out[i], rw, swa_sinks[i])
            rw_S_p.append(S_p); rw_sh_p.append(row_p); sw_k_p.append(kw_p); sw_v_p.append(vw_p)
            rw_S_s.append(S_s); rw_sh_s.append(row_s); sw_k_s.append(kw_s); sw_v_s.append(vw_s)
        else:
            j = l // 2
            q_p, k_p, v_p, lf_p = fox_project(hp, fox_w_in[j], fox_b_f[j])
            q_s, k_s, v_s, lf_s = fox_project(hs, fox_w_in[j], fox_b_f[j])
            o_p = fox_prompt(q_p, k_p, v_p, lf_p) @ fox_w_out[j]
            o_s = fox_sample(q_s, k_s, v_s, lf_s, cache_fox_k, cache_fox_v, cache_fox_logf, j, page_table) @ fox_w_out[j]
            fk_p.append(k_p); fv_p.append(v_p); flf_p.append(lf_p)
            fk_s.append(k_s); fv_s.append(v_s); flf_s.append(lf_s)
        xp = xp + g1_p * o_p
        xs = xs + g1_s * o_s
        h2p = rmsnorm(xp, norm_w[l, 1]) * (1 + sc2_p) + sh2_p
        h2s = rmsnorm(xs, norm_w[l, 1]) * (1 + sc2_s) + sh2_s
        tok = jnp.concatenate([h2p.reshape(Bp * Tp, D), h2s.reshape(Bs * Ts, D)], axis=0)
        y = moe_ffn(tok, l, moe_w_router, moe_b_router, moe_w_gate, moe_w_up, moe_w_down,
                    shared_w_gate, shared_w_up, shared_w_down)
        xp = xp + g2_p * y[:Bp * Tp].reshape(Bp, Tp, D)
        xs = xs + g2_s * y[Bp * Tp:].reshape(Bs, Ts, D)
    y_prompt = rmsnorm(xp, final_norm_w)
    y_sample = rmsnorm(xs, final_norm_w)
    return (y_prompt, y_sample,
            jnp.stack(rw_S_p), jnp.stack(rw_sh_p), jnp.stack(sw_k_p), jnp.stack(sw_v_p),
            jnp.stack(fk_p), jnp.stack(fv_p), jnp.stack(flf_p),
            jnp.stack(rw_S_s), jnp.stack(rw_sh_s), jnp.stack(sw_k_s), jnp.stack(sw_v_s),
            jnp.stack(fk_s), jnp.stack(fv_s), jnp.stack(flf_s))
```

```python
import functools

import jax
import jax.numpy as jnp
from jax import lax
from jax.experimental import pallas as pl
from jax.experimental.pallas import tpu as pltpu

D_MODEL = 4096
PAGE_SIZE = 128

H_A = 32
HD_A = 64
C_A = H_A * HD_A
LORA_DECAY = 128
LORA_AAA = 128
LORA_GATE = 480
P_A = 3 * C_A + LORA_DECAY + LORA_AAA + LORA_GATE
RWKV_SPLIT = (C_A, 2 * C_A, 3 * C_A, 3 * C_A + LORA_DECAY, 3 * C_A + LORA_DECAY + LORA_AAA)
GN_EPS_A = 64e-5

H_B = 32
KVH_B = 4
G_B = H_B // KVH_B
HD_B = 64
C_B = H_B * HD_B
KV_B = KVH_B * HD_B
WINDOW = 128
ROPE_THETA = 10000.0

H_C = 32
KVH_C = 8
G_C = H_C // KVH_C
HD_C = 128
C_C = H_C * HD_C
KV_C = KVH_C * HD_C
Q_BLOCK = 128

N_EXPERTS = 64
TOP_K = 6
N_GROUPS = 8
TOPK_GROUPS = 4
ROUTED_SCALE = 2.5
MOE_BLOCK = 128

NORM_EPS = 1e-6

VMEM_LIMIT_BYTES = 48 * 1024 * 1024


def _mm_body(x_ref, w_ref, o_ref, acc_ref):
    k = pl.program_id(2)

    @pl.when(k == 0)
    def _():
        acc_ref[...] = jnp.zeros_like(acc_ref)

    acc_ref[...] += jnp.dot(x_ref[...].astype(jnp.bfloat16), w_ref[...].astype(jnp.bfloat16),
                            preferred_element_type=jnp.float32)

    @pl.when(k == pl.num_programs(2) - 1)
    def _():
        o_ref[...] = acc_ref[...]


def _pick_tile(n, candidates):
    for c in candidates:
        if n % c == 0:
            return c
    return n


def matmul(x, w):
    M, K = x.shape
    _, N = w.shape
    tm = _pick_tile(M, (1024, 512, 256))
    tn = _pick_tile(N, (1024, 512, 256, 128))
    tk = _pick_tile(K, (1024, 512))
    return pl.pallas_call(
        _mm_body,
        grid=(M // tm, N // tn, K // tk),
        in_specs=[pl.BlockSpec((tm, tk), lambda i, j, k: (i, k)),
                  pl.BlockSpec((tk, tn), lambda i, j, k: (k, j))],
        out_specs=pl.BlockSpec((tm, tn), lambda i, j, k: (i, j)),
        out_shape=jax.ShapeDtypeStruct((M, N), jnp.float32),
        scratch_shapes=[pltpu.VMEM((tm, tn), jnp.float32)],
        compiler_params=pltpu.CompilerParams(
            dimension_semantics=("parallel", "parallel", "arbitrary"),
            vmem_limit_bytes=VMEM_LIMIT_BYTES),
        name="dense_matmul",
    )(x, w)


def mm3(h, w):
    B, T, K = h.shape
    return matmul(h.reshape(B * T, K), w).reshape(B, T, w.shape[1])


def rmsnorm(x, w):
    y = x * lax.rsqrt(jnp.mean(x * x, axis=-1, keepdims=True) + NORM_EPS)
    return y * w


def ada_params(c, w, b):
    m = matmul(jax.nn.silu(c), w) + b
    return jnp.split(m[:, None, :], 6, axis=-1)


def rope(x, pos):
    half = x.shape[-1] // 2
    inv = ROPE_THETA ** (-jnp.arange(half, dtype=jnp.float32) / half)
    ang = pos.astype(jnp.float32)[:, None] * inv[None, :]
    cos = jnp.cos(ang)[None, :, None, :]
    sin = jnp.sin(ang)[None, :, None, :]
    x1, x2 = x[..., :half], x[..., half:]
    return jnp.concatenate([x1 * cos - x2 * sin, x2 * cos + x1 * sin], axis=-1)


def rwkv7_scan(S0, r, decay, k, v, kk, a):
    def step(S, inp):
        r_t, d_t, k_t, v_t, kk_t, a_t = inp
        sa = jnp.einsum('bhvk,bhk->bhv', S, kk_t)
        S = (S * d_t[:, :, None, :] - sa[..., None] * (kk_t * a_t)[:, :, None, :]
             + v_t[..., None] * k_t[:, :, None, :])
        return S, jnp.einsum('bhvk,bhk->bhv', S, r_t)
    xs = tuple(jnp.swapaxes(t, 0, 1) for t in (r, decay, k, v, kk, a))
    S, ys = lax.scan(step, S0, xs)
    return jnp.swapaxes(ys, 0, 1), S


def rwkv7_mixer(pa, prev_row, S0, mu, w0, w_dec_up, a0, w_aaa_up, w_gate_up, k_k, k_a, r_k, lnx_w, lnx_b):
    B, T, _ = pa.shape
    shifted = jnp.concatenate([prev_row[:, None, :], pa[:, :-1]], axis=1)
    m = pa + (shifted - pa) * mu
    r, k, v, wd, ad, gd = jnp.split(m, RWKV_SPLIT, axis=-1)
    w = -jax.nn.softplus(-(w0 + mm3(jnp.tanh(wd), w_dec_up))) - 0.5
    a = jax.nn.sigmoid(a0 + mm3(ad, w_aaa_up))
    g = mm3(jax.nn.sigmoid(gd), w_gate_up)
    heads = lambda t: t.reshape(B, T, H_A, HD_A)
    kk = heads(k * k_k)
    kk = kk / jnp.maximum(jnp.sqrt(jnp.sum(kk * kk, axis=-1, keepdims=True)), 1e-12)
    k = k * (1 + (a - 1) * k_a)
    rh, kh, vh, ah = heads(r), heads(k), heads(v), heads(a)
    decay = jnp.exp(-jnp.exp(heads(w)))
    y, S = rwkv7_scan(S0, rh, decay, kh, vh, kk, ah)
    mean = jnp.mean(y, axis=-1, keepdims=True)
    var = jnp.mean(jnp.square(y - mean), axis=-1, keepdims=True)
    y = ((y - mean) * lax.rsqrt(var + GN_EPS_A)).reshape(B, T, C_A) * lnx_w + lnx_b
    bonus = (jnp.sum(rh * kh * r_k, axis=-1, keepdims=True) * vh).reshape(B, T, C_A)
    out = (y + bonus) * g
    return out, S, pa[:, -1]


def sink_softmax(s, sink):
    sk = jnp.broadcast_to(sink[:, :, None], s.shape[:-1])[..., None]
    return jax.nn.softmax(jnp.concatenate([s, sk], axis=-1), axis=-1)[..., :-1]


def swa_prompt(q, k, v, sinks):
    B, T = q.shape[:2]
    nb = T // WINDOW
    qb = q.reshape(B, nb, WINDOW, KVH_B, G_B, HD_B)
    kb = k.reshape(B, nb, WINDOW, KVH_B, HD_B)
    vb = v.reshape(B, nb, WINDOW, KVH_B, HD_B)
    pad = ((0, 0), (1, 0), (0, 0), (0, 0), (0, 0))
    kc = jnp.concatenate([jnp.pad(kb, pad)[:, :-1], kb], axis=2)
    vc = jnp.concatenate([jnp.pad(vb, pad)[:, :-1], vb], axis=2)
    s = jnp.einsum('bnqhgd,bnkhd->bnhgqk', qb, kc) * (HD_B ** -0.5)
    qi = jnp.arange(WINDOW)[:, None]
    ki = jnp.arange(2 * WINDOW)[None, :] - WINDOW
    band = (ki <= qi) & (qi - ki < WINDOW)
    valid = band[None] & ((jnp.arange(nb)[:, None, None] * WINDOW + ki[None]) >= 0)
    s = jnp.where(valid[None, :, None, None], s, -jnp.inf)
    p = sink_softmax(s, sinks.reshape(KVH_B, G_B))
    o = jnp.einsum('bnhgqk,bnkhd->bnqhgd', p, vc)
    return o.reshape(B, T, C_B), k[:, -WINDOW:], v[:, -WINDOW:]


def swa_sample(q, k, v, kbuf, vbuf, sinks):
    B, T = q.shape[:2]
    kc = jnp.concatenate([kbuf, k], axis=1)
    vc = jnp.concatenate([vbuf, v], axis=1)
    kpos = jnp.arange(WINDOW + T) - WINDOW
    qpos = jnp.arange(T)
    mask = (kpos[None, :] <= qpos[:, None]) & (qpos[:, None] - kpos[None, :] < WINDOW)
    s = jnp.einsum('bqhgd,bkhd->bhgqk', q.reshape(B, T, KVH_B, G_B, HD_B), kc) * (HD_B ** -0.5)
    s = jnp.where(mask, s, -jnp.inf)
    p = sink_softmax(s, sinks.reshape(KVH_B, G_B))
    o = jnp.einsum('bhgqk,bkhd->bqhgd', p, vc)
    return o.reshape(B, T, C_B), kc[:, -WINDOW:], vc[:, -WINDOW:]


def ab_mixer(h, pos, prev_row, S0, kbuf, vbuf, w_in, w_out, rwkv_params, sinks):
    B, T, _ = h.shape
    h2 = h.reshape(B * T, -1)
    w_parts = (w_in[:, :3 * C_A], w_in[:, 3 * C_A:P_A], w_in[:, P_A:P_A + C_B], w_in[:, P_A + C_B:])
    pa = jnp.concatenate([matmul(h2, w_parts[0]), matmul(h2, w_parts[1])], axis=-1).reshape(B, T, P_A)
    q = matmul(h2, w_parts[2]).reshape(B, T, C_B)
    kv = matmul(h2, w_parts[3]).reshape(B, T, 2 * KV_B)
    k, v = kv[..., :KV_B], kv[..., KV_B:]
    o_a, S, last_row = rwkv7_mixer(pa, prev_row, S0, *rwkv_params)
    q = rope(q.reshape(B, T, H_B, HD_B), pos)
    k = rope(k.reshape(B, T, KVH_B, HD_B), pos)
    v = v.reshape(B, T, KVH_B, HD_B)
    if kbuf is None:
        o_b, kw, vw = swa_prompt(q, k, v, sinks)
    else:
        o_b, kw, vw = swa_sample(q, k, v, kbuf, vbuf, sinks)
    out = mm3(jnp.concatenate([o_a, o_b], axis=-1), w_out)
    return out, S, last_row, kw, vw


def fox_project(h, w_in, b_f):
    B, T, _ = h.shape
    h2 = h.reshape(B * T, -1)
    q = matmul(h2, w_in[:, :C_C])
    kv = matmul(h2, w_in[:, C_C:C_C + 2 * KV_C])
    fl = matmul(h2, w_in[:, C_C + 2 * KV_C:])
    logf = jax.nn.log_sigmoid(fl + b_f).reshape(B, T, H_C)
    return (q.reshape(B, T, H_C, HD_C), kv[:, :KV_C].reshape(B, T, KVH_C, HD_C),
            kv[:, KV_C:].reshape(B, T, KVH_C, HD_C), logf)


def fox_prompt(q, k, v, logf):
    B, T = q.shape[:2]
    nb = T // Q_BLOCK
    F = jnp.cumsum(logf, axis=1)
    Fk = jnp.transpose(F.reshape(B, T, KVH_C, G_C), (0, 2, 3, 1))
    qb = jnp.moveaxis(q.reshape(B, nb, Q_BLOCK, KVH_C, G_C, HD_C), 1, 0)
    Fq = jnp.moveaxis(Fk.reshape(B, KVH_C, G_C, nb, Q_BLOCK), 3, 0)
    kpos = jnp.arange(T)

    def block(args):
        q_blk, fq, start = args
        s = jnp.einsum('bqhgd,bkhd->bhgqk', q_blk, k) * (HD_C ** -0.5)
        s = s + fq[..., None] - Fk[:, :, :, None, :]
        qpos = start + jnp.arange(Q_BLOCK)
        s = jnp.where(kpos[None, :] <= qpos[:, None], s, -jnp.inf)
        p = jax.nn.softmax(s, axis=-1)
        return jnp.einsum('bhgqk,bkhd->bqhgd', p, v)

    o = lax.map(block, (qb, Fq, jnp.arange(nb) * Q_BLOCK))
    return jnp.moveaxis(o, 0, 1).reshape(B, T, C_C)


def fox_sample(q, k, v, logf, cache_k, cache_v, cache_lf, layer, page_table):
    B, T = q.shape[:2]
    past_len = page_table.shape[1] * PAGE_SIZE
    L = past_len + T
    kpos = jnp.arange(L)
    qpos = past_len + jnp.arange(T)
    causal = kpos[None, :] <= qpos[:, None]

    def one(args):
        pt, q_s, k_s, v_s, lf_s = args
        k_all = jnp.concatenate([cache_k[layer, pt].reshape(past_len, KVH_C, HD_C), k_s], axis=0)
        v_all = jnp.concatenate([cache_v[layer, pt].reshape(past_len, KVH_C, HD_C), v_s], axis=0)
        lf_all = jnp.concatenate([cache_lf[layer, pt].reshape(past_len, H_C), lf_s], axis=0)
        F = jnp.transpose(jnp.cumsum(lf_all, axis=0).reshape(L, KVH_C, G_C), (1, 2, 0))
        s = jnp.einsum('qhgd,khd->hgqk', q_s.reshape(T, KVH_C, G_C, HD_C), k_all) * (HD_C ** -0.5)
        s = s + F[:, :, past_len:, None] - F[:, :, None, :]
        s = jnp.where(causal, s, -jnp.inf)
        p = jax.nn.softmax(s, axis=-1)
        return jnp.einsum('hgqk,khd->qhgd', p, v_all).reshape(T, C_C)

    return lax.map(one, (page_table, q, k, v, logf))


def moe_ffn(h, layer, w_router, b_router, w_gate, w_up, w_down, ws_gate, ws_up, ws_down):
    T = h.shape[0]
    per_group = N_EXPERTS // N_GROUPS
    scores = jax.nn.sigmoid(matmul(h, w_router[layer]))
    biased = scores + b_router[layer]
    grp = jnp.sum(lax.top_k(biased.reshape(T, N_GROUPS, per_group), 2)[0], axis=-1)
    gidx = lax.top_k(grp, TOPK_GROUPS)[1]
    gmask = jnp.any(gidx[:, :, None] == jnp.arange(N_GROUPS)[None, None, :], axis=1)
    biased = jnp.where(jnp.repeat(gmask, per_group, axis=1), biased, -jnp.inf)
    eidx = lax.top_k(biased, TOP_K)[1]
    gates = jnp.take_along_axis(scores, eidx, axis=1)
    gates = gates / jnp.sum(gates, axis=-1, keepdims=True) * ROUTED_SCALE

    A = T * TOP_K
    e_flat = eidx.reshape(A).astype(jnp.int32)
    t_flat = jnp.repeat(jnp.arange(T, dtype=jnp.int32), TOP_K)
    g_flat = gates.reshape(A)
    order = jnp.argsort(e_flat)
    e_sorted = e_flat[order]
    counts = jnp.zeros((N_EXPERTS,), jnp.int32).at[e_flat].add(1)
    padded = (counts + MOE_BLOCK - 1) // MOE_BLOCK * MOE_BLOCK
    seg_start = jnp.cumsum(counts) - counts
    pad_end = jnp.cumsum(padded)
    pad_start = pad_end - padded
    dest = pad_start[e_sorted] + jnp.arange(A, dtype=jnp.int32) - seg_start[e_sorted]
    n_blocks = -(-(A + N_EXPERTS * (MOE_BLOCK - 1)) // MOE_BLOCK)
    n_rows = n_blocks * MOE_BLOCK
    row_tok = jnp.zeros((n_rows,), jnp.int32).at[dest].set(t_flat[order])
    row_w = jnp.zeros((n_rows,), jnp.float32).at[dest].set(g_flat[order])
    blk_exp = jnp.minimum(jnp.searchsorted(pad_end, jnp.arange(n_blocks, dtype=jnp.int32) * MOE_BLOCK, side='right'),
                          N_EXPERTS - 1).astype(jnp.int32)

    def expert_block(args):
        tok, e = args
        xb = h[tok]
        act = jax.nn.silu(xb @ w_gate[layer, e]) * (xb @ w_up[layer, e])
        return act @ w_down[layer, e]

    y_rows = lax.map(expert_block, (row_tok.reshape(n_blocks, MOE_BLOCK), blk_exp))
    y_rows = y_rows.reshape(n_rows, -1) * row_w[:, None]
    routed = jax.ops.segment_sum(y_rows, row_tok, num_segments=T)
    shared = matmul(jax.nn.silu(matmul(h, ws_gate[layer])) * matmul(h, ws_up[layer]), ws_down[layer])
    return routed + shared


def kernel(x_prompt, x_sample, c_prompt, c_sample, state_rwkv, state_rwkv_shift, cache_swa_k, cache_swa_v, cache_fox_k, cache_fox_v, cache_fox_logf, page_table, ada_w, ada_b, norm_w, final_norm_w, ab_w_in, ab_w_out, rwkv_mu, rwkv_w0, rwkv_w_decay_up, rwkv_a0, rwkv_w_aaa_up, rwkv_w_gate_up, rwkv_k_k, rwkv_k_a, rwkv_r_k, rwkv_lnx_w, rwkv_lnx_b, swa_sinks, fox_w_in, fox_b_f, fox_w_out, moe_w_router, moe_b_router, moe_w_gate, moe_w_up, moe_w_down, shared_w_gate, shared_w_up, shared_w_down):
    Bp, Tp, D = x_prompt.shape
    Bs, Ts, _ = x_sample.shape
    depth = ada_w.shape[0]
    past_len = page_table.shape[1] * PAGE_SIZE
    pos_p = jnp.arange(Tp)
    pos_s = past_len + jnp.arange(Ts)
    xp, xs = x_prompt, x_sample
    rw_S_p, rw_sh_p, sw_k_p, sw_v_p, fk_p, fv_p, flf_p = [], [], [], [], [], [], []
    rw_S_s, rw_sh_s, sw_k_s, sw_v_s, fk_s, fv_s, flf_s = [], [], [], [], [], [], []
    for l in range(depth):
        sh1_p, sc1_p, g1_p, sh2_p, sc2_p, g2_p = ada_params(c_prompt, ada_w[l], ada_b[l])
        sh1_s, sc1_s, g1_s, sh2_s, sc2_s, g2_s = ada_params(c_sample, ada_w[l], ada_b[l])
        hp = rmsnorm(xp, norm_w[l, 0]) * (1 + sc1_p) + sh1_p
        hs = rmsnorm(xs, norm_w[l, 0]) * (1 + sc1_s) + sh1_s
        if l % 2 == 0:
            i = l // 2
            rw = (rwkv_mu[i], rwkv_w0[i], rwkv_w_decay_up[i], rwkv_a0[i], rwkv_w_aaa_up[i], rwkv_w_gate_up[i],
                  rwkv_k_k[i], rwkv_k_a[i], rwkv_r_k[i], rwkv_lnx_w[i], rwkv_lnx_b[i])
            o_p, S_p, row_p, kw_p, vw_p = ab_mixer(
                hp, pos_p, jnp.zeros((Bp, P_A), hp.dtype), jnp.zeros((Bp, H_A, HD_A, HD_A), jnp.float32),
                None, None, ab_w_in[i], ab_w_out[i], rw, swa_sinks[i])
            o_s, S_s, row_s, kw_s, vw_s = ab_mixer(
                hs, pos_s, state_rwkv_shift[i], state_rwkv[i], cache_swa_k[i], cache_swa_v[i],
                ab_w_in[i], ab_w_out[i], rw, swa_sinks[i])
            rw_S_p.append(S_p); rw_sh_p.append(row_p); sw_k_p.append(kw_p); sw_v_p.append(vw_p)
            rw_S_s.append(S_s); rw_sh_s.append(row_s); sw_k_s.append(kw_s); sw_v_s.append(vw_s)
        else:
            j = l // 2
            q_p, k_p, v_p, lf_p = fox_project(hp, fox_w_in[j], fox_b_f[j])
            q_s, k_s, v_s, lf_s = fox_project(hs, fox_w_in[j], fox_b_f[j])
            o_p = mm3(fox_prompt(q_p, k_p, v_p, lf_p), fox_w_out[j])
            o_s = mm3(fox_sample(q_s, k_s, v_s, lf_s, cache_fox_k, cache_fox_v, cache_fox_logf, j, page_table),
                      fox_w_out[j])
            fk_p.append(k_p); fv_p.append(v_p); flf_p.append(lf_p)
            fk_s.append(k_s); fv_s.append(v_s); flf_s.append(lf_s)
        xp = xp + g1_p * o_p
        xs = xs + g1_s * o_s
        h2p = rmsnorm(xp, norm_w[l, 1]) * (1 + sc2_p) + sh2_p
        h2s = rmsnorm(xs, norm_w[l, 1]) * (1 + sc2_s) + sh2_s
        tok = jnp.concatenate([h2p.reshape(Bp * Tp, D), h2s.reshape(Bs * Ts, D)], axis=0)
        y = moe_ffn(tok, l, moe_w_router, moe_b_router, moe_w_gate, moe_w_up, moe_w_down,
                    shared_w_gate, shared_w_up, shared_w_down)
        xp = xp + g2_p * y[:Bp * Tp].reshape(Bp, Tp, D)
        xs = xs + g2_s * y[Bp * Tp:].reshape(Bs, Ts, D)
    y_prompt = rmsnorm(xp, final_norm_w)
    y_sample = rmsnorm(xs, final_norm_w)
    return (y_prompt, y_sample,
            jnp.stack(rw_S_p), jnp.stack(rw_sh_p), jnp.stack(sw_k_p), jnp.stack(sw_v_p),
            jnp.stack(fk_p), jnp.stack(fv_p), jnp.stack(flf_p),
            jnp.stack(rw_S_s), jnp.stack(rw_sh_s), jnp.stack(sw_k_s), jnp.stack(sw_v_s),
            jnp.stack(fk_s), jnp.stack(fv_s), jnp.stack(flf_s))
```

```python
import functools

import jax
import jax.numpy as jnp
from jax import lax
from jax.experimental import pallas as pl
from jax.experimental.pallas import tpu as pltpu

D_MODEL = 4096
PAGE_SIZE = 128

H_A = 32
HD_A = 64
C_A = H_A * HD_A
LORA_DECAY = 128
LORA_AAA = 128
LORA_GATE = 480
P_A = 3 * C_A + LORA_DECAY + LORA_AAA + LORA_GATE
RWKV_SPLIT = (C_A, 2 * C_A, 3 * C_A, 3 * C_A + LORA_DECAY, 3 * C_A + LORA_DECAY + LORA_AAA)
GN_EPS_A = 64e-5

H_B = 32
KVH_B = 4
G_B = H_B // KVH_B
HD_B = 64
C_B = H_B * HD_B
KV_B = KVH_B * HD_B
WINDOW = 128
ROPE_THETA = 10000.0

H_C = 32
KVH_C = 8
G_C = H_C // KVH_C
HD_C = 128
C_C = H_C * HD_C
KV_C = KVH_C * HD_C
Q_BLOCK = 128

N_EXPERTS = 64
TOP_K = 6
N_GROUPS = 8
TOPK_GROUPS = 4
ROUTED_SCALE = 2.5
MOE_BLOCK = 128

NORM_EPS = 1e-6

VMEM_LIMIT_BYTES = 48 * 1024 * 1024


def _mm_body(x_ref, w_ref, o_ref, acc_ref):
    k = pl.program_id(2)

    @pl.when(k == 0)
    def _():
        acc_ref[...] = jnp.zeros_like(acc_ref)

    acc_ref[...] += jnp.dot(x_ref[...].astype(jnp.bfloat16), w_ref[...].astype(jnp.bfloat16),
                            preferred_element_type=jnp.float32)

    @pl.when(k == pl.num_programs(2) - 1)
    def _():
        o_ref[...] = acc_ref[...]


def _pick_tile(n, candidates):
    for c in candidates:
        if n % c == 0:
            return c
    return n


def matmul(x, w):
    M, K = x.shape
    _, N = w.shape
    tm = _pick_tile(M, (1024, 512, 256))
    tn = _pick_tile(N, (1024, 512, 256, 128))
    tk = _pick_tile(K, (1024, 512))
    return pl.pallas_call(
        _mm_body,
        grid=(M // tm, N // tn, K // tk),
        in_specs=[pl.BlockSpec((tm, tk), lambda i, j, k: (i, k)),
                  pl.BlockSpec((tk, tn), lambda i, j, k: (k, j))],
        out_specs=pl.BlockSpec((tm, tn), lambda i, j, k: (i, j)),
        out_shape=jax.ShapeDtypeStruct((M, N), jnp.float32),
        scratch_shapes=[pltpu.VMEM((tm, tn), jnp.float32)],
        compiler_params=pltpu.CompilerParams(
            dimension_semantics=("parallel", "parallel", "arbitrary"),
            vmem_limit_bytes=VMEM_LIMIT_BYTES),
        name="dense_matmul",
    )(x, w)


def mm3(h, w):
    B, T, K = h.shape
    return matmul(h.reshape(B * T, K), w).reshape(B, T, w.shape[1])


HD = HD_A
PAIR = 2 * HD
RWKV_CHUNK = 64
RWKV_PAIRS_PER_STEP = 4


def _bf(x):
    return x.astype(jnp.bfloat16)


def _dot(a, b):
    return jnp.dot(_bf(a), _bf(b), preferred_element_type=jnp.float32)


def _dot_nt(a, b):
    return lax.dot_general(_bf(a), _bf(b), (((1,), (1,)), ((), ())), preferred_element_type=jnp.float32)


def _split3(x):
    h = x.astype(jnp.bfloat16)
    r1 = x - h.astype(jnp.float32)
    m = r1.astype(jnp.bfloat16)
    l = (r1 - m.astype(jnp.float32)).astype(jnp.bfloat16)
    return h, m, l


def _rwkv_body(s0_ref, r_ref, lw_ref, k_ref, v_ref, kk_ref, b_ref, y_ref, sT_ref, s_scr, *, C, hp):
    c = pl.program_id(2)

    @pl.when(c == 0)
    def _():
        s_scr[...] = s0_ref[...]

    n = 2 * C
    row = lax.broadcasted_iota(jnp.int32, (n, n), 0)
    col = lax.broadcasted_iota(jnp.int32, (n, n), 1)
    same_head = (row >= C) == (col >= C)
    tt = row & (C - 1)
    ss = col & (C - 1)
    strict = same_head & (ss < tt)
    incl = same_head & (ss <= tt)
    trow = lax.broadcasted_iota(jnp.int32, (C, C), 0)
    tcol = lax.broadcasted_iota(jnp.int32, (C, C), 1)
    tri = jnp.where(tcol <= trow, 1.0, 0.0).astype(jnp.bfloat16)
    lane = lax.broadcasted_iota(jnp.int32, (1, PAIR), 1)
    lo = jnp.where(lane < HD, 1.0, 0.0)
    hi = 1.0 - lo
    vrow = lax.broadcasted_iota(jnp.int32, (PAIR, PAIR), 0)
    vcol = lax.broadcasted_iota(jnp.int32, (PAIR, PAIR), 1)
    bd = (vrow >= HD) == (vcol >= HD)
    eye = jnp.where(row == col, 1.0, 0.0)
    nlev = C.bit_length() - 1

    P = range(hp)
    sls = [slice(p * PAIR, (p + 1) * PAIR) for p in P]
    lw = [lw_ref[:, s] for s in sls]
    r = [r_ref[:, s] for s in sls]
    k = [k_ref[:, s] for s in sls]
    v = [v_ref[:, s] for s in sls]
    kk = [kk_ref[:, s] for s in sls]
    b = [b_ref[:, s] for s in sls]
    csum = lambda x: jnp.dot(tri, x, preferred_element_type=jnp.float32)
    w3 = [_split3(x) for x in lw]
    Lc = [csum(h) + (csum(m) + csum(l)) for h, m, l in w3]
    e_inc = [jnp.exp(x) for x in Lc]
    e_neg = [jnp.exp(-x) for x in Lc]
    e_end = [jnp.exp(x[C - 1:C, :] - x) for x in Lc]
    Kq = [kk[p] * jnp.exp(Lc[p] - lw[p]) for p in P]
    Rq = [r[p] * e_inc[p] for p in P]
    Bd = [b[p] * e_neg[p] for p in P]
    Kd = [k[p] * e_neg[p] for p in P]
    G = [_dot_nt(jnp.concatenate([Kq[p] * lo, Kq[p] * hi, Rq[p] * lo, Rq[p] * hi], axis=0),
                 jnp.concatenate([Bd[p], Bd[p], Kd[p], Kd[p]], axis=0)) for p in P]
    Mb = [jnp.where(strict, g[:n, :n], 0.0) for g in G]
    Mk = [jnp.where(strict, g[:n, n:], 0.0) for g in G]
    Nb = [jnp.where(incl, g[n:, :n], 0.0) for g in G]
    Nk = [jnp.where(incl, g[n:, n:], 0.0) for g in G]
    T = [eye - jnp.where((tt >> 1) == (ss >> 1), m, 0.0) for m in Mb]
    for j in range(2, nlev + 1):
        lower_left = ((tt >> j) == (ss >> j)) & ((tt >> (j - 1)) > (ss >> (j - 1)))
        TM = [_dot(T[p], jnp.where(lower_left, Mb[p], 0.0)) for p in P]
        T = [T[p] - _dot(TM[p], T[p]) for p in P]
    V2 = [jnp.concatenate([x * lo, x * hi], axis=0) for x in v]
    MkV = [_dot(Mk[p], V2[p]) for p in P]
    NkV = [_dot(Nk[p], V2[p]) for p in P]
    BK = [jnp.concatenate([b[p] * e_end[p], k[p] * e_end[p]], axis=0) for p in P]
    S = [s_scr[p] for p in P]
    W0 = [_dot_nt(jnp.concatenate([Kq[p], Rq[p]], axis=0), S[p]) for p in P]
    rhs_u = [-(W0[p][:C] + (MkV[p][:C] + MkV[p][C:])) for p in P]
    UU = [_dot(T[p], jnp.concatenate([rhs_u[p] * lo, rhs_u[p] * hi], axis=0)) for p in P]
    YY = [_dot(Nb[p], UU[p]) + NkV[p] for p in P]
    UV = [jnp.concatenate([UU[p][:C] + UU[p][C:], v[p]], axis=0) for p in P]
    upd = [_dot(UV[p].T, BK[p]) for p in P]
    for p in P:
        y_ref[:, sls[p]] = YY[p][:C] + YY[p][C:] + W0[p][C:]
        S_new = S[p] * e_inc[p][C - 1:C, :] + jnp.where(bd, upd[p], 0.0)
        s_scr[p] = S_new
        sT_ref[p] = S_new


def rwkv7_scan(S0, r, lw, k, v, kk, b):
    B, T, CA = r.shape
    H = CA // HD
    npair = H // 2
    C = RWKV_CHUNK
    hp = min(RWKV_PAIRS_PER_STEP, npair)
    Tp = -(-T // C) * C
    if Tp != T:
        padf = lambda x: jnp.pad(x, ((0, 0), (0, Tp - T), (0, 0)))
        r, lw, k, v, kk, b = map(padf, (r, lw, k, v, kk, b))
    S0p = S0.reshape(B, npair, 2, HD, HD)
    z = jnp.zeros_like(S0p[:, :, 0])
    S0bd = jnp.concatenate([jnp.concatenate([S0p[:, :, 0], z], axis=-1),
                            jnp.concatenate([z, S0p[:, :, 1]], axis=-1)], axis=-2)
    seq_spec = pl.BlockSpec((None, C, hp * PAIR), lambda bb, g, c: (bb, c, g))
    st_spec = pl.BlockSpec((None, hp, PAIR, PAIR), lambda bb, g, c: (bb, g, 0, 0))
    y, Sbd = pl.pallas_call(
        functools.partial(_rwkv_body, C=C, hp=hp),
        grid=(B, npair // hp, Tp // C),
        in_specs=[st_spec] + [seq_spec] * 6,
        out_specs=[seq_spec, st_spec],
        out_shape=[jax.ShapeDtypeStruct((B, Tp, CA), jnp.float32),
                   jax.ShapeDtypeStruct((B, npair, PAIR, PAIR), jnp.float32)],
        scratch_shapes=[pltpu.VMEM((hp, PAIR, PAIR), jnp.float32)],
        compiler_params=pltpu.CompilerParams(
            dimension_semantics=("parallel", "parallel", "arbitrary")),
        name="rwkv7_chunk_scan",
    )(S0bd, r, lw, k, v, kk, b)
    S = jnp.stack([Sbd[:, :, :HD, :HD], Sbd[:, :, HD:, HD:]], axis=2).reshape(B, H, HD, HD)
    return y[:, :T], S


def rmsnorm(x, w):
    y = x * lax.rsqrt(jnp.mean(x * x, axis=-1, keepdims=True) + NORM_EPS)
    return y * w


def ada_params(c, w, b):
    m = matmul(jax.nn.silu(c), w) + b
    return jnp.split(m[:, None, :], 6, axis=-1)


def rope(x, pos):
    half = x.shape[-1] // 2
    inv = ROPE_THETA ** (-jnp.arange(half, dtype=jnp.float32) / half)
    ang = pos.astype(jnp.float32)[:, None] * inv[None, :]
    cos = jnp.cos(ang)[None, :, None, :]
    sin = jnp.sin(ang)[None, :, None, :]
    x1, x2 = x[..., :half], x[..., half:]
    return jnp.concatenate([x1 * cos - x2 * sin, x2 * cos + x1 * sin], axis=-1)


def rwkv7_mixer(pa, prev_row, S0, mu, w0, w_dec_up, a0, w_aaa_up, w_gate_up, k_k, k_a, r_k, lnx_w, lnx_b):
    B, T, _ = pa.shape
    shifted = jnp.concatenate([prev_row[:, None, :], pa[:, :-1]], axis=1)
    m = pa + (shifted - pa) * mu
    r, k, v, wd, ad, gd = jnp.split(m, RWKV_SPLIT, axis=-1)
    w = -jax.nn.softplus(-(w0 + mm3(jnp.tanh(wd), w_dec_up))) - 0.5
    a = jax.nn.sigmoid(a0 + mm3(ad, w_aaa_up))
    g = mm3(jax.nn.sigmoid(gd), w_gate_up)
    heads = lambda t: t.reshape(B, T, H_A, HD_A)
    kk = heads(k * k_k)
    kk = kk / jnp.maximum(jnp.sqrt(jnp.sum(kk * kk, axis=-1, keepdims=True)), 1e-12)
    k = k * (1 + (a - 1) * k_a)
    rh, kh, vh = heads(r), heads(k), heads(v)
    y, S = rwkv7_scan(S0, r, -jnp.exp(w), k, v, kk.reshape(B, T, C_A), kk.reshape(B, T, C_A) * a)
    y = heads(y)
    mean = jnp.mean(y, axis=-1, keepdims=True)
    var = jnp.mean(jnp.square(y - mean), axis=-1, keepdims=True)
    y = ((y - mean) * lax.rsqrt(var + GN_EPS_A)).reshape(B, T, C_A) * lnx_w + lnx_b
    bonus = (jnp.sum(rh * kh * r_k, axis=-1, keepdims=True) * vh).reshape(B, T, C_A)
    out = (y + bonus) * g
    return out, S, pa[:, -1]


def sink_softmax(s, sink):
    sk = jnp.broadcast_to(sink[:, :, None], s.shape[:-1])[..., None]
    return jax.nn.softmax(jnp.concatenate([s, sk], axis=-1), axis=-1)[..., :-1]


def swa_prompt(q, k, v, sinks):
    B, T = q.shape[:2]
    nb = T // WINDOW
    qb = q.reshape(B, nb, WINDOW, KVH_B, G_B, HD_B)
    kb = k.reshape(B, nb, WINDOW, KVH_B, HD_B)
    vb = v.reshape(B, nb, WINDOW, KVH_B, HD_B)
    pad = ((0, 0), (1, 0), (0, 0), (0, 0), (0, 0))
    kc = jnp.concatenate([jnp.pad(kb, pad)[:, :-1], kb], axis=2)
    vc = jnp.concatenate([jnp.pad(vb, pad)[:, :-1], vb], axis=2)
    s = jnp.einsum('bnqhgd,bnkhd->bnhgqk', qb, kc) * (HD_B ** -0.5)
    qi = jnp.arange(WINDOW)[:, None]
    ki = jnp.arange(2 * WINDOW)[None, :] - WINDOW
    band = (ki <= qi) & (qi - ki < WINDOW)
    valid = band[None] & ((jnp.arange(nb)[:, None, None] * WINDOW + ki[None]) >= 0)
    s = jnp.where(valid[None, :, None, None], s, -jnp.inf)
    p = sink_softmax(s, sinks.reshape(KVH_B, G_B))
    o = jnp.einsum('bnhgqk,bnkhd->bnqhgd', p, vc)
    return o.reshape(B, T, C_B), k[:, -WINDOW:], v[:, -WINDOW:]


def swa_sample(q, k, v, kbuf, vbuf, sinks):
    B, T = q.shape[:2]
    kc = jnp.concatenate([kbuf, k], axis=1)
    vc = jnp.concatenate([vbuf, v], axis=1)
    kpos = jnp.arange(WINDOW + T) - WINDOW
    qpos = jnp.arange(T)
    mask = (kpos[None, :] <= qpos[:, None]) & (qpos[:, None] - kpos[None, :] < WINDOW)
    s = jnp.einsum('bqhgd,bkhd->bhgqk', q.reshape(B, T, KVH_B, G_B, HD_B), kc) * (HD_B ** -0.5)
    s = jnp.where(mask, s, -jnp.inf)
    p = sink_softmax(s, sinks.reshape(KVH_B, G_B))
    o = jnp.einsum('bhgqk,bkhd->bqhgd', p, vc)
    return o.reshape(B, T, C_B), kc[:, -WINDOW:], vc[:, -WINDOW:]


def ab_mixer(h, pos, prev_row, S0, kbuf, vbuf, w_in, w_out, rwkv_params, sinks):
    B, T, _ = h.shape
    h2 = h.reshape(B * T, -1)
    w_parts = (w_in[:, :3 * C_A], w_in[:, 3 * C_A:P_A], w_in[:, P_A:P_A + C_B], w_in[:, P_A + C_B:])
    pa = jnp.concatenate([matmul(h2, w_parts[0]), matmul(h2, w_parts[1])], axis=-1).reshape(B, T, P_A)
    q = matmul(h2, w_parts[2]).reshape(B, T, C_B)
    kv = matmul(h2, w_parts[3]).reshape(B, T, 2 * KV_B)
    k, v = kv[..., :KV_B], kv[..., KV_B:]
    o_a, S, last_row = rwkv7_mixer(pa, prev_row, S0, *rwkv_params)
    q = rope(q.reshape(B, T, H_B, HD_B), pos)
    k = rope(k.reshape(B, T, KVH_B, HD_B), pos)
    v = v.reshape(B, T, KVH_B, HD_B)
    if kbuf is None:
        o_b, kw, vw = swa_prompt(q, k, v, sinks)
    else:
        o_b, kw, vw = swa_sample(q, k, v, kbuf, vbuf, sinks)
    out = mm3(jnp.concatenate([o_a, o_b], axis=-1), w_out)
    return out, S, last_row, kw, vw


def fox_project(h, w_in, b_f):
    B, T, _ = h.shape
    h2 = h.reshape(B * T, -1)
    q = matmul(h2, w_in[:, :C_C])
    kv = matmul(h2, w_in[:, C_C:C_C + 2 * KV_C])
    fl = matmul(h2, w_in[:, C_C + 2 * KV_C:])
    logf = jax.nn.log_sigmoid(fl + b_f).reshape(B, T, H_C)
    return (q.reshape(B, T, H_C, HD_C), kv[:, :KV_C].reshape(B, T, KVH_C, HD_C),
            kv[:, KV_C:].reshape(B, T, KVH_C, HD_C), logf)


def fox_prompt(q, k, v, logf):
    B, T = q.shape[:2]
    nb = T // Q_BLOCK
    F = jnp.cumsum(logf, axis=1)
    Fk = jnp.transpose(F.reshape(B, T, KVH_C, G_C), (0, 2, 3, 1))
    qb = jnp.moveaxis(q.reshape(B, nb, Q_BLOCK, KVH_C, G_C, HD_C), 1, 0)
    Fq = jnp.moveaxis(Fk.reshape(B, KVH_C, G_C, nb, Q_BLOCK), 3, 0)
    kpos = jnp.arange(T)

    def block(args):
        q_blk, fq, start = args
        s = jnp.einsum('bqhgd,bkhd->bhgqk', q_blk, k) * (HD_C ** -0.5)
        s = s + fq[..., None] - Fk[:, :, :, None, :]
        qpos = start + jnp.arange(Q_BLOCK)
        s = jnp.where(kpos[None, :] <= qpos[:, None], s, -jnp.inf)
        p = jax.nn.softmax(s, axis=-1)
        return jnp.einsum('bhgqk,bkhd->bqhgd', p, v)

    o = lax.map(block, (qb, Fq, jnp.arange(nb) * Q_BLOCK))
    return jnp.moveaxis(o, 0, 1).reshape(B, T, C_C)


def fox_sample(q, k, v, logf, cache_k, cache_v, cache_lf, layer, page_table):
    B, T = q.shape[:2]
    past_len = page_table.shape[1] * PAGE_SIZE
    L = past_len + T
    kpos = jnp.arange(L)
    qpos = past_len + jnp.arange(T)
    causal = kpos[None, :] <= qpos[:, None]

    def one(args):
        pt, q_s, k_s, v_s, lf_s = args
        k_all = jnp.concatenate([cache_k[layer, pt].reshape(past_len, KVH_C, HD_C), k_s], axis=0)
        v_all = jnp.concatenate([cache_v[layer, pt].reshape(past_len, KVH_C, HD_C), v_s], axis=0)
        lf_all = jnp.concatenate([cache_lf[layer, pt].reshape(past_len, H_C), lf_s], axis=0)
        F = jnp.transpose(jnp.cumsum(lf_all, axis=0).reshape(L, KVH_C, G_C), (1, 2, 0))
        s = jnp.einsum('qhgd,khd->hgqk', q_s.reshape(T, KVH_C, G_C, HD_C), k_all) * (HD_C ** -0.5)
        s = s + F[:, :, past_len:, None] - F[:, :, None, :]
        s = jnp.where(causal, s, -jnp.inf)
        p = jax.nn.softmax(s, axis=-1)
        return jnp.einsum('hgqk,khd->qhgd', p, v_all).reshape(T, C_C)

    return lax.map(one, (page_table, q, k, v, logf))


MOE_ROWS = 256
MOE_COMBINE_TOKENS = 64
VMEM_LIMIT_MOE = 56 * 1024 * 1024


def _moe_up_body(blk_exp_ref, nused_ref, row_tok_ref, roww_ref, x_hbm, wg_ref, wu_ref, act_ref,
                 xbuf, wg_bf, wu_bf, sem, *, RB):
    b = pl.program_id(0)
    nused = nused_ref[0]
    slot = b % 2

    def gather(blk, slot_):
        def body(i, carry):
            t = row_tok_ref[blk * RB + i]
            pltpu.make_async_copy(x_hbm.at[pl.ds(t, 1)], xbuf.at[slot_, pl.ds(i, 1)], sem.at[slot_]).start()
            return carry
        lax.fori_loop(0, RB, body, 0)

    @pl.when(b == 0)
    def _():
        gather(0, 0)

    @pl.when(b + 1 < nused)
    def _():
        gather(b + 1, 1 - slot)

    @pl.when(b < nused)
    def _():
        pltpu.make_async_copy(x_hbm.at[pl.ds(0, RB)], xbuf.at[slot], sem.at[slot]).wait()
        e = blk_exp_ref[b]
        e_prev = blk_exp_ref[jnp.maximum(b - 1, 0)]

        @pl.when((b == 0) | (e != e_prev))
        def _():
            wg_bf[...] = wg_ref[...].astype(jnp.bfloat16)
            wu_bf[...] = wu_ref[...].astype(jnp.bfloat16)

        x = xbuf[slot].astype(jnp.bfloat16)
        g = jnp.dot(x, wg_bf[...], preferred_element_type=jnp.float32)
        u = jnp.dot(x, wu_bf[...], preferred_element_type=jnp.float32)
        act_ref[...] = (g * jax.nn.sigmoid(g)) * u * roww_ref[...]

    @pl.when(b >= nused)
    def _():
        act_ref[...] = jnp.zeros_like(act_ref)


def _moe_down_body(blk_exp_ref, nused_ref, act_ref, wd_ref, y_ref, wd_bf):
    b = pl.program_id(0)
    nused = nused_ref[0]

    @pl.when(b < nused)
    def _():
        e = blk_exp_ref[b]
        e_prev = blk_exp_ref[jnp.maximum(b - 1, 0)]

        @pl.when((b == 0) | (e != e_prev))
        def _():
            wd_bf[...] = wd_ref[...].astype(jnp.bfloat16)

        y_ref[...] = jnp.dot(act_ref[...].astype(jnp.bfloat16), wd_bf[...], preferred_element_type=jnp.float32)

    @pl.when(b >= nused)
    def _():
        y_ref[...] = jnp.zeros_like(y_ref)


def _moe_combine_body(pos_ref, y_hbm, shared_ref, out_ref, buf, sem, *, TB):
    b = pl.program_id(0)
    nb = pl.num_programs(0)
    slot = b % 2

    def gather(blk, slot_):
        def body(i, carry):
            for kk in range(TOP_K):
                p = pos_ref[(blk * TB + i) * TOP_K + kk]
                pltpu.make_async_copy(y_hbm.at[pl.ds(p, 1)], buf.at[slot_, pl.ds(kk * TB + i, 1)],
                                      sem.at[slot_]).start()
            return carry
        lax.fori_loop(0, TB, body, 0)

    @pl.when(b == 0)
    def _():
        gather(0, 0)

    @pl.when(b + 1 < nb)
    def _():
        gather(b + 1, 1 - slot)

    pltpu.make_async_copy(y_hbm.at[pl.ds(0, TOP_K * TB)], buf.at[slot], sem.at[slot]).wait()
    acc = shared_ref[...]
    for kk in range(TOP_K):
        acc = acc + buf[slot, kk * TB:(kk + 1) * TB]
    out_ref[...] = acc


def moe_route(scores, b_router):
    T = scores.shape[0]
    per_group = N_EXPERTS // N_GROUPS
    biased = scores + b_router
    grp = jnp.sum(lax.top_k(biased.reshape(T, N_GROUPS, per_group), 2)[0], axis=-1)
    gidx = lax.top_k(grp, TOPK_GROUPS)[1]
    gmask = jnp.any(gidx[:, :, None] == jnp.arange(N_GROUPS)[None, None, :], axis=1)
    biased = jnp.where(jnp.repeat(gmask, per_group, axis=1), biased, -jnp.inf)
    eidx = lax.top_k(biased, TOP_K)[1]
    gates = jnp.take_along_axis(scores, eidx, axis=1)
    gates = gates / jnp.sum(gates, axis=-1, keepdims=True) * ROUTED_SCALE
    return eidx, gates


def moe_layout(eidx, gates, RB):
    T = eidx.shape[0]
    A = T * TOP_K
    E = N_EXPERTS
    e_flat = eidx.reshape(A).astype(jnp.int32)
    g_flat = gates.reshape(A)
    onehot = (e_flat[:, None] == jnp.arange(E, dtype=jnp.int32)[None, :]).astype(jnp.int32)
    rank_incl = jnp.cumsum(onehot, axis=0)
    rank = jnp.sum(onehot * rank_incl, axis=1) - 1
    counts = rank_incl[-1]
    padded = (counts + RB - 1) // RB * RB
    seg_start = jnp.cumsum(counts) - counts
    pad_end = jnp.cumsum(padded)
    pad_start = pad_end - padded
    pos = pad_start[e_flat] + rank
    n_blocks = -(-(A + E * (RB - 1)) // RB)
    n_rows = n_blocks * RB
    blk_exp = jnp.minimum(jnp.searchsorted(pad_end, jnp.arange(n_blocks, dtype=jnp.int32) * RB, side='right'),
                          E - 1).astype(jnp.int32)
    nused = (pad_end[-1] // RB).astype(jnp.int32).reshape(1)
    order = jnp.argsort(e_flat, stable=True).astype(jnp.int32)
    prow = jnp.arange(n_rows, dtype=jnp.int32)
    e_row = blk_exp[prow // RB]
    off = prow - pad_start[e_row]
    valid = (off < counts[e_row]) & (prow < pad_end[-1])
    src = order[jnp.clip(seg_start[e_row] + off, 0, A - 1)]
    row_tok = jnp.where(valid, src // TOP_K, 0).astype(jnp.int32)
    row_w = jnp.where(valid, g_flat[src], 0.0).astype(jnp.float32)
    return pos.astype(jnp.int32), row_tok, row_w, blk_exp, nused, n_blocks


def moe_routed_plus_shared(h, layer, eidx, gates, w_gate, w_up, w_down, shared):
    T, D = h.shape
    F = w_gate.shape[-1]
    RB = MOE_ROWS
    TB = MOE_COMBINE_TOKENS
    pos, row_tok, row_w, blk_exp, nused, n_blocks = moe_layout(eidx, gates, RB)
    n_rows = n_blocks * RB
    wspec_up = pl.BlockSpec((None, None, D, F), lambda b, be, nu, rt: (layer, be[b], 0, 0))
    act = pl.pallas_call(
        functools.partial(_moe_up_body, RB=RB),
        grid_spec=pltpu.PrefetchScalarGridSpec(
            num_scalar_prefetch=3,
            grid=(n_blocks,),
            in_specs=[pl.BlockSpec((RB, 1), lambda b, be, nu, rt: (b, 0)),
                      pl.BlockSpec(memory_space=pl.ANY),
                      wspec_up, wspec_up],
            out_specs=pl.BlockSpec((RB, F), lambda b, be, nu, rt: (b, 0)),
            scratch_shapes=[pltpu.VMEM((2, RB, D), jnp.float32),
                            pltpu.VMEM((D, F), jnp.bfloat16),
                            pltpu.VMEM((D, F), jnp.bfloat16),
                            pltpu.SemaphoreType.DMA((2,))]),
        out_shape=jax.ShapeDtypeStruct((n_rows, F), jnp.float32),
        compiler_params=pltpu.CompilerParams(dimension_semantics=("arbitrary",),
                                             vmem_limit_bytes=VMEM_LIMIT_MOE),
        name="moe_gate_up",
    )(blk_exp, nused, row_tok, row_w.reshape(n_rows, 1), h, w_gate, w_up)
    y = pl.pallas_call(
        _moe_down_body,
        grid_spec=pltpu.PrefetchScalarGridSpec(
            num_scalar_prefetch=2,
            grid=(n_blocks,),
            in_specs=[pl.BlockSpec((RB, F), lambda b, be, nu: (b, 0)),
                      pl.BlockSpec((None, None, F, D), lambda b, be, nu: (layer, be[b], 0, 0))],
            out_specs=pl.BlockSpec((RB, D), lambda b, be, nu: (b, 0)),
            scratch_shapes=[pltpu.VMEM((F, D), jnp.bfloat16)]),
        out_shape=jax.ShapeDtypeStruct((n_rows, D), jnp.float32),
        compiler_params=pltpu.CompilerParams(dimension_semantics=("arbitrary",),
                                             vmem_limit_bytes=VMEM_LIMIT_MOE),
        name="moe_down",
    )(blk_exp, nused, act, w_down)
    return pl.pallas_call(
        functools.partial(_moe_combine_body, TB=TB),
        grid_spec=pltpu.PrefetchScalarGridSpec(
            num_scalar_prefetch=1,
            grid=(T // TB,),
            in_specs=[pl.BlockSpec(memory_space=pl.ANY),
                      pl.BlockSpec((TB, D), lambda b, ps: (b, 0))],
            out_specs=pl.BlockSpec((TB, D), lambda b, ps: (b, 0)),
            scratch_shapes=[pltpu.VMEM((2, TOP_K * TB, D), jnp.float32),
                            pltpu.SemaphoreType.DMA((2,))]),
        out_shape=jax.ShapeDtypeStruct((T, D), jnp.float32),
        compiler_params=pltpu.CompilerParams(dimension_semantics=("arbitrary",),
                                             vmem_limit_bytes=VMEM_LIMIT_MOE),
        name="moe_combine",
    )(pos, y, shared)


def moe_ffn(h, layer, w_router, b_router, w_gate, w_up, w_down, ws_gate, ws_up, ws_down):
    scores = jax.nn.sigmoid(matmul(h, w_router[layer]))
    eidx, gates = moe_route(scores, b_router[layer])
    shared = matmul(jax.nn.silu(matmul(h, ws_gate[layer])) * matmul(h, ws_up[layer]), ws_down[layer])
    return moe_routed_plus_shared(h, layer, eidx, gates, w_gate, w_up, w_down, shared)


def kernel(x_prompt, x_sample, c_prompt, c_sample, state_rwkv, state_rwkv_shift, cache_swa_k, cache_swa_v, cache_fox_k, cache_fox_v, cache_fox_logf, page_table, ada_w, ada_b, norm_w, final_norm_w, ab_w_in, ab_w_out, rwkv_mu, rwkv_w0, rwkv_w_decay_up, rwkv_a0, rwkv_w_aaa_up, rwkv_w_gate_up, rwkv_k_k, rwkv_k_a, rwkv_r_k, rwkv_lnx_w, rwkv_lnx_b, swa_sinks, fox_w_in, fox_b_f, fox_w_out, moe_w_router, moe_b_router, moe_w_gate, moe_w_up, moe_w_down, shared_w_gate, shared_w_up, shared_w_down):
    Bp, Tp, D = x_prompt.shape
    Bs, Ts, _ = x_sample.shape
    depth = ada_w.shape[0]
    past_len = page_table.shape[1] * PAGE_SIZE
    pos_p = jnp.arange(Tp)
    pos_s = past_len + jnp.arange(Ts)
    xp, xs = x_prompt, x_sample
    rw_S_p, rw_sh_p, sw_k_p, sw_v_p, fk_p, fv_p, flf_p = [], [], [], [], [], [], []
    rw_S_s, rw_sh_s, sw_k_s, sw_v_s, fk_s, fv_s, flf_s = [], [], [], [], [], [], []
    for l in range(depth):
        sh1_p, sc1_p, g1_p, sh2_p, sc2_p, g2_p = ada_params(c_prompt, ada_w[l], ada_b[l])
        sh1_s, sc1_s, g1_s, sh2_s, sc2_s, g2_s = ada_params(c_sample, ada_w[l], ada_b[l])
        hp = rmsnorm(xp, norm_w[l, 0]) * (1 + sc1_p) + sh1_p
        hs = rmsnorm(xs, norm_w[l, 0]) * (1 + sc1_s) + sh1_s
        if l % 2 == 0:
            i = l // 2
            rw = (rwkv_mu[i], rwkv_w0[i], rwkv_w_decay_up[i], rwkv_a0[i], rwkv_w_aaa_up[i], rwkv_w_gate_up[i],
                  rwkv_k_k[i], rwkv_k_a[i], rwkv_r_k[i], rwkv_lnx_w[i], rwkv_lnx_b[i])
            o_p, S_p, row_p, kw_p, vw_p = ab_mixer(
                hp, pos_p, jnp.zeros((Bp, P_A), hp.dtype), jnp.zeros((Bp, H_A, HD_A, HD_A), jnp.float32),
                None, None, ab_w_in[i], ab_w_out[i], rw, swa_sinks[i])
            o_s, S_s, row_s, kw_s, vw_s = ab_mixer(
                hs, pos_s, state_rwkv_shift[i], state_rwkv[i], cache_swa_k[i], cache_swa_v[i],
                ab_w_in[i], ab_w_out[i], rw, swa_sinks[i])
            rw_S_p.append(S_p); rw_sh_p.append(row_p); sw_k_p.append(kw_p); sw_v_p.append(vw_p)
            rw_S_s.append(S_s); rw_sh_s.append(row_s); sw_k_s.append(kw_s); sw_v_s.append(vw_s)
        else:
            j = l // 2
            q_p, k_p, v_p, lf_p = fox_project(hp, fox_w_in[j], fox_b_f[j])
            q_s, k_s, v_s, lf_s = fox_project(hs, fox_w_in[j], fox_b_f[j])
            o_p = mm3(fox_prompt(q_p, k_p, v_p, lf_p), fox_w_out[j])
            o_s = mm3(fox_sample(q_s, k_s, v_s, lf_s, cache_fox_k, cache_fox_v, cache_fox_logf, j, page_table),
                      fox_w_out[j])
            fk_p.append(k_p); fv_p.append(v_p); flf_p.append(lf_p)
            fk_s.append(k_s); fv_s.append(v_s); flf_s.append(lf_s)
        xp = xp + g1_p * o_p
        xs = xs + g1_s * o_s
        h2p = rmsnorm(xp, norm_w[l, 1]) * (1 + sc2_p) + sh2_p
        h2s = rmsnorm(xs, norm_w[l, 1]) * (1 + sc2_s) + sh2_s
        tok = jnp.concatenate([h2p.reshape(Bp * Tp, D), h2s.reshape(Bs * Ts, D)], axis=0)
        y = moe_ffn(tok, l, moe_w_router, moe_b_router, moe_w_gate, moe_w_up, moe_w_down,
                    shared_w_gate, shared_w_up, shared_w_down)
        xp = xp + g2_p * y[:Bp * Tp].reshape(Bp, Tp, D)
        xs = xs + g2_s * y[Bp * Tp:].reshape(Bs, Ts, D)
    y_prompt = rmsnorm(xp, final_norm_w)
    y_sample = rmsnorm(xs, final_norm_w)
    return (y_prompt, y_sample,
            jnp.stack(rw_S_p), jnp.stack(rw_sh_p), jnp.stack(sw_k_p), jnp.stack(sw_v_p),
            jnp.stack(fk_p), jnp.stack(fv_p), jnp.stack(flf_p),
            jnp.stack(rw_S_s), jnp.stack(rw_sh_s), jnp.stack(sw_k_s), jnp.stack(sw_v_s),
            jnp.stack(fk_s), jnp.stack(fv_s), jnp.stack(flf_s))
```

```python
import functools

import jax
import jax.numpy as jnp
from jax import lax
from jax.experimental import pallas as pl
from jax.experimental.pallas import tpu as pltpu

D_MODEL = 4096
PAGE_SIZE = 128

H_A = 32
HD_A = 64
C_A = H_A * HD_A
LORA_DECAY = 128
LORA_AAA = 128
LORA_GATE = 480
P_A = 3 * C_A + LORA_DECAY + LORA_AAA + LORA_GATE
RWKV_SPLIT = (C_A, 2 * C_A, 3 * C_A, 3 * C_A + LORA_DECAY, 3 * C_A + LORA_DECAY + LORA_AAA)
GN_EPS_A = 64e-5

H_B = 32
KVH_B = 4
G_B = H_B // KVH_B
HD_B = 64
C_B = H_B * HD_B
KV_B = KVH_B * HD_B
WINDOW = 128
ROPE_THETA = 10000.0

H_C = 32
KVH_C = 8
G_C = H_C // KVH_C
HD_C = 128
C_C = H_C * HD_C
KV_C = KVH_C * HD_C
Q_BLOCK = 128

N_EXPERTS = 64
TOP_K = 6
N_GROUPS = 8
TOPK_GROUPS = 4
ROUTED_SCALE = 2.5
MOE_BLOCK = 128

NORM_EPS = 1e-6

VMEM_LIMIT_BYTES = 48 * 1024 * 1024


def _mm_body(x_ref, w_ref, o_ref, acc_ref):
    k = pl.program_id(2)

    @pl.when(k == 0)
    def _():
        acc_ref[...] = jnp.zeros_like(acc_ref)

    acc_ref[...] += jnp.dot(x_ref[...].astype(jnp.bfloat16), w_ref[...].astype(jnp.bfloat16),
                            preferred_element_type=jnp.float32)

    @pl.when(k == pl.num_programs(2) - 1)
    def _():
        o_ref[...] = acc_ref[...]


def _pick_tile(n, candidates):
    for c in candidates:
        if n % c == 0:
            return c
    return n


def matmul(x, w):
    M, K = x.shape
    _, N = w.shape
    tm = _pick_tile(M, (1024, 768, 512, 256))
    tn = _pick_tile(N, (1024, 512, 256, 128))
    tk = _pick_tile(K, (1024, 512))
    return pl.pallas_call(
        _mm_body,
        grid=(M // tm, N // tn, K // tk),
        in_specs=[pl.BlockSpec((tm, tk), lambda i, j, k: (i, k)),
                  pl.BlockSpec((tk, tn), lambda i, j, k: (k, j))],
        out_specs=pl.BlockSpec((tm, tn), lambda i, j, k: (i, j)),
        out_shape=jax.ShapeDtypeStruct((M, N), jnp.float32),
        scratch_shapes=[pltpu.VMEM((tm, tn), jnp.float32)],
        compiler_params=pltpu.CompilerParams(
            dimension_semantics=("parallel", "parallel", "arbitrary"),
            vmem_limit_bytes=VMEM_LIMIT_BYTES),
        name="dense_matmul",
    )(x, w)


def mm3(h, w):
    B, T, K = h.shape
    return matmul(h.reshape(B * T, K), w).reshape(B, T, w.shape[1])


HD = HD_A
PAIR = 2 * HD
RWKV_CHUNK = 64
RWKV_PAIRS_PER_STEP = 4


def _bf(x):
    return x.astype(jnp.bfloat16)


def _dot(a, b):
    return jnp.dot(_bf(a), _bf(b), preferred_element_type=jnp.float32)


def _dot_nt(a, b):
    return lax.dot_general(_bf(a), _bf(b), (((1,), (1,)), ((), ())), preferred_element_type=jnp.float32)


def _split3(x):
    h = x.astype(jnp.bfloat16)
    r1 = x - h.astype(jnp.float32)
    m = r1.astype(jnp.bfloat16)
    l = (r1 - m.astype(jnp.float32)).astype(jnp.bfloat16)
    return h, m, l


def _rwkv_body(s0_ref, r_ref, lw_ref, k_ref, v_ref, kk_ref, b_ref, y_ref, sT_ref, s_scr, *, C, hp):
    c = pl.program_id(2)

    @pl.when(c == 0)
    def _():
        s_scr[...] = s0_ref[...]

    n = 2 * C
    row = lax.broadcasted_iota(jnp.int32, (n, n), 0)
    col = lax.broadcasted_iota(jnp.int32, (n, n), 1)
    same_head = (row >= C) == (col >= C)
    tt = row & (C - 1)
    ss = col & (C - 1)
    strict = same_head & (ss < tt)
    incl = same_head & (ss <= tt)
    trow = lax.broadcasted_iota(jnp.int32, (C, C), 0)
    tcol = lax.broadcasted_iota(jnp.int32, (C, C), 1)
    tri = jnp.where(tcol <= trow, 1.0, 0.0).astype(jnp.bfloat16)
    lane = lax.broadcasted_iota(jnp.int32, (1, PAIR), 1)
    lo = jnp.where(lane < HD, 1.0, 0.0)
    hi = 1.0 - lo
    vrow = lax.broadcasted_iota(jnp.int32, (PAIR, PAIR), 0)
    vcol = lax.broadcasted_iota(jnp.int32, (PAIR, PAIR), 1)
    bd = (vrow >= HD) == (vcol >= HD)
    eye = jnp.where(row == col, 1.0, 0.0)
    nlev = C.bit_length() - 1

    P = range(hp)
    sls = [slice(p * PAIR, (p + 1) * PAIR) for p in P]
    lw = [lw_ref[:, s] for s in sls]
    r = [r_ref[:, s] for s in sls]
    k = [k_ref[:, s] for s in sls]
    v = [v_ref[:, s] for s in sls]
    kk = [kk_ref[:, s] for s in sls]
    b = [b_ref[:, s] for s in sls]
    csum = lambda x: jnp.dot(tri, x, preferred_element_type=jnp.float32)
    w3 = [_split3(x) for x in lw]
    Lc = [csum(h) + (csum(m) + csum(l)) for h, m, l in w3]
    e_inc = [jnp.exp(x) for x in Lc]
    e_neg = [jnp.exp(-x) for x in Lc]
    e_end = [jnp.exp(x[C - 1:C, :] - x) for x in Lc]
    Kq = [kk[p] * jnp.exp(Lc[p] - lw[p]) for p in P]
    Rq = [r[p] * e_inc[p] for p in P]
    Bd = [b[p] * e_neg[p] for p in P]
    Kd = [k[p] * e_neg[p] for p in P]
    G = [_dot_nt(jnp.concatenate([Kq[p] * lo, Kq[p] * hi, Rq[p] * lo, Rq[p] * hi], axis=0),
                 jnp.concatenate([Bd[p], Bd[p], Kd[p], Kd[p]], axis=0)) for p in P]
    Mb = [jnp.where(strict, g[:n, :n], 0.0) for g in G]
    Mk = [jnp.where(strict, g[:n, n:], 0.0) for g in G]
    Nb = [jnp.where(incl, g[n:, :n], 0.0) for g in G]
    Nk = [jnp.where(incl, g[n:, n:], 0.0) for g in G]
    T = [eye - jnp.where((tt >> 1) == (ss >> 1), m, 0.0) for m in Mb]
    for j in range(2, nlev + 1):
        lower_left = ((tt >> j) == (ss >> j)) & ((tt >> (j - 1)) > (ss >> (j - 1)))
        TM = [_dot(T[p], jnp.where(lower_left, Mb[p], 0.0)) for p in P]
        T = [T[p] - _dot(TM[p], T[p]) for p in P]
    V2 = [jnp.concatenate([x * lo, x * hi], axis=0) for x in v]
    MkV = [_dot(Mk[p], V2[p]) for p in P]
    NkV = [_dot(Nk[p], V2[p]) for p in P]
    BK = [jnp.concatenate([b[p] * e_end[p], k[p] * e_end[p]], axis=0) for p in P]
    S = [s_scr[p] for p in P]
    W0 = [_dot_nt(jnp.concatenate([Kq[p], Rq[p]], axis=0), S[p]) for p in P]
    rhs_u = [-(W0[p][:C] + (MkV[p][:C] + MkV[p][C:])) for p in P]
    UU = [_dot(T[p], jnp.concatenate([rhs_u[p] * lo, rhs_u[p] * hi], axis=0)) for p in P]
    YY = [_dot(Nb[p], UU[p]) + NkV[p] for p in P]
    UV = [jnp.concatenate([UU[p][:C] + UU[p][C:], v[p]], axis=0) for p in P]
    upd = [_dot(UV[p].T, BK[p]) for p in P]
    for p in P:
        y_ref[:, sls[p]] = YY[p][:C] + YY[p][C:] + W0[p][C:]
        S_new = S[p] * e_inc[p][C - 1:C, :] + jnp.where(bd, upd[p], 0.0)
        s_scr[p] = S_new
        sT_ref[p] = S_new


def rwkv7_scan(S0, r, lw, k, v, kk, b):
    B, T, CA = r.shape
    H = CA // HD
    npair = H // 2
    C = RWKV_CHUNK
    hp = min(RWKV_PAIRS_PER_STEP, npair)
    Tp = -(-T // C) * C
    if Tp != T:
        padf = lambda x: jnp.pad(x, ((0, 0), (0, Tp - T), (0, 0)))
        r, lw, k, v, kk, b = map(padf, (r, lw, k, v, kk, b))
    S0p = S0.reshape(B, npair, 2, HD, HD)
    z = jnp.zeros_like(S0p[:, :, 0])
    S0bd = jnp.concatenate([jnp.concatenate([S0p[:, :, 0], z], axis=-1),
                            jnp.concatenate([z, S0p[:, :, 1]], axis=-1)], axis=-2)
    seq_spec = pl.BlockSpec((None, C, hp * PAIR), lambda bb, g, c: (bb, c, g))
    st_spec = pl.BlockSpec((None, hp, PAIR, PAIR), lambda bb, g, c: (bb, g, 0, 0))
    y, Sbd = pl.pallas_call(
        functools.partial(_rwkv_body, C=C, hp=hp),
        grid=(B, npair // hp, Tp // C),
        in_specs=[st_spec] + [seq_spec] * 6,
        out_specs=[seq_spec, st_spec],
        out_shape=[jax.ShapeDtypeStruct((B, Tp, CA), jnp.float32),
                   jax.ShapeDtypeStruct((B, npair, PAIR, PAIR), jnp.float32)],
        scratch_shapes=[pltpu.VMEM((hp, PAIR, PAIR), jnp.float32)],
        compiler_params=pltpu.CompilerParams(
            dimension_semantics=("parallel", "parallel", "arbitrary")),
        name="rwkv7_chunk_scan",
    )(S0bd, r, lw, k, v, kk, b)
    S = jnp.stack([Sbd[:, :, :HD, :HD], Sbd[:, :, HD:, HD:]], axis=2).reshape(B, H, HD, HD)
    return y[:, :T], S


def rmsnorm(x, w):
    y = x * lax.rsqrt(jnp.mean(x * x, axis=-1, keepdims=True) + NORM_EPS)
    return y * w


def ada_params(c, w, b):
    m = matmul(jax.nn.silu(c), w) + b
    return jnp.split(m[:, None, :], 6, axis=-1)


def rope(x, pos):
    half = x.shape[-1] // 2
    inv = ROPE_THETA ** (-jnp.arange(half, dtype=jnp.float32) / half)
    ang = pos.astype(jnp.float32)[:, None] * inv[None, :]
    cos = jnp.cos(ang)[None, :, None, :]
    sin = jnp.sin(ang)[None, :, None, :]
    x1, x2 = x[..., :half], x[..., half:]
    return jnp.concatenate([x1 * cos - x2 * sin, x2 * cos + x1 * sin], axis=-1)


def rwkv7_mixer(pa, prev_row, S0, mu, w0, w_dec_up, a0, w_aaa_up, w_gate_up, k_k, k_a, r_k, lnx_w, lnx_b):
    B, T, _ = pa.shape
    shifted = jnp.concatenate([prev_row[:, None, :], pa[:, :-1]], axis=1)
    m = pa + (shifted - pa) * mu
    r, k, v, wd, ad, gd = jnp.split(m, RWKV_SPLIT, axis=-1)
    w = -jax.nn.softplus(-(w0 + mm3(jnp.tanh(wd), w_dec_up))) - 0.5
    a = jax.nn.sigmoid(a0 + mm3(ad, w_aaa_up))
    g = mm3(jax.nn.sigmoid(gd), w_gate_up)
    heads = lambda t: t.reshape(B, T, H_A, HD_A)
    kk = heads(k * k_k)
    kk = kk / jnp.maximum(jnp.sqrt(jnp.sum(kk * kk, axis=-1, keepdims=True)), 1e-12)
    k = k * (1 + (a - 1) * k_a)
    rh, kh, vh = heads(r), heads(k), heads(v)
    y, S = rwkv7_scan(S0, r, -jnp.exp(w), k, v, kk.reshape(B, T, C_A), kk.reshape(B, T, C_A) * a)
    y = heads(y)
    mean = jnp.mean(y, axis=-1, keepdims=True)
    var = jnp.mean(jnp.square(y - mean), axis=-1, keepdims=True)
    y = ((y - mean) * lax.rsqrt(var + GN_EPS_A)).reshape(B, T, C_A) * lnx_w + lnx_b
    bonus = (jnp.sum(rh * kh * r_k, axis=-1, keepdims=True) * vh).reshape(B, T, C_A)
    out = (y + bonus) * g
    return out, S, pa[:, -1]


def sink_softmax(s, sink):
    sk = jnp.broadcast_to(sink[:, :, None], s.shape[:-1])[..., None]
    return jax.nn.softmax(jnp.concatenate([s, sk], axis=-1), axis=-1)[..., :-1]


def swa_prompt(q, k, v, sinks):
    B, T = q.shape[:2]
    nb = T // WINDOW
    qb = q.reshape(B, nb, WINDOW, KVH_B, G_B, HD_B)
    kb = k.reshape(B, nb, WINDOW, KVH_B, HD_B)
    vb = v.reshape(B, nb, WINDOW, KVH_B, HD_B)
    pad = ((0, 0), (1, 0), (0, 0), (0, 0), (0, 0))
    kc = jnp.concatenate([jnp.pad(kb, pad)[:, :-1], kb], axis=2)
    vc = jnp.concatenate([jnp.pad(vb, pad)[:, :-1], vb], axis=2)
    s = jnp.einsum('bnqhgd,bnkhd->bnhgqk', qb, kc) * (HD_B ** -0.5)
    qi = jnp.arange(WINDOW)[:, None]
    ki = jnp.arange(2 * WINDOW)[None, :] - WINDOW
    band = (ki <= qi) & (qi - ki < WINDOW)
    valid = band[None] & ((jnp.arange(nb)[:, None, None] * WINDOW + ki[None]) >= 0)
    s = jnp.where(valid[None, :, None, None], s, -jnp.inf)
    p = sink_softmax(s, sinks.reshape(KVH_B, G_B))
    o = jnp.einsum('bnhgqk,bnkhd->bnqhgd', p, vc)
    return o.reshape(B, T, C_B), k[:, -WINDOW:], v[:, -WINDOW:]


def swa_sample(q, k, v, kbuf, vbuf, sinks):
    B, T = q.shape[:2]
    kc = jnp.concatenate([kbuf, k], axis=1)
    vc = jnp.concatenate([vbuf, v], axis=1)
    kpos = jnp.arange(WINDOW + T) - WINDOW
    qpos = jnp.arange(T)
    mask = (kpos[None, :] <= qpos[:, None]) & (qpos[:, None] - kpos[None, :] < WINDOW)
    s = jnp.einsum('bqhgd,bkhd->bhgqk', q.reshape(B, T, KVH_B, G_B, HD_B), kc) * (HD_B ** -0.5)
    s = jnp.where(mask, s, -jnp.inf)
    p = sink_softmax(s, sinks.reshape(KVH_B, G_B))
    o = jnp.einsum('bhgqk,bkhd->bqhgd', p, vc)
    return o.reshape(B, T, C_B), kc[:, -WINDOW:], vc[:, -WINDOW:]


def ab_mixer(h, pos, prev_row, S0, kbuf, vbuf, w_in, w_out, rwkv_params, sinks):
    B, T, _ = h.shape
    h2 = h.reshape(B * T, -1)
    w_parts = (w_in[:, :3 * C_A], w_in[:, 3 * C_A:P_A], w_in[:, P_A:P_A + C_B], w_in[:, P_A + C_B:])
    pa = jnp.concatenate([matmul(h2, w_parts[0]), matmul(h2, w_parts[1])], axis=-1).reshape(B, T, P_A)
    q = matmul(h2, w_parts[2]).reshape(B, T, C_B)
    kv = matmul(h2, w_parts[3]).reshape(B, T, 2 * KV_B)
    k, v = kv[..., :KV_B], kv[..., KV_B:]
    o_a, S, last_row = rwkv7_mixer(pa, prev_row, S0, *rwkv_params)
    q = rope(q.reshape(B, T, H_B, HD_B), pos)
    k = rope(k.reshape(B, T, KVH_B, HD_B), pos)
    v = v.reshape(B, T, KVH_B, HD_B)
    if kbuf is None:
        o_b, kw, vw = swa_prompt(q, k, v, sinks)
    else:
        o_b, kw, vw = swa_sample(q, k, v, kbuf, vbuf, sinks)
    out = mm3(jnp.concatenate([o_a, o_b], axis=-1), w_out)
    return out, S, last_row, kw, vw


def fox_project(h, w_in, b_f):
    B, T, _ = h.shape
    h2 = h.reshape(B * T, -1)
    q = matmul(h2, w_in[:, :C_C])
    kv = matmul(h2, w_in[:, C_C:C_C + 2 * KV_C])
    fl = matmul(h2, w_in[:, C_C + 2 * KV_C:])
    logf = jax.nn.log_sigmoid(fl + b_f).reshape(B, T, H_C)
    return (q.reshape(B, T, H_C, HD_C), kv[:, :KV_C].reshape(B, T, KVH_C, HD_C),
            kv[:, KV_C:].reshape(B, T, KVH_C, HD_C), logf)


NEG_BIG = -1e30
FOX_PAGES_PER_STEP = 8
FOX_Q_TILE = 256


def _dot3(parts, w):
    f = lambda x: jnp.dot(x, w, preferred_element_type=jnp.float32)
    return f(parts[0]) + (f(parts[1]) + f(parts[2]))


def _dot3_left(w, parts):
    f = lambda x: jnp.dot(w, x, preferred_element_type=jnp.float32)
    return f(parts[0]) + (f(parts[1]) + f(parts[2]))


def _upper_tri(n):
    r = lax.broadcasted_iota(jnp.int32, (n, n), 0)
    c = lax.broadcasted_iota(jnp.int32, (n, n), 1)
    return jnp.where(r <= c, 1.0, 0.0).astype(jnp.bfloat16)


def _online_softmax_step(s, V, m_ref, l_ref, acc_ref, rows):
    m_old = m_ref[rows]
    m_new = jnp.maximum(m_old, jnp.max(s, axis=1, keepdims=True))
    p = jnp.exp(s - jnp.concatenate([m_new] * (s.shape[1] // HD_C), axis=1))
    alpha = jnp.exp(m_old - m_new)
    l_ref[rows] = alpha * l_ref[rows] + jnp.sum(p, axis=1, keepdims=True)
    acc_ref[rows] = alpha * acc_ref[rows] + jnp.dot(_bf(p), V, preferred_element_type=jnp.float32)
    m_ref[rows] = m_new


def _fox_sample_body(pt_ref, q_ref, *rest, G, T_new):
    k_refs = rest[:G]
    v_refs = rest[G:2 * G]
    lf_refs = rest[2 * G:3 * G]
    knew_ref, vnew_ref, lfnew_ref, o_ref, m_ref, l_ref, acc_ref, carry_ref = rest[3 * G:]
    g = pl.program_id(1)
    n_groups = pl.num_programs(1) - 1
    R = G_C * T_new
    RQ = KVH_C * R

    @pl.when(g == 0)
    def _():
        m_ref[...] = jnp.full_like(m_ref, NEG_BIG)
        l_ref[...] = jnp.zeros_like(l_ref)
        acc_ref[...] = jnp.zeros_like(acc_ref)
        carry_ref[...] = jnp.zeros_like(carry_ref)

    q = _bf(q_ref[...] * (HD_C ** -0.5))
    tri = _upper_tri(PAGE_SIZE)
    er = lax.broadcasted_iota(jnp.int32, (RQ, H_C), 0)
    ec = lax.broadcasted_iota(jnp.int32, (RQ, H_C), 1)
    expand = jnp.where(er // T_new == ec, 1.0, 0.0).astype(jnp.bfloat16)

    def process(k_list, v_list, lf_list, mask):
        carry = carry_ref[...]
        Fs = []
        for lf_r in lf_list:
            Fp = _dot3(_split3(lf_r[...]), tri) + carry
            carry = Fp[:, PAGE_SIZE - 1:PAGE_SIZE]
            Fs.append(Fp)
        carry_ref[...] = carry
        F_all = Fs[0] if len(Fs) == 1 else jnp.concatenate(Fs, axis=1)
        bias = _dot3_left(expand, _split3(F_all))
        for kv in range(KVH_C):
            rows = slice(kv * R, (kv + 1) * R)
            kk = [kr[pl.ds(kv, PAGE_SIZE, stride=KVH_C), :] for kr in k_list]
            vv = [vr[pl.ds(kv, PAGE_SIZE, stride=KVH_C), :] for vr in v_list]
            K = _bf(kk[0] if len(kk) == 1 else jnp.concatenate(kk, axis=0))
            V = _bf(vv[0] if len(vv) == 1 else jnp.concatenate(vv, axis=0))
            s = lax.dot_general(q[rows], K, (((1,), (1,)), ((), ())), preferred_element_type=jnp.float32)
            s = s - bias[rows]
            if mask is not None:
                s = jnp.where(mask, s, NEG_BIG)
            _online_softmax_step(s, V, m_ref, l_ref, acc_ref, rows)

    @pl.when(g < n_groups)
    def _():
        process(k_refs, v_refs, lf_refs, None)

    @pl.when(g == n_groups)
    def _():
        tq = lax.broadcasted_iota(jnp.int32, (R, PAGE_SIZE), 0) % T_new
        key = lax.broadcasted_iota(jnp.int32, (R, PAGE_SIZE), 1)
        process([knew_ref], [vnew_ref], [lfnew_ref], key <= tq)
        o_ref[...] = acc_ref[...] / l_ref[...]


def fox_sample(q, k, v, logf, cache_k, cache_v, cache_lf, layer, page_table):
    B, T = q.shape[:2]
    NP = page_table.shape[1]
    G = min(FOX_PAGES_PER_STEP, NP)
    n_groups = NP // G
    n_pool = cache_k.shape[1]
    rows = PAGE_SIZE * KVH_C
    ck = cache_k.reshape(cache_k.shape[0], n_pool, rows, HD_C)
    cv = cache_v.reshape(cache_v.shape[0], n_pool, rows, HD_C)
    clf = jnp.swapaxes(cache_lf, -1, -2)
    qr = jnp.swapaxes(q, 1, 2).reshape(B, H_C * T, HD_C)
    padt = lambda x: jnp.pad(x, ((0, 0), (0, PAGE_SIZE - T), (0, 0), (0, 0))).reshape(B, rows, HD_C)
    knew, vnew = padt(k), padt(v)
    lfnew = jnp.pad(jnp.swapaxes(logf, 1, 2), ((0, 0), (0, 0), (0, PAGE_SIZE - T)))

    def page_spec(i, shape):
        return pl.BlockSpec((None, None) + shape,
                            lambda b, g, pt: (layer, pt[b, jnp.minimum(g, n_groups - 1) * G + i], 0, 0))

    per_b = lambda shape: pl.BlockSpec((None,) + shape, lambda b, g, pt: (b, 0, 0))
    in_specs = ([per_b((H_C * T, HD_C))]
                + [page_spec(i, (rows, HD_C)) for i in range(G)]
                + [page_spec(i, (rows, HD_C)) for i in range(G)]
                + [page_spec(i, (H_C, PAGE_SIZE)) for i in range(G)]
                + [per_b((rows, HD_C)), per_b((rows, HD_C)), per_b((H_C, PAGE_SIZE))])
    RQ = H_C * T
    o = pl.pallas_call(
        functools.partial(_fox_sample_body, G=G, T_new=T),
        grid_spec=pltpu.PrefetchScalarGridSpec(
            num_scalar_prefetch=1,
            grid=(B, n_groups + 1),
            in_specs=in_specs,
            out_specs=per_b((RQ, HD_C)),
            scratch_shapes=[pltpu.VMEM((RQ, HD_C), jnp.float32), pltpu.VMEM((RQ, HD_C), jnp.float32),
                            pltpu.VMEM((RQ, HD_C), jnp.float32), pltpu.VMEM((H_C, 1), jnp.float32)]),
        out_shape=jax.ShapeDtypeStruct((B, RQ, HD_C), jnp.float32),
        compiler_params=pltpu.CompilerParams(dimension_semantics=("parallel", "arbitrary"),
                                             vmem_limit_bytes=VMEM_LIMIT_BYTES),
        name="fox_sample_paged_attention",
    )(page_table, qr, *([ck] * G), *([cv] * G), *([clf] * G), knew, vnew, lfnew)
    return jnp.swapaxes(o.reshape(B, H_C, T, HD_C), 1, 2).reshape(B, T, H_C * HD_C)


def _fox_cumsum_body(lf_ref, f_ref, *, blk):
    T = lf_ref.shape[-1]
    tri = _upper_tri(blk)
    carry = jnp.zeros((lf_ref.shape[0], 1), jnp.float32)
    for j in range(T // blk):
        sl = slice(j * blk, (j + 1) * blk)
        Fp = _dot3(_split3(lf_ref[:, sl]), tri) + carry
        f_ref[:, sl] = Fp
        carry = Fp[:, blk - 1:blk]


def _fox_prompt_body(q_ref, k_ref, v_ref, f_ref, o_ref, qs_ref, m_ref, l_ref, acc_ref, *, tq):
    kv = pl.program_id(1)
    qi = pl.program_id(2)
    for g in range(G_C):
        qs_ref[g * tq:(g + 1) * tq, :] = _bf(q_ref[:, g, :] * (HD_C ** -0.5))
    m_ref[...] = jnp.full_like(m_ref, NEG_BIG)
    l_ref[...] = jnp.zeros_like(l_ref)
    acc_ref[...] = jnp.zeros_like(acc_ref)
    qpos = qi * tq + lax.broadcasted_iota(jnp.int32, (G_C * tq, tq), 0) % tq
    kcol = lax.broadcasted_iota(jnp.int32, (G_C * tq, tq), 1)

    def body(j, carry):
        ks = pl.multiple_of(j * tq, tq)
        K = _bf(k_ref[pl.ds(ks * KVH_C + kv, tq, stride=KVH_C), :])
        V = _bf(v_ref[pl.ds(ks * KVH_C + kv, tq, stride=KVH_C), :])
        s = lax.dot_general(qs_ref[...], K, (((1,), (1,)), ((), ())), preferred_element_type=jnp.float32)
        F = f_ref[:, pl.ds(ks, tq)]
        bias = jnp.concatenate([jnp.broadcast_to(F[g:g + 1, :], (tq, tq)) for g in range(G_C)], axis=0)
        s = jnp.where(kcol + ks <= qpos, s - bias, NEG_BIG)
        _online_softmax_step(s, V, m_ref, l_ref, acc_ref, slice(None))
        return carry

    lax.fori_loop(0, qi + 1, body, 0)
    o = acc_ref[...] / l_ref[...]
    for g in range(G_C):
        o_ref[:, g, :] = o[g * tq:(g + 1) * tq, :]


def fox_prompt(q, k, v, logf):
    B, T = q.shape[:2]
    tq = min(FOX_Q_TILE, T)
    lfT = jnp.swapaxes(logf, 1, 2)
    F = pl.pallas_call(
        functools.partial(_fox_cumsum_body, blk=tq),
        grid=(B,),
        in_specs=[pl.BlockSpec((None, H_C, T), lambda b: (b, 0, 0))],
        out_specs=pl.BlockSpec((None, H_C, T), lambda b: (b, 0, 0)),
        out_shape=jax.ShapeDtypeStruct((B, H_C, T), jnp.float32),
        name="fox_logf_cumsum",
    )(lfT)
    o = pl.pallas_call(
        functools.partial(_fox_prompt_body, tq=tq),
        grid=(B, KVH_C, T // tq),
        in_specs=[pl.BlockSpec((None, tq, None, G_C, HD_C), lambda b, kv, qi: (b, qi, kv, 0, 0)),
                  pl.BlockSpec((None, T * KVH_C, HD_C), lambda b, kv, qi: (b, 0, 0)),
                  pl.BlockSpec((None, T * KVH_C, HD_C), lambda b, kv, qi: (b, 0, 0)),
                  pl.BlockSpec((None, None, G_C, T), lambda b, kv, qi: (b, kv, 0, 0))],
        out_specs=pl.BlockSpec((None, tq, None, G_C, HD_C), lambda b, kv, qi: (b, qi, kv, 0, 0)),
        out_shape=jax.ShapeDtypeStruct((B, T, KVH_C, G_C, HD_C), jnp.float32),
        scratch_shapes=[pltpu.VMEM((G_C * tq, HD_C), jnp.bfloat16),
                        pltpu.VMEM((G_C * tq, HD_C), jnp.float32), pltpu.VMEM((G_C * tq, HD_C), jnp.float32),
                        pltpu.VMEM((G_C * tq, HD_C), jnp.float32)],
        compiler_params=pltpu.CompilerParams(dimension_semantics=("parallel", "parallel", "arbitrary"),
                                             vmem_limit_bytes=VMEM_LIMIT_BYTES),
        name="fox_prompt_flash_attention",
    )(q.reshape(B, T, KVH_C, G_C, HD_C), k.reshape(B, T * KVH_C, HD_C), v.reshape(B, T * KVH_C, HD_C),
      F.reshape(B, KVH_C, G_C, T))
    return o.reshape(B, T, H_C * HD_C)


MOE_ROWS = 256
MOE_COMBINE_TOKENS = 64
VMEM_LIMIT_MOE = 56 * 1024 * 1024


def _moe_up_body(blk_exp_ref, nused_ref, row_tok_ref, roww_ref, x_hbm, wg_ref, wu_ref, act_ref,
                 xbuf, wg_bf, wu_bf, sem, *, RB):
    b = pl.program_id(0)
    nused = nused_ref[0]
    slot = b % 2

    def gather(blk, slot_):
        def body(i, carry):
            t = row_tok_ref[blk * RB + i]
            pltpu.make_async_copy(x_hbm.at[pl.ds(t, 1)], xbuf.at[slot_, pl.ds(i, 1)], sem.at[slot_]).start()
            return carry
        lax.fori_loop(0, RB, body, 0)

    @pl.when(b == 0)
    def _():
        gather(0, 0)

    @pl.when(b + 1 < nused)
    def _():
        gather(b + 1, 1 - slot)

    @pl.when(b < nused)
    def _():
        pltpu.make_async_copy(x_hbm.at[pl.ds(0, RB)], xbuf.at[slot], sem.at[slot]).wait()
        e = blk_exp_ref[b]
        e_prev = blk_exp_ref[jnp.maximum(b - 1, 0)]

        @pl.when((b == 0) | (e != e_prev))
        def _():
            wg_bf[...] = wg_ref[...].astype(jnp.bfloat16)
            wu_bf[...] = wu_ref[...].astype(jnp.bfloat16)

        x = xbuf[slot].astype(jnp.bfloat16)
        g = jnp.dot(x, wg_bf[...], preferred_element_type=jnp.float32)
        u = jnp.dot(x, wu_bf[...], preferred_element_type=jnp.float32)
        act_ref[...] = (g * jax.nn.sigmoid(g)) * u * roww_ref[...]

    @pl.when(b >= nused)
    def _():
        act_ref[...] = jnp.zeros_like(act_ref)


def _moe_down_body(blk_exp_ref, nused_ref, act_ref, wd_ref, y_ref, wd_bf):
    b = pl.program_id(0)
    nused = nused_ref[0]

    @pl.when(b < nused)
    def _():
        e = blk_exp_ref[b]
        e_prev = blk_exp_ref[jnp.maximum(b - 1, 0)]

        @pl.when((b == 0) | (e != e_prev))
        def _():
            wd_bf[...] = wd_ref[...].astype(jnp.bfloat16)

        y_ref[...] = jnp.dot(act_ref[...].astype(jnp.bfloat16), wd_bf[...], preferred_element_type=jnp.float32)

    @pl.when(b >= nused)
    def _():
        y_ref[...] = jnp.zeros_like(y_ref)


def _moe_combine_body(pos_ref, y_hbm, shared_ref, out_ref, buf, sem, *, TB):
    b = pl.program_id(0)
    nb = pl.num_programs(0)
    slot = b % 2

    def gather(blk, slot_):
        def body(i, carry):
            for kk in range(TOP_K):
                p = pos_ref[(blk * TB + i) * TOP_K + kk]
                pltpu.make_async_copy(y_hbm.at[pl.ds(p, 1)], buf.at[slot_, pl.ds(kk * TB + i, 1)],
                                      sem.at[slot_]).start()
            return carry
        lax.fori_loop(0, TB, body, 0)

    @pl.when(b == 0)
    def _():
        gather(0, 0)

    @pl.when(b + 1 < nb)
    def _():
        gather(b + 1, 1 - slot)

    pltpu.make_async_copy(y_hbm.at[pl.ds(0, TOP_K * TB)], buf.at[slot], sem.at[slot]).wait()
    acc = shared_ref[...]
    for kk in range(TOP_K):
        acc = acc + buf[slot, kk * TB:(kk + 1) * TB]
    out_ref[...] = acc


def _moe_invert_body(pos_ref, src_ref):
    def clear(i, carry):
        src_ref[i] = -1
        return carry
    lax.fori_loop(0, src_ref.shape[0], clear, 0, unroll=8)

    def put(i, carry):
        src_ref[pos_ref[i]] = i
        return carry
    lax.fori_loop(0, pos_ref.shape[0], put, 0, unroll=8)


def moe_route(scores, b_router):
    T = scores.shape[0]
    per_group = N_EXPERTS // N_GROUPS
    biased = scores + b_router
    grp = jnp.sum(lax.top_k(biased.reshape(T, N_GROUPS, per_group), 2)[0], axis=-1)
    gidx = lax.top_k(grp, TOPK_GROUPS)[1]
    gmask = jnp.any(gidx[:, :, None] == jnp.arange(N_GROUPS)[None, None, :], axis=1)
    biased = jnp.where(jnp.repeat(gmask, per_group, axis=1), biased, -jnp.inf)
    eidx = lax.top_k(biased, TOP_K)[1]
    gates = jnp.take_along_axis(scores, eidx, axis=1)
    gates = gates / jnp.sum(gates, axis=-1, keepdims=True) * ROUTED_SCALE
    return eidx, gates


def moe_layout(eidx, gates, RB):
    T = eidx.shape[0]
    A = T * TOP_K
    E = N_EXPERTS
    e_flat = eidx.reshape(A).astype(jnp.int32)
    g_flat = gates.reshape(A)
    onehot = (e_flat[:, None] == jnp.arange(E, dtype=jnp.int32)[None, :]).astype(jnp.int32)
    rank_incl = jnp.cumsum(onehot, axis=0)
    rank = jnp.sum(onehot * rank_incl, axis=1) - 1
    counts = rank_incl[-1]
    padded = (counts + RB - 1) // RB * RB
    seg_start = jnp.cumsum(counts) - counts
    pad_end = jnp.cumsum(padded)
    pad_start = pad_end - padded
    pos = pad_start[e_flat] + rank
    n_blocks = -(-(A + E * (RB - 1)) // RB)
    n_rows = n_blocks * RB
    blk_exp = jnp.minimum(jnp.searchsorted(pad_end, jnp.arange(n_blocks, dtype=jnp.int32) * RB, side='right'),
                          E - 1).astype(jnp.int32)
    nused = (pad_end[-1] // RB).astype(jnp.int32).reshape(1)
    pos = pos.astype(jnp.int32)
    src = pl.pallas_call(
        _moe_invert_body,
        grid_spec=pltpu.PrefetchScalarGridSpec(
            num_scalar_prefetch=1, grid=(1,), in_specs=[],
            out_specs=pl.BlockSpec(memory_space=pltpu.SMEM)),
        out_shape=jax.ShapeDtypeStruct((n_rows,), jnp.int32),
        name="moe_invert_positions",
    )(pos)
    valid = src >= 0
    row_tok = jnp.where(valid, src // TOP_K, 0).astype(jnp.int32)
    row_w = jnp.where(valid, g_flat[jnp.maximum(src, 0)], 0.0).astype(jnp.float32)
    return pos, row_tok, row_w, blk_exp, nused, n_blocks


def moe_routed_plus_shared(h, layer, eidx, gates, w_gate, w_up, w_down, shared):
    T, D = h.shape
    F = w_gate.shape[-1]
    RB = MOE_ROWS
    TB = MOE_COMBINE_TOKENS
    pos, row_tok, row_w, blk_exp, nused, n_blocks = moe_layout(eidx, gates, RB)
    n_rows = n_blocks * RB
    wspec_up = pl.BlockSpec((None, None, D, F), lambda b, be, nu, rt: (layer, be[b], 0, 0))
    act = pl.pallas_call(
        functools.partial(_moe_up_body, RB=RB),
        grid_spec=pltpu.PrefetchScalarGridSpec(
            num_scalar_prefetch=3,
            grid=(n_blocks,),
            in_specs=[pl.BlockSpec((RB, 1), lambda b, be, nu, rt: (b, 0)),
                      pl.BlockSpec(memory_space=pl.ANY),
                      wspec_up, wspec_up],
            out_specs=pl.BlockSpec((RB, F), lambda b, be, nu, rt: (b, 0)),
            scratch_shapes=[pltpu.VMEM((2, RB, D), jnp.float32),
                            pltpu.VMEM((D, F), jnp.bfloat16),
                            pltpu.VMEM((D, F), jnp.bfloat16),
                            pltpu.SemaphoreType.DMA((2,))]),
        out_shape=jax.ShapeDtypeStruct((n_rows, F), jnp.float32),
        compiler_params=pltpu.CompilerParams(dimension_semantics=("arbitrary",),
                                             vmem_limit_bytes=VMEM_LIMIT_MOE),
        name="moe_gate_up",
    )(blk_exp, nused, row_tok, row_w.reshape(n_rows, 1), h, w_gate, w_up)
    y = pl.pallas_call(
        _moe_down_body,
        grid_spec=pltpu.PrefetchScalarGridSpec(
            num_scalar_prefetch=2,
            grid=(n_blocks,),
            in_specs=[pl.BlockSpec((RB, F), lambda b, be, nu: (b, 0)),
                      pl.BlockSpec((None, None, F, D), lambda b, be, nu: (layer, be[b], 0, 0))],
            out_specs=pl.BlockSpec((RB, D), lambda b, be, nu: (b, 0)),
            scratch_shapes=[pltpu.VMEM((F, D), jnp.bfloat16)]),
        out_shape=jax.ShapeDtypeStruct((n_rows, D), jnp.float32),
        compiler_params=pltpu.CompilerParams(dimension_semantics=("arbitrary",),
                                             vmem_limit_bytes=VMEM_LIMIT_MOE),
        name="moe_down",
    )(blk_exp, nused, act, w_down)
    return pl.pallas_call(
        functools.partial(_moe_combine_body, TB=TB),
        grid_spec=pltpu.PrefetchScalarGridSpec(
            num_scalar_prefetch=1,
            grid=(T // TB,),
            in_specs=[pl.BlockSpec(memory_space=pl.ANY),
                      pl.BlockSpec((TB, D), lambda b, ps: (b, 0))],
            out_specs=pl.BlockSpec((TB, D), lambda b, ps: (b, 0)),
            scratch_shapes=[pltpu.VMEM((2, TOP_K * TB, D), jnp.float32),
                            pltpu.SemaphoreType.DMA((2,))]),
        out_shape=jax.ShapeDtypeStruct((T, D), jnp.float32),
        compiler_params=pltpu.CompilerParams(dimension_semantics=("arbitrary",),
                                             vmem_limit_bytes=VMEM_LIMIT_MOE),
        name="moe_combine",
    )(pos, y, shared)


def moe_ffn(h, layer, w_router, b_router, w_gate, w_up, w_down, ws_gate, ws_up, ws_down):
    scores = jax.nn.sigmoid(matmul(h, w_router[layer]))
    eidx, gates = moe_route(scores, b_router[layer])
    shared = matmul(jax.nn.silu(matmul(h, ws_gate[layer])) * matmul(h, ws_up[layer]), ws_down[layer])
    return moe_routed_plus_shared(h, layer, eidx, gates, w_gate, w_up, w_down, shared)


def kernel(x_prompt, x_sample, c_prompt, c_sample, state_rwkv, state_rwkv_shift, cache_swa_k, cache_swa_v, cache_fox_k, cache_fox_v, cache_fox_logf, page_table, ada_w, ada_b, norm_w, final_norm_w, ab_w_in, ab_w_out, rwkv_mu, rwkv_w0, rwkv_w_decay_up, rwkv_a0, rwkv_w_aaa_up, rwkv_w_gate_up, rwkv_k_k, rwkv_k_a, rwkv_r_k, rwkv_lnx_w, rwkv_lnx_b, swa_sinks, fox_w_in, fox_b_f, fox_w_out, moe_w_router, moe_b_router, moe_w_gate, moe_w_up, moe_w_down, shared_w_gate, shared_w_up, shared_w_down):
    Bp, Tp, D = x_prompt.shape
    Bs, Ts, _ = x_sample.shape
    depth = ada_w.shape[0]
    past_len = page_table.shape[1] * PAGE_SIZE
    pos_p = jnp.arange(Tp)
    pos_s = past_len + jnp.arange(Ts)
    xp, xs = x_prompt, x_sample
    rw_S_p, rw_sh_p, sw_k_p, sw_v_p, fk_p, fv_p, flf_p = [], [], [], [], [], [], []
    rw_S_s, rw_sh_s, sw_k_s, sw_v_s, fk_s, fv_s, flf_s = [], [], [], [], [], [], []
    for l in range(depth):
        sh1_p, sc1_p, g1_p, sh2_p, sc2_p, g2_p = ada_params(c_prompt, ada_w[l], ada_b[l])
        sh1_s, sc1_s, g1_s, sh2_s, sc2_s, g2_s = ada_params(c_sample, ada_w[l], ada_b[l])
        hp = rmsnorm(xp, norm_w[l, 0]) * (1 + sc1_p) + sh1_p
        hs = rmsnorm(xs, norm_w[l, 0]) * (1 + sc1_s) + sh1_s
        if l % 2 == 0:
            i = l // 2
            rw = (rwkv_mu[i], rwkv_w0[i], rwkv_w_decay_up[i], rwkv_a0[i], rwkv_w_aaa_up[i], rwkv_w_gate_up[i],
                  rwkv_k_k[i], rwkv_k_a[i], rwkv_r_k[i], rwkv_lnx_w[i], rwkv_lnx_b[i])
            o_p, S_p, row_p, kw_p, vw_p = ab_mixer(
                hp, pos_p, jnp.zeros((Bp, P_A), hp.dtype), jnp.zeros((Bp, H_A, HD_A, HD_A), jnp.float32),
                None, None, ab_w_in[i], ab_w_out[i], rw, swa_sinks[i])
            o_s, S_s, row_s, kw_s, vw_s = ab_mixer(
                hs, pos_s, state_rwkv_shift[i], state_rwkv[i], cache_swa_k[i], cache_swa_v[i],
                ab_w_in[i], ab_w_out[i], rw, swa_sinks[i])
            rw_S_p.append(S_p); rw_sh_p.append(row_p); sw_k_p.append(kw_p); sw_v_p.append(vw_p)
            rw_S_s.append(S_s); rw_sh_s.append(row_s); sw_k_s.append(kw_s); sw_v_s.append(vw_s)
        else:
            j = l // 2
            q_p, k_p, v_p, lf_p = fox_project(hp, fox_w_in[j], fox_b_f[j])
            q_s, k_s, v_s, lf_s = fox_project(hs, fox_w_in[j], fox_b_f[j])
            o_p = mm3(fox_prompt(q_p, k_p, v_p, lf_p), fox_w_out[j])
            o_s = mm3(fox_sample(q_s, k_s, v_s, lf_s, cache_fox_k, cache_fox_v, cache_fox_logf, j, page_table),
                      fox_w_out[j])
            fk_p.append(k_p); fv_p.append(v_p); flf_p.append(lf_p)
            fk_s.append(k_s); fv_s.append(v_s); flf_s.append(lf_s)
        xp = xp + g1_p * o_p
        xs = xs + g1_s * o_s
        h2p = rmsnorm(xp, norm_w[l, 1]) * (1 + sc2_p) + sh2_p
        h2s = rmsnorm(xs, norm_w[l, 1]) * (1 + sc2_s) + sh2_s
        tok = jnp.concatenate([h2p.reshape(Bp * Tp, D), h2s.reshape(Bs * Ts, D)], axis=0)
        y = moe_ffn(tok, l, moe_w_router, moe_b_router, moe_w_gate, moe_w_up, moe_w_down,
                    shared_w_gate, shared_w_up, shared_w_down)
        xp = xp + g2_p * y[:Bp * Tp].reshape(Bp, Tp, D)
        xs = xs + g2_s * y[Bp * Tp:].reshape(Bs, Ts, D)
    y_prompt = rmsnorm(xp, final_norm_w)
    y_sample = rmsnorm(xs, final_norm_w)
    return (y_prompt, y_sample,
            jnp.stack(rw_S_p), jnp.stack(rw_sh_p), jnp.stack(sw_k_p), jnp.stack(sw_v_p),
            jnp.stack(fk_p), jnp.stack(fv_p), jnp.stack(flf_p),
            jnp.stack(rw_S_s), jnp.stack(rw_sh_s), jnp.stack(sw_k_s), jnp.stack(sw_v_s),
            jnp.stack(fk_s), jnp.stack(fv_s), jnp.stack(flf_s))
```

```python
import functools

import jax
import jax.numpy as jnp
from jax import lax
from jax.experimental import pallas as pl
from jax.experimental.pallas import tpu as pltpu

D_MODEL = 4096
PAGE_SIZE = 128

H_A = 32
HD_A = 64
C_A = H_A * HD_A
LORA_DECAY = 128
LORA_AAA = 128
LORA_GATE = 480
P_A = 3 * C_A + LORA_DECAY + LORA_AAA + LORA_GATE
RWKV_SPLIT = (C_A, 2 * C_A, 3 * C_A, 3 * C_A + LORA_DECAY, 3 * C_A + LORA_DECAY + LORA_AAA)
GN_EPS_A = 64e-5

H_B = 32
KVH_B = 4
G_B = H_B // KVH_B
HD_B = 64
C_B = H_B * HD_B
KV_B = KVH_B * HD_B
WINDOW = 128
ROPE_THETA = 10000.0

H_C = 32
KVH_C = 8
G_C = H_C // KVH_C
HD_C = 128
C_C = H_C * HD_C
KV_C = KVH_C * HD_C
Q_BLOCK = 128

N_EXPERTS = 64
TOP_K = 6
N_GROUPS = 8
TOPK_GROUPS = 4
ROUTED_SCALE = 2.5
MOE_BLOCK = 128

NORM_EPS = 1e-6

VMEM_LIMIT_BYTES = 48 * 1024 * 1024


MM_VMEM_BUDGET = 40 * 1024 * 1024


def _mm_body(x_ref, w_ref, o_ref):
    o_ref[...] = jnp.dot(x_ref[...], w_ref[...].astype(jnp.bfloat16), preferred_element_type=jnp.float32)


def _pick_tile(n, candidates):
    for c in candidates:
        if n % c == 0:
            return c
    return n


def _mm_tiles(M, K, N):
    tn = _pick_tile(N, (512, 256, 128))
    for tm in (1024, 768, 512, 256, 128):
        if M % tm == 0 and 2 * (tm * K * 2 + K * tn * 4 + tm * tn * 4) <= MM_VMEM_BUDGET:
            return tm, tn
    return M, tn


def matmul(x, w):
    M, K = x.shape
    _, N = w.shape
    tm, tn = _mm_tiles(M, K, N)
    return pl.pallas_call(
        _mm_body,
        grid=(M // tm, N // tn),
        in_specs=[pl.BlockSpec((tm, K), lambda i, j: (i, 0)),
                  pl.BlockSpec((K, tn), lambda i, j: (0, j))],
        out_specs=pl.BlockSpec((tm, tn), lambda i, j: (i, j)),
        out_shape=jax.ShapeDtypeStruct((M, N), jnp.float32),
        compiler_params=pltpu.CompilerParams(
            dimension_semantics=("parallel", "parallel"),
            vmem_limit_bytes=VMEM_LIMIT_BYTES),
        name="dense_matmul",
    )(x.astype(jnp.bfloat16), w)


def mm3(h, w):
    B, T, K = h.shape
    return matmul(h.reshape(B * T, K), w).reshape(B, T, w.shape[1])


HD = HD_A
PAIR = 2 * HD
RWKV_CHUNK = 64
RWKV_PAIRS_PER_STEP = 4


def _bf(x):
    return x.astype(jnp.bfloat16)


def _dot(a, b):
    return jnp.dot(_bf(a), _bf(b), preferred_element_type=jnp.float32)


def _dot_nt(a, b):
    return lax.dot_general(_bf(a), _bf(b), (((1,), (1,)), ((), ())), preferred_element_type=jnp.float32)


def _split3(x):
    h = x.astype(jnp.bfloat16)
    r1 = x - h.astype(jnp.float32)
    m = r1.astype(jnp.bfloat16)
    l = (r1 - m.astype(jnp.float32)).astype(jnp.bfloat16)
    return h, m, l


def _rwkv_body(s0_ref, r_ref, lw_ref, k_ref, v_ref, kk_ref, b_ref, y_ref, sT_ref, s_scr, *, C, hp):
    c = pl.program_id(2)

    @pl.when(c == 0)
    def _():
        s_scr[...] = s0_ref[...]

    n = 2 * C
    row = lax.broadcasted_iota(jnp.int32, (n, n), 0)
    col = lax.broadcasted_iota(jnp.int32, (n, n), 1)
    same_head = (row >= C) == (col >= C)
    tt = row & (C - 1)
    ss = col & (C - 1)
    strict = same_head & (ss < tt)
    incl = same_head & (ss <= tt)
    trow = lax.broadcasted_iota(jnp.int32, (C, C), 0)
    tcol = lax.broadcasted_iota(jnp.int32, (C, C), 1)
    tri = jnp.where(tcol <= trow, 1.0, 0.0).astype(jnp.bfloat16)
    lane = lax.broadcasted_iota(jnp.int32, (1, PAIR), 1)
    lo = jnp.where(lane < HD, 1.0, 0.0)
    hi = 1.0 - lo
    vrow = lax.broadcasted_iota(jnp.int32, (PAIR, PAIR), 0)
    vcol = lax.broadcasted_iota(jnp.int32, (PAIR, PAIR), 1)
    bd = (vrow >= HD) == (vcol >= HD)
    eye = jnp.where(row == col, 1.0, 0.0)
    nlev = C.bit_length() - 1

    P = range(hp)
    sls = [slice(p * PAIR, (p + 1) * PAIR) for p in P]
    lw = [lw_ref[:, s] for s in sls]
    r = [r_ref[:, s] for s in sls]
    k = [k_ref[:, s] for s in sls]
    v = [v_ref[:, s] for s in sls]
    kk = [kk_ref[:, s] for s in sls]
    b = [b_ref[:, s] for s in sls]
    csum = lambda x: jnp.dot(tri, x, preferred_element_type=jnp.float32)
    w3 = [_split3(x) for x in lw]
    Lc = [csum(h) + (csum(m) + csum(l)) for h, m, l in w3]
    e_inc = [jnp.exp(x) for x in Lc]
    e_neg = [jnp.exp(-x) for x in Lc]
    e_end = [jnp.exp(x[C - 1:C, :] - x) for x in Lc]
    Kq = [kk[p] * jnp.exp(Lc[p] - lw[p]) for p in P]
    Rq = [r[p] * e_inc[p] for p in P]
    Bd = [b[p] * e_neg[p] for p in P]
    Kd = [k[p] * e_neg[p] for p in P]
    G = [_dot_nt(jnp.concatenate([Kq[p] * lo, Kq[p] * hi, Rq[p] * lo, Rq[p] * hi], axis=0),
                 jnp.concatenate([Bd[p], Bd[p], Kd[p], Kd[p]], axis=0)) for p in P]
    Mb = [jnp.where(strict, g[:n, :n], 0.0) for g in G]
    Mk = [jnp.where(strict, g[:n, n:], 0.0) for g in G]
    Nb = [jnp.where(incl, g[n:, :n], 0.0) for g in G]
    Nk = [jnp.where(incl, g[n:, n:], 0.0) for g in G]
    T = [eye - jnp.where((tt >> 1) == (ss >> 1), m, 0.0) for m in Mb]
    for j in range(2, nlev + 1):
        lower_left = ((tt >> j) == (ss >> j)) & ((tt >> (j - 1)) > (ss >> (j - 1)))
        TM = [_dot(T[p], jnp.where(lower_left, Mb[p], 0.0)) for p in P]
        T = [T[p] - _dot(TM[p], T[p]) for p in P]
    V2 = [jnp.concatenate([x * lo, x * hi], axis=0) for x in v]
    MkV = [_dot(Mk[p], V2[p]) for p in P]
    NkV = [_dot(Nk[p], V2[p]) for p in P]
    BK = [jnp.concatenate([b[p] * e_end[p], k[p] * e_end[p]], axis=0) for p in P]
    S = [s_scr[p] for p in P]
    W0 = [_dot_nt(jnp.concatenate([Kq[p], Rq[p]], axis=0), S[p]) for p in P]
    rhs_u = [-(W0[p][:C] + (MkV[p][:C] + MkV[p][C:])) for p in P]
    UU = [_dot(T[p], jnp.concatenate([rhs_u[p] * lo, rhs_u[p] * hi], axis=0)) for p in P]
    YY = [_dot(Nb[p], UU[p]) + NkV[p] for p in P]
    UV = [jnp.concatenate([UU[p][:C] + UU[p][C:], v[p]], axis=0) for p in P]
    upd = [_dot(UV[p].T, BK[p]) for p in P]
    for p in P:
        y_ref[:, sls[p]] = YY[p][:C] + YY[p][C:] + W0[p][C:]
        S_new = S[p] * e_inc[p][C - 1:C, :] + jnp.where(bd, upd[p], 0.0)
        s_scr[p] = S_new
        sT_ref[p] = S_new


def rwkv7_scan(S0, r, lw, k, v, kk, b):
    B, T, CA = r.shape
    H = CA // HD
    npair = H // 2
    C = RWKV_CHUNK
    hp = min(RWKV_PAIRS_PER_STEP, npair)
    Tp = -(-T // C) * C
    if Tp != T:
        padf = lambda x: jnp.pad(x, ((0, 0), (0, Tp - T), (0, 0)))
        r, lw, k, v, kk, b = map(padf, (r, lw, k, v, kk, b))
    S0p = S0.reshape(B, npair, 2, HD, HD)
    z = jnp.zeros_like(S0p[:, :, 0])
    S0bd = jnp.concatenate([jnp.concatenate([S0p[:, :, 0], z], axis=-1),
                            jnp.concatenate([z, S0p[:, :, 1]], axis=-1)], axis=-2)
    seq_spec = pl.BlockSpec((None, C, hp * PAIR), lambda bb, g, c: (bb, c, g))
    st_spec = pl.BlockSpec((None, hp, PAIR, PAIR), lambda bb, g, c: (bb, g, 0, 0))
    y, Sbd = pl.pallas_call(
        functools.partial(_rwkv_body, C=C, hp=hp),
        grid=(B, npair // hp, Tp // C),
        in_specs=[st_spec] + [seq_spec] * 6,
        out_specs=[seq_spec, st_spec],
        out_shape=[jax.ShapeDtypeStruct((B, Tp, CA), jnp.float32),
                   jax.ShapeDtypeStruct((B, npair, PAIR, PAIR), jnp.float32)],
        scratch_shapes=[pltpu.VMEM((hp, PAIR, PAIR), jnp.float32)],
        compiler_params=pltpu.CompilerParams(
            dimension_semantics=("parallel", "parallel", "arbitrary")),
        name="rwkv7_chunk_scan",
    )(S0bd, r, lw, k, v, kk, b)
    S = jnp.stack([Sbd[:, :, :HD, :HD], Sbd[:, :, HD:, HD:]], axis=2).reshape(B, H, HD, HD)
    return y[:, :T], S


def rmsnorm(x, w):
    y = x * lax.rsqrt(jnp.mean(x * x, axis=-1, keepdims=True) + NORM_EPS)
    return y * w


def ada_params(c, w, b):
    m = matmul(jax.nn.silu(c), w) + b
    return jnp.split(m[:, None, :], 6, axis=-1)


def rope(x, pos):
    half = x.shape[-1] // 2
    inv = ROPE_THETA ** (-jnp.arange(half, dtype=jnp.float32) / half)
    ang = pos.astype(jnp.float32)[:, None] * inv[None, :]
    cos = jnp.cos(ang)[None, :, None, :]
    sin = jnp.sin(ang)[None, :, None, :]
    x1, x2 = x[..., :half], x[..., half:]
    return jnp.concatenate([x1 * cos - x2 * sin, x2 * cos + x1 * sin], axis=-1)


def rwkv7_mixer(pa, prev_row, S0, mu, w0, w_dec_up, a0, w_aaa_up, w_gate_up, k_k, k_a, r_k, lnx_w, lnx_b):
    B, T, _ = pa.shape
    shifted = jnp.concatenate([prev_row[:, None, :], pa[:, :-1]], axis=1)
    m = pa + (shifted - pa) * mu
    r, k, v, wd, ad, gd = jnp.split(m, RWKV_SPLIT, axis=-1)
    w = -jax.nn.softplus(-(w0 + mm3(jnp.tanh(wd), w_dec_up))) - 0.5
    a = jax.nn.sigmoid(a0 + mm3(ad, w_aaa_up))
    g = mm3(jax.nn.sigmoid(gd), w_gate_up)
    heads = lambda t: t.reshape(B, T, H_A, HD_A)
    kk = heads(k * k_k)
    kk = kk / jnp.maximum(jnp.sqrt(jnp.sum(kk * kk, axis=-1, keepdims=True)), 1e-12)
    k = k * (1 + (a - 1) * k_a)
    rh, kh, vh = heads(r), heads(k), heads(v)
    y, S = rwkv7_scan(S0, r, -jnp.exp(w), k, v, kk.reshape(B, T, C_A), kk.reshape(B, T, C_A) * a)
    y = heads(y)
    mean = jnp.mean(y, axis=-1, keepdims=True)
    var = jnp.mean(jnp.square(y - mean), axis=-1, keepdims=True)
    y = ((y - mean) * lax.rsqrt(var + GN_EPS_A)).reshape(B, T, C_A) * lnx_w + lnx_b
    bonus = (jnp.sum(rh * kh * r_k, axis=-1, keepdims=True) * vh).reshape(B, T, C_A)
    out = (y + bonus) * g
    return out, S, pa[:, -1]


def sink_softmax(s, sink):
    sk = jnp.broadcast_to(sink[:, :, None], s.shape[:-1])[..., None]
    return jax.nn.softmax(jnp.concatenate([s, sk], axis=-1), axis=-1)[..., :-1]


def swa_prompt(q, k, v, sinks):
    B, T = q.shape[:2]
    nb = T // WINDOW
    qb = q.reshape(B, nb, WINDOW, KVH_B, G_B, HD_B)
    kb = k.reshape(B, nb, WINDOW, KVH_B, HD_B)
    vb = v.reshape(B, nb, WINDOW, KVH_B, HD_B)
    pad = ((0, 0), (1, 0), (0, 0), (0, 0), (0, 0))
    kc = jnp.concatenate([jnp.pad(kb, pad)[:, :-1], kb], axis=2)
    vc = jnp.concatenate([jnp.pad(vb, pad)[:, :-1], vb], axis=2)
    s = jnp.einsum('bnqhgd,bnkhd->bnhgqk', qb, kc) * (HD_B ** -0.5)
    qi = jnp.arange(WINDOW)[:, None]
    ki = jnp.arange(2 * WINDOW)[None, :] - WINDOW
    band = (ki <= qi) & (qi - ki < WINDOW)
    valid = band[None] & ((jnp.arange(nb)[:, None, None] * WINDOW + ki[None]) >= 0)
    s = jnp.where(valid[None, :, None, None], s, -jnp.inf)
    p = sink_softmax(s, sinks.reshape(KVH_B, G_B))
    o = jnp.einsum('bnhgqk,bnkhd->bnqhgd', p, vc)
    return o.reshape(B, T, C_B), k[:, -WINDOW:], v[:, -WINDOW:]


def swa_sample(q, k, v, kbuf, vbuf, sinks):
    B, T = q.shape[:2]
    kc = jnp.concatenate([kbuf, k], axis=1)
    vc = jnp.concatenate([vbuf, v], axis=1)
    kpos = jnp.arange(WINDOW + T) - WINDOW
    qpos = jnp.arange(T)
    mask = (kpos[None, :] <= qpos[:, None]) & (qpos[:, None] - kpos[None, :] < WINDOW)
    s = jnp.einsum('bqhgd,bkhd->bhgqk', q.reshape(B, T, KVH_B, G_B, HD_B), kc) * (HD_B ** -0.5)
    s = jnp.where(mask, s, -jnp.inf)
    p = sink_softmax(s, sinks.reshape(KVH_B, G_B))
    o = jnp.einsum('bhgqk,bkhd->bqhgd', p, vc)
    return o.reshape(B, T, C_B), kc[:, -WINDOW:], vc[:, -WINDOW:]


def ab_mixer(h, pos, prev_row, S0, kbuf, vbuf, w_in, w_out, rwkv_params, sinks):
    B, T, _ = h.shape
    h2 = h.reshape(B * T, -1)
    w_parts = (w_in[:, :3 * C_A], w_in[:, 3 * C_A:P_A], w_in[:, P_A:P_A + C_B], w_in[:, P_A + C_B:])
    pa = jnp.concatenate([matmul(h2, w_parts[0]), matmul(h2, w_parts[1])], axis=-1).reshape(B, T, P_A)
    q = matmul(h2, w_parts[2]).reshape(B, T, C_B)
    kv = matmul(h2, w_parts[3]).reshape(B, T, 2 * KV_B)
    k, v = kv[..., :KV_B], kv[..., KV_B:]
    o_a, S, last_row = rwkv7_mixer(pa, prev_row, S0, *rwkv_params)
    q = rope(q.reshape(B, T, H_B, HD_B), pos)
    k = rope(k.reshape(B, T, KVH_B, HD_B), pos)
    v = v.reshape(B, T, KVH_B, HD_B)
    if kbuf is None:
        o_b, kw, vw = swa_prompt(q, k, v, sinks)
    else:
        o_b, kw, vw = swa_sample(q, k, v, kbuf, vbuf, sinks)
    out = mm3(jnp.concatenate([o_a, o_b], axis=-1), w_out)
    return out, S, last_row, kw, vw


def fox_project(h, w_in, b_f):
    B, T, _ = h.shape
    h2 = h.reshape(B * T, -1)
    q = matmul(h2, w_in[:, :C_C])
    kv = matmul(h2, w_in[:, C_C:C_C + 2 * KV_C])
    fl = matmul(h2, w_in[:, C_C + 2 * KV_C:])
    logf = jax.nn.log_sigmoid(fl + b_f).reshape(B, T, H_C)
    return (q.reshape(B, T, H_C, HD_C), kv[:, :KV_C].reshape(B, T, KVH_C, HD_C),
            kv[:, KV_C:].reshape(B, T, KVH_C, HD_C), logf)


NEG_BIG = -1e30
FOX_PAGES_PER_STEP = 8
FOX_Q_TILE = 256


def _dot3(parts, w):
    f = lambda x: jnp.dot(x, w, preferred_element_type=jnp.float32)
    return f(parts[0]) + (f(parts[1]) + f(parts[2]))


def _dot3_left(w, parts):
    f = lambda x: jnp.dot(w, x, preferred_element_type=jnp.float32)
    return f(parts[0]) + (f(parts[1]) + f(parts[2]))


def _upper_tri(n):
    r = lax.broadcasted_iota(jnp.int32, (n, n), 0)
    c = lax.broadcasted_iota(jnp.int32, (n, n), 1)
    return jnp.where(r <= c, 1.0, 0.0).astype(jnp.bfloat16)


def _online_softmax_step(s, V, m_ref, l_ref, acc_ref, rows):
    m_old = m_ref[rows]
    m_new = jnp.maximum(m_old, jnp.max(s, axis=1, keepdims=True))
    p = jnp.exp(s - jnp.concatenate([m_new] * (s.shape[1] // HD_C), axis=1))
    alpha = jnp.exp(m_old - m_new)
    l_ref[rows] = alpha * l_ref[rows] + jnp.sum(p, axis=1, keepdims=True)
    acc_ref[rows] = alpha * acc_ref[rows] + jnp.dot(_bf(p), V, preferred_element_type=jnp.float32)
    m_ref[rows] = m_new


def _fox_sample_body(pt_ref, q_ref, *rest, G, T_new):
    k_refs = rest[:G]
    v_refs = rest[G:2 * G]
    lf_refs = rest[2 * G:3 * G]
    knew_ref, vnew_ref, lfnew_ref, o_ref, m_ref, l_ref, acc_ref, carry_ref = rest[3 * G:]
    g = pl.program_id(1)
    n_groups = pl.num_programs(1) - 1
    R = G_C * T_new
    RQ = KVH_C * R

    @pl.when(g == 0)
    def _():
        m_ref[...] = jnp.full_like(m_ref, NEG_BIG)
        l_ref[...] = jnp.zeros_like(l_ref)
        acc_ref[...] = jnp.zeros_like(acc_ref)
        carry_ref[...] = jnp.zeros_like(carry_ref)

    q = _bf(q_ref[...] * (HD_C ** -0.5))
    tri = _upper_tri(PAGE_SIZE)
    er = lax.broadcasted_iota(jnp.int32, (RQ, H_C), 0)
    ec = lax.broadcasted_iota(jnp.int32, (RQ, H_C), 1)
    expand = jnp.where(er // T_new == ec, 1.0, 0.0).astype(jnp.bfloat16)

    def process(k_list, v_list, lf_list, mask):
        carry = carry_ref[...]
        Fs = []
        for lf_r in lf_list:
            Fp = _dot3(_split3(lf_r[...]), tri) + carry
            carry = Fp[:, PAGE_SIZE - 1:PAGE_SIZE]
            Fs.append(Fp)
        carry_ref[...] = carry
        F_all = Fs[0] if len(Fs) == 1 else jnp.concatenate(Fs, axis=1)
        bias = _dot3_left(expand, _split3(F_all))
        for kv in range(KVH_C):
            rows = slice(kv * R, (kv + 1) * R)
            kk = [kr[pl.ds(kv, PAGE_SIZE, stride=KVH_C), :] for kr in k_list]
            vv = [vr[pl.ds(kv, PAGE_SIZE, stride=KVH_C), :] for vr in v_list]
            K = _bf(kk[0] if len(kk) == 1 else jnp.concatenate(kk, axis=0))
            V = _bf(vv[0] if len(vv) == 1 else jnp.concatenate(vv, axis=0))
            s = lax.dot_general(q[rows], K, (((1,), (1,)), ((), ())), preferred_element_type=jnp.float32)
            s = s - bias[rows]
            if mask is not None:
                s = jnp.where(mask, s, NEG_BIG)
            _online_softmax_step(s, V, m_ref, l_ref, acc_ref, rows)

    @pl.when(g < n_groups)
    def _():
        process(k_refs, v_refs, lf_refs, None)

    @pl.when(g == n_groups)
    def _():
        tq = lax.broadcasted_iota(jnp.int32, (R, PAGE_SIZE), 0) % T_new
        key = lax.broadcasted_iota(jnp.int32, (R, PAGE_SIZE), 1)
        process([knew_ref], [vnew_ref], [lfnew_ref], key <= tq)
        o_ref[...] = acc_ref[...] / l_ref[...]


def fox_sample(q, k, v, logf, cache_k, cache_v, cache_lf, layer, page_table):
    B, T = q.shape[:2]
    NP = page_table.shape[1]
    G = min(FOX_PAGES_PER_STEP, NP)
    n_groups = NP // G
    n_pool = cache_k.shape[1]
    rows = PAGE_SIZE * KVH_C
    ck = cache_k.reshape(cache_k.shape[0], n_pool, rows, HD_C)
    cv = cache_v.reshape(cache_v.shape[0], n_pool, rows, HD_C)
    clf = jnp.swapaxes(cache_lf, -1, -2)
    qr = jnp.swapaxes(q, 1, 2).reshape(B, H_C * T, HD_C)
    padt = lambda x: jnp.pad(x, ((0, 0), (0, PAGE_SIZE - T), (0, 0), (0, 0))).reshape(B, rows, HD_C)
    knew, vnew = padt(k), padt(v)
    lfnew = jnp.pad(jnp.swapaxes(logf, 1, 2), ((0, 0), (0, 0), (0, PAGE_SIZE - T)))

    def page_spec(i, shape):
        return pl.BlockSpec((None, None) + shape,
                            lambda b, g, pt: (layer, pt[b, jnp.minimum(g, n_groups - 1) * G + i], 0, 0))

    per_b = lambda shape: pl.BlockSpec((None,) + shape, lambda b, g, pt: (b, 0, 0))
    in_specs = ([per_b((H_C * T, HD_C))]
                + [page_spec(i, (rows, HD_C)) for i in range(G)]
                + [page_spec(i, (rows, HD_C)) for i in range(G)]
                + [page_spec(i, (H_C, PAGE_SIZE)) for i in range(G)]
                + [per_b((rows, HD_C)), per_b((rows, HD_C)), per_b((H_C, PAGE_SIZE))])
    RQ = H_C * T
    o = pl.pallas_call(
        functools.partial(_fox_sample_body, G=G, T_new=T),
        grid_spec=pltpu.PrefetchScalarGridSpec(
            num_scalar_prefetch=1,
            grid=(B, n_groups + 1),
            in_specs=in_specs,
            out_specs=per_b((RQ, HD_C)),
            scratch_shapes=[pltpu.VMEM((RQ, HD_C), jnp.float32), pltpu.VMEM((RQ, HD_C), jnp.float32),
                            pltpu.VMEM((RQ, HD_C), jnp.float32), pltpu.VMEM((H_C, 1), jnp.float32)]),
        out_shape=jax.ShapeDtypeStruct((B, RQ, HD_C), jnp.float32),
        compiler_params=pltpu.CompilerParams(dimension_semantics=("parallel", "arbitrary"),
                                             vmem_limit_bytes=VMEM_LIMIT_BYTES),
        name="fox_sample_paged_attention",
    )(page_table, qr, *([ck] * G), *([cv] * G), *([clf] * G), knew, vnew, lfnew)
    return jnp.swapaxes(o.reshape(B, H_C, T, HD_C), 1, 2).reshape(B, T, H_C * HD_C)


def _fox_cumsum_body(lf_ref, f_ref, *, blk):
    T = lf_ref.shape[-1]
    tri = _upper_tri(blk)
    carry = jnp.zeros((lf_ref.shape[0], 1), jnp.float32)
    for j in range(T // blk):
        sl = slice(j * blk, (j + 1) * blk)
        Fp = _dot3(_split3(lf_ref[:, sl]), tri) + carry
        f_ref[:, sl] = Fp
        carry = Fp[:, blk - 1:blk]


def _fox_prompt_body(q_ref, k_ref, v_ref, f_ref, o_ref, qs_ref, m_ref, l_ref, acc_ref, *, tq):
    kv = pl.program_id(1)
    qi = pl.program_id(2)
    for g in range(G_C):
        qs_ref[g * tq:(g + 1) * tq, :] = _bf(q_ref[:, g, :] * (HD_C ** -0.5))
    m_ref[...] = jnp.full_like(m_ref, NEG_BIG)
    l_ref[...] = jnp.zeros_like(l_ref)
    acc_ref[...] = jnp.zeros_like(acc_ref)
    qpos = qi * tq + lax.broadcasted_iota(jnp.int32, (G_C * tq, tq), 0) % tq
    kcol = lax.broadcasted_iota(jnp.int32, (G_C * tq, tq), 1)

    def body(j, carry):
        ks = pl.multiple_of(j * tq, tq)
        K = _bf(k_ref[pl.ds(ks * KVH_C + kv, tq, stride=KVH_C), :])
        V = _bf(v_ref[pl.ds(ks * KVH_C + kv, tq, stride=KVH_C), :])
        s = lax.dot_general(qs_ref[...], K, (((1,), (1,)), ((), ())), preferred_element_type=jnp.float32)
        F = f_ref[:, pl.ds(ks, tq)]
        bias = jnp.concatenate([jnp.broadcast_to(F[g:g + 1, :], (tq, tq)) for g in range(G_C)], axis=0)
        s = jnp.where(kcol + ks <= qpos, s - bias, NEG_BIG)
        _online_softmax_step(s, V, m_ref, l_ref, acc_ref, slice(None))
        return carry

    lax.fori_loop(0, qi + 1, body, 0)
    o = acc_ref[...] / l_ref[...]
    for g in range(G_C):
        o_ref[:, g, :] = o[g * tq:(g + 1) * tq, :]


def fox_prompt(q, k, v, logf):
    B, T = q.shape[:2]
    tq = min(FOX_Q_TILE, T)
    lfT = jnp.swapaxes(logf, 1, 2)
    F = pl.pallas_call(
        functools.partial(_fox_cumsum_body, blk=tq),
        grid=(B,),
        in_specs=[pl.BlockSpec((None, H_C, T), lambda b: (b, 0, 0))],
        out_specs=pl.BlockSpec((None, H_C, T), lambda b: (b, 0, 0)),
        out_shape=jax.ShapeDtypeStruct((B, H_C, T), jnp.float32),
        name="fox_logf_cumsum",
    )(lfT)
    o = pl.pallas_call(
        functools.partial(_fox_prompt_body, tq=tq),
        grid=(B, KVH_C, T // tq),
        in_specs=[pl.BlockSpec((None, tq, None, G_C, HD_C), lambda b, kv, qi: (b, qi, kv, 0, 0)),
                  pl.BlockSpec((None, T * KVH_C, HD_C), lambda b, kv, qi: (b, 0, 0)),
                  pl.BlockSpec((None, T * KVH_C, HD_C), lambda b, kv, qi: (b, 0, 0)),
                  pl.BlockSpec((None, None, G_C, T), lambda b, kv, qi: (b, kv, 0, 0))],
        out_specs=pl.BlockSpec((None, tq, None, G_C, HD_C), lambda b, kv, qi: (b, qi, kv, 0, 0)),
        out_shape=jax.ShapeDtypeStruct((B, T, KVH_C, G_C, HD_C), jnp.float32),
        scratch_shapes=[pltpu.VMEM((G_C * tq, HD_C), jnp.bfloat16),
                        pltpu.VMEM((G_C * tq, HD_C), jnp.float32), pltpu.VMEM((G_C * tq, HD_C), jnp.float32),
                        pltpu.VMEM((G_C * tq, HD_C), jnp.float32)],
        compiler_params=pltpu.CompilerParams(dimension_semantics=("parallel", "parallel", "arbitrary"),
                                             vmem_limit_bytes=VMEM_LIMIT_BYTES),
        name="fox_prompt_flash_attention",
    )(q.reshape(B, T, KVH_C, G_C, HD_C), k.reshape(B, T * KVH_C, HD_C), v.reshape(B, T * KVH_C, HD_C),
      F.reshape(B, KVH_C, G_C, T))
    return o.reshape(B, T, H_C * HD_C)


MOE_ROWS = 256
MOE_COMBINE_TOKENS = 64
VMEM_LIMIT_MOE = 56 * 1024 * 1024


def _moe_up_body(blk_exp_ref, nused_ref, row_tok_ref, roww_ref, x_hbm, wg_ref, wu_ref, act_ref,
                 xbuf, wg_bf, wu_bf, sem, *, RB):
    b = pl.program_id(0)
    nused = nused_ref[0]
    slot = b % 2

    def gather(blk, slot_):
        def body(i, carry):
            t = row_tok_ref[blk * RB + i]
            pltpu.make_async_copy(x_hbm.at[pl.ds(t, 1)], xbuf.at[slot_, pl.ds(i, 1)], sem.at[slot_]).start()
            return carry
        lax.fori_loop(0, RB, body, 0, unroll=8)

    @pl.when(b == 0)
    def _():
        gather(0, 0)

    @pl.when(b + 1 < nused)
    def _():
        gather(b + 1, 1 - slot)

    @pl.when(b < nused)
    def _():
        pltpu.make_async_copy(x_hbm.at[pl.ds(0, RB)], xbuf.at[slot], sem.at[slot]).wait()
        e = blk_exp_ref[b]
        e_prev = blk_exp_ref[jnp.maximum(b - 1, 0)]

        @pl.when((b == 0) | (e != e_prev))
        def _():
            wg_bf[...] = wg_ref[...].astype(jnp.bfloat16)
            wu_bf[...] = wu_ref[...].astype(jnp.bfloat16)

        x = xbuf[slot].astype(jnp.bfloat16)
        g = jnp.dot(x, wg_bf[...], preferred_element_type=jnp.float32)
        u = jnp.dot(x, wu_bf[...], preferred_element_type=jnp.float32)
        act_ref[...] = (g * jax.nn.sigmoid(g)) * u * roww_ref[...]

    @pl.when(b >= nused)
    def _():
        act_ref[...] = jnp.zeros_like(act_ref)


def _moe_down_body(blk_exp_ref, nused_ref, act_ref, wd_ref, y_ref, wd_bf):
    b = pl.program_id(0)
    nused = nused_ref[0]

    @pl.when(b < nused)
    def _():
        e = blk_exp_ref[b]
        e_prev = blk_exp_ref[jnp.maximum(b - 1, 0)]

        @pl.when((b == 0) | (e != e_prev))
        def _():
            wd_bf[...] = wd_ref[...].astype(jnp.bfloat16)

        y_ref[...] = jnp.dot(act_ref[...].astype(jnp.bfloat16), wd_bf[...], preferred_element_type=jnp.float32)

    @pl.when(b >= nused)
    def _():
        y_ref[...] = jnp.zeros_like(y_ref)


def _moe_combine_body(pos_ref, y_hbm, shared_ref, out_ref, buf, sem, *, TB):
    b = pl.program_id(0)
    nb = pl.num_programs(0)
    slot = b % 2

    def gather(blk, slot_):
        def body(i, carry):
            for kk in range(TOP_K):
                p = pos_ref[(blk * TB + i) * TOP_K + kk]
                pltpu.make_async_copy(y_hbm.at[pl.ds(p, 1)], buf.at[slot_, pl.ds(kk * TB + i, 1)],
                                      sem.at[slot_]).start()
            return carry
        lax.fori_loop(0, TB, body, 0, unroll=2)

    @pl.when(b == 0)
    def _():
        gather(0, 0)

    @pl.when(b + 1 < nb)
    def _():
        gather(b + 1, 1 - slot)

    pltpu.make_async_copy(y_hbm.at[pl.ds(0, TOP_K * TB)], buf.at[slot], sem.at[slot]).wait()
    acc = shared_ref[...]
    for kk in range(TOP_K):
        acc = acc + buf[slot, kk * TB:(kk + 1) * TB]
    out_ref[...] = acc


def _moe_invert_body(pos_ref, src_ref):
    def clear(i, carry):
        src_ref[i] = -1
        return carry
    lax.fori_loop(0, src_ref.shape[0], clear, 0, unroll=8)

    def put(i, carry):
        src_ref[pos_ref[i]] = i
        return carry
    lax.fori_loop(0, pos_ref.shape[0], put, 0, unroll=8)


ROUTE_TOKENS = 256
NEG_INF = float("-inf")


def _first_max(x, idx, axis, n):
    m = jnp.max(x, axis=axis, keepdims=True)
    first = jnp.min(jnp.where(x == m, idx, n), axis=axis, keepdims=True)
    return m, first


def _moe_router_body(wt_ref, h_ref, bias_ref, eidx_ref, gate_ref):
    tn = h_ref.shape[0]
    per_group = N_EXPERTS // N_GROUPS
    logits = lax.dot_general(_bf(wt_ref[...]), _bf(h_ref[...]), (((1,), (1,)), ((), ())),
                             preferred_element_type=jnp.float32)
    scores = jax.nn.sigmoid(logits)
    biased = scores + bias_ref[:, :1]
    b3 = biased.reshape(N_GROUPS, per_group, tn)
    i3 = lax.broadcasted_iota(jnp.int32, b3.shape, 1)
    m1, f1 = _first_max(b3, i3, 1, per_group)
    m2 = jnp.max(jnp.where(i3 == f1, NEG_INF, b3), axis=1, keepdims=True)
    grp = (m1 + m2).reshape(N_GROUPS, tn)
    gi = lax.broadcasted_iota(jnp.int32, grp.shape, 0)
    keep = jnp.zeros(grp.shape, jnp.float32)
    for _ in range(TOPK_GROUPS):
        _, f = _first_max(grp, gi, 0, N_GROUPS)
        keep = jnp.where(gi == f, 1.0, keep)
        grp = jnp.where(gi == f, NEG_INF, grp)
    cand = jnp.where(keep.reshape(N_GROUPS, 1, tn) > 0.5, b3, NEG_INF).reshape(N_EXPERTS, tn)
    ei = lax.broadcasted_iota(jnp.int32, cand.shape, 0)
    ids, gs = [], []
    for _ in range(TOP_K):
        _, f = _first_max(cand, ei, 0, N_EXPERTS)
        ids.append(f)
        gs.append(jnp.sum(jnp.where(ei == f, scores, 0.0), axis=0, keepdims=True))
        cand = jnp.where(ei == f, NEG_INF, cand)
    tot = gs[0]
    for x in gs[1:]:
        tot = tot + x
    pad = eidx_ref.shape[0] - TOP_K
    eidx_ref[...] = jnp.concatenate(ids + [jnp.zeros((pad, tn), jnp.int32)], axis=0)
    gate_ref[...] = jnp.concatenate([x / tot * ROUTED_SCALE for x in gs] + [jnp.zeros((pad, tn), jnp.float32)],
                                    axis=0)


def moe_route(h, w_router, b_router):
    T, D = h.shape
    tn = ROUTE_TOKENS
    rows = 8
    bias = jnp.broadcast_to(b_router[:, None], (N_EXPERTS, 128))
    eidx, gates = pl.pallas_call(
        _moe_router_body,
        grid=(T // tn,),
        in_specs=[pl.BlockSpec((N_EXPERTS, D), lambda i: (0, 0)),
                  pl.BlockSpec((tn, D), lambda i: (i, 0)),
                  pl.BlockSpec((N_EXPERTS, 128), lambda i: (0, 0))],
        out_specs=[pl.BlockSpec((rows, tn), lambda i: (0, i)), pl.BlockSpec((rows, tn), lambda i: (0, i))],
        out_shape=[jax.ShapeDtypeStruct((rows, T), jnp.int32), jax.ShapeDtypeStruct((rows, T), jnp.float32)],
        compiler_params=pltpu.CompilerParams(dimension_semantics=("parallel",)),
        name="moe_router_topk",
    )(w_router.T, h, bias)
    return eidx[:TOP_K].T, gates[:TOP_K].T


def moe_layout(eidx, gates, RB):
    T = eidx.shape[0]
    A = T * TOP_K
    E = N_EXPERTS
    e_flat = eidx.reshape(A).astype(jnp.int32)
    g_flat = gates.reshape(A)
    onehot = (e_flat[:, None] == jnp.arange(E, dtype=jnp.int32)[None, :]).astype(jnp.int32)
    rank_incl = jnp.cumsum(onehot, axis=0)
    rank = jnp.sum(onehot * rank_incl, axis=1) - 1
    counts = rank_incl[-1]
    padded = (counts + RB - 1) // RB * RB
    seg_start = jnp.cumsum(counts) - counts
    pad_end = jnp.cumsum(padded)
    pad_start = pad_end - padded
    pos = pad_start[e_flat] + rank
    n_blocks = -(-(A + E * (RB - 1)) // RB)
    n_rows = n_blocks * RB
    blk_exp = jnp.minimum(jnp.searchsorted(pad_end, jnp.arange(n_blocks, dtype=jnp.int32) * RB, side='right'),
                          E - 1).astype(jnp.int32)
    nused = (pad_end[-1] // RB).astype(jnp.int32).reshape(1)
    pos = pos.astype(jnp.int32)
    src = pl.pallas_call(
        _moe_invert_body,
        grid_spec=pltpu.PrefetchScalarGridSpec(
            num_scalar_prefetch=1, grid=(1,), in_specs=[],
            out_specs=pl.BlockSpec(memory_space=pltpu.SMEM)),
        out_shape=jax.ShapeDtypeStruct((n_rows,), jnp.int32),
        name="moe_invert_positions",
    )(pos)
    valid = src >= 0
    row_tok = jnp.where(valid, src // TOP_K, 0).astype(jnp.int32)
    row_w = jnp.where(valid, g_flat[jnp.maximum(src, 0)], 0.0).astype(jnp.float32)
    return pos, row_tok, row_w, blk_exp, nused, n_blocks


def moe_routed_plus_shared(h, layer, eidx, gates, w_gate, w_up, w_down, shared):
    T, D = h.shape
    F = w_gate.shape[-1]
    RB = MOE_ROWS
    TB = MOE_COMBINE_TOKENS
    pos, row_tok, row_w, blk_exp, nused, n_blocks = moe_layout(eidx, gates, RB)
    n_rows = n_blocks * RB
    wspec_up = pl.BlockSpec((None, None, D, F), lambda b, be, nu, rt: (layer, be[b], 0, 0))
    act = pl.pallas_call(
        functools.partial(_moe_up_body, RB=RB),
        grid_spec=pltpu.PrefetchScalarGridSpec(
            num_scalar_prefetch=3,
            grid=(n_blocks,),
            in_specs=[pl.BlockSpec((RB, 1), lambda b, be, nu, rt: (b, 0)),
                      pl.BlockSpec(memory_space=pl.ANY),
                      wspec_up, wspec_up],
            out_specs=pl.BlockSpec((RB, F), lambda b, be, nu, rt: (b, 0)),
            scratch_shapes=[pltpu.VMEM((2, RB, D), jnp.float32),
                            pltpu.VMEM((D, F), jnp.bfloat16),
                            pltpu.VMEM((D, F), jnp.bfloat16),
                            pltpu.SemaphoreType.DMA((2,))]),
        out_shape=jax.ShapeDtypeStruct((n_rows, F), jnp.float32),
        compiler_params=pltpu.CompilerParams(dimension_semantics=("arbitrary",),
                                             vmem_limit_bytes=VMEM_LIMIT_MOE),
        name="moe_gate_up",
    )(blk_exp, nused, row_tok, row_w.reshape(n_rows, 1), h, w_gate, w_up)
    y = pl.pallas_call(
        _moe_down_body,
        grid_spec=pltpu.PrefetchScalarGridSpec(
            num_scalar_prefetch=2,
            grid=(n_blocks,),
            in_specs=[pl.BlockSpec((RB, F), lambda b, be, nu: (b, 0)),
                      pl.BlockSpec((None, None, F, D), lambda b, be, nu: (layer, be[b], 0, 0))],
            out_specs=pl.BlockSpec((RB, D), lambda b, be, nu: (b, 0)),
            scratch_shapes=[pltpu.VMEM((F, D), jnp.bfloat16)]),
        out_shape=jax.ShapeDtypeStruct((n_rows, D), jnp.float32),
        compiler_params=pltpu.CompilerParams(dimension_semantics=("arbitrary",),
                                             vmem_limit_bytes=VMEM_LIMIT_MOE),
        name="moe_down",
    )(blk_exp, nused, act, w_down)
    return pl.pallas_call(
        functools.partial(_moe_combine_body, TB=TB),
        grid_spec=pltpu.PrefetchScalarGridSpec(
            num_scalar_prefetch=1,
            grid=(T // TB,),
            in_specs=[pl.BlockSpec(memory_space=pl.ANY),
                      pl.BlockSpec((TB, D), lambda b, ps: (b, 0))],
            out_specs=pl.BlockSpec((TB, D), lambda b, ps: (b, 0)),
            scratch_shapes=[pltpu.VMEM((2, TOP_K * TB, D), jnp.float32),
                            pltpu.SemaphoreType.DMA((2,))]),
        out_shape=jax.ShapeDtypeStruct((T, D), jnp.float32),
        compiler_params=pltpu.CompilerParams(dimension_semantics=("arbitrary",),
                                             vmem_limit_bytes=VMEM_LIMIT_MOE),
        name="moe_combine",
    )(pos, y, shared)


def moe_ffn(h, layer, w_router, b_router, w_gate, w_up, w_down, ws_gate, ws_up, ws_down):
    eidx, gates = moe_route(h, w_router[layer], b_router[layer])
    shared = matmul(jax.nn.silu(matmul(h, ws_gate[layer])) * matmul(h, ws_up[layer]), ws_down[layer])
    return moe_routed_plus_shared(h, layer, eidx, gates, w_gate, w_up, w_down, shared)


def kernel(x_prompt, x_sample, c_prompt, c_sample, state_rwkv, state_rwkv_shift, cache_swa_k, cache_swa_v, cache_fox_k, cache_fox_v, cache_fox_logf, page_table, ada_w, ada_b, norm_w, final_norm_w, ab_w_in, ab_w_out, rwkv_mu, rwkv_w0, rwkv_w_decay_up, rwkv_a0, rwkv_w_aaa_up, rwkv_w_gate_up, rwkv_k_k, rwkv_k_a, rwkv_r_k, rwkv_lnx_w, rwkv_lnx_b, swa_sinks, fox_w_in, fox_b_f, fox_w_out, moe_w_router, moe_b_router, moe_w_gate, moe_w_up, moe_w_down, shared_w_gate, shared_w_up, shared_w_down):
    Bp, Tp, D = x_prompt.shape
    Bs, Ts, _ = x_sample.shape
    depth = ada_w.shape[0]
    past_len = page_table.shape[1] * PAGE_SIZE
    pos_p = jnp.arange(Tp)
    pos_s = past_len + jnp.arange(Ts)
    xp, xs = x_prompt, x_sample
    rw_S_p, rw_sh_p, sw_k_p, sw_v_p, fk_p, fv_p, flf_p = [], [], [], [], [], [], []
    rw_S_s, rw_sh_s, sw_k_s, sw_v_s, fk_s, fv_s, flf_s = [], [], [], [], [], [], []
    for l in range(depth):
        mods = ada_params(jnp.concatenate([c_prompt, c_sample], axis=0), ada_w[l], ada_b[l])
        sh1_p, sc1_p, g1_p, sh2_p, sc2_p, g2_p = [m[:Bp] for m in mods]
        sh1_s, sc1_s, g1_s, sh2_s, sc2_s, g2_s = [m[Bp:] for m in mods]
        hp = rmsnorm(xp, norm_w[l, 0]) * (1 + sc1_p) + sh1_p
        hs = rmsnorm(xs, norm_w[l, 0]) * (1 + sc1_s) + sh1_s
        if l % 2 == 0:
            i = l // 2
            rw = (rwkv_mu[i], rwkv_w0[i], rwkv_w_decay_up[i], rwkv_a0[i], rwkv_w_aaa_up[i], rwkv_w_gate_up[i],
                  rwkv_k_k[i], rwkv_k_a[i], rwkv_r_k[i], rwkv_lnx_w[i], rwkv_lnx_b[i])
            o_p, S_p, row_p, kw_p, vw_p = ab_mixer(
                hp, pos_p, jnp.zeros((Bp, P_A), hp.dtype), jnp.zeros((Bp, H_A, HD_A, HD_A), jnp.float32),
                None, None, ab_w_in[i], ab_w_out[i], rw, swa_sinks[i])
            o_s, S_s, row_s, kw_s, vw_s = ab_mixer(
                hs, pos_s, state_rwkv_shift[i], state_rwkv[i], cache_swa_k[i], cache_swa_v[i],
                ab_w_in[i], ab_w_out[i], rw, swa_sinks[i])
            rw_S_p.append(S_p); rw_sh_p.append(row_p); sw_k_p.append(kw_p); sw_v_p.append(vw_p)
            rw_S_s.append(S_s); rw_sh_s.append(row_s); sw_k_s.append(kw_s); sw_v_s.append(vw_s)
        else:
            j = l // 2
            q_p, k_p, v_p, lf_p = fox_project(hp, fox_w_in[j], fox_b_f[j])
            q_s, k_s, v_s, lf_s = fox_project(hs, fox_w_in[j], fox_b_f[j])
            o_p = mm3(fox_prompt(q_p, k_p, v_p, lf_p), fox_w_out[j])
            o_s = mm3(fox_sample(q_s, k_s, v_s, lf_s, cache_fox_k, cache_fox_v, cache_fox_logf, j, page_table),
                      fox_w_out[j])
            fk_p.append(k_p); fv_p.append(v_p); flf_p.append(lf_p)
            fk_s.append(k_s); fv_s.append(v_s); flf_s.append(lf_s)
        xp = xp + g1_p * o_p
        xs = xs + g1_s * o_s
        h2p = rmsnorm(xp, norm_w[l, 1]) * (1 + sc2_p) + sh2_p
        h2s = rmsnorm(xs, norm_w[l, 1]) * (1 + sc2_s) + sh2_s
        tok = jnp.concatenate([h2p.reshape(Bp * Tp, D), h2s.reshape(Bs * Ts, D)], axis=0)
        y = moe_ffn(tok, l, moe_w_router, moe_b_router, moe_w_gate, moe_w_up, moe_w_down,
                    shared_w_gate, shared_w_up, shared_w_down)
        xp = xp + g2_p * y[:Bp * Tp].reshape(Bp, Tp, D)
        xs = xs + g2_s * y[Bp * Tp:].reshape(Bs, Ts, D)
    y_prompt = rmsnorm(xp, final_norm_w)
    y_sample = rmsnorm(xs, final_norm_w)
    return (y_prompt, y_sample,
            jnp.stack(rw_S_p), jnp.stack(rw_sh_p), jnp.stack(sw_k_p), jnp.stack(sw_v_p),
            jnp.stack(fk_p), jnp.stack(fv_p), jnp.stack(flf_p),
            jnp.stack(rw_S_s), jnp.stack(rw_sh_s), jnp.stack(sw_k_s), jnp.stack(sw_v_s),
            jnp.stack(fk_s), jnp.stack(fv_s), jnp.stack(flf_s))
```

```python
import functools

import jax
import jax.numpy as jnp
from jax import lax
from jax.experimental import pallas as pl
from jax.experimental.pallas import tpu as pltpu

D_MODEL = 4096
PAGE_SIZE = 128

H_A = 32
HD_A = 64
C_A = H_A * HD_A
LORA_DECAY = 128
LORA_AAA = 128
LORA_GATE = 480
P_A = 3 * C_A + LORA_DECAY + LORA_AAA + LORA_GATE
RWKV_SPLIT = (C_A, 2 * C_A, 3 * C_A, 3 * C_A + LORA_DECAY, 3 * C_A + LORA_DECAY + LORA_AAA)
GN_EPS_A = 64e-5

H_B = 32
KVH_B = 4
G_B = H_B // KVH_B
HD_B = 64
C_B = H_B * HD_B
KV_B = KVH_B * HD_B
WINDOW = 128
ROPE_THETA = 10000.0

H_C = 32
KVH_C = 8
G_C = H_C // KVH_C
HD_C = 128
C_C = H_C * HD_C
KV_C = KVH_C * HD_C
Q_BLOCK = 128

N_EXPERTS = 64
TOP_K = 6
N_GROUPS = 8
TOPK_GROUPS = 4
ROUTED_SCALE = 2.5
MOE_BLOCK = 128

NORM_EPS = 1e-6

VMEM_LIMIT_BYTES = 48 * 1024 * 1024
VMEM_LIMIT_LARGE = 56 * 1024 * 1024


MM_VMEM_BUDGET = 40 * 1024 * 1024


def _mm_body(x_ref, w_ref, o_ref):
    o_ref[...] = jnp.dot(x_ref[...], w_ref[...].astype(jnp.bfloat16), preferred_element_type=jnp.float32)


def _pick_tile(n, candidates):
    for c in candidates:
        if n % c == 0:
            return c
    return n


def _mm_tiles(M, K, N):
    tn = _pick_tile(N, (512, 256, 128))
    for tm in (1024, 768, 512, 256, 128):
        if M % tm == 0 and 2 * (tm * K * 2 + K * tn * 4 + tm * tn * 4) <= MM_VMEM_BUDGET:
            return tm, tn
    return M, tn


def matmul(x, w, layer=None):
    M, K = x.shape
    N = w.shape[-1]
    if layer is None:
        w_spec = lambda tn: pl.BlockSpec((K, tn), lambda i, j: (0, j))
    else:
        w_spec = lambda tn: pl.BlockSpec((None, K, tn), lambda i, j: (layer, 0, j))
    tm, tn = _mm_tiles(M, K, N)
    return pl.pallas_call(
        _mm_body,
        grid=(M // tm, N // tn),
        in_specs=[pl.BlockSpec((tm, K), lambda i, j: (i, 0)), w_spec(tn)],
        out_specs=pl.BlockSpec((tm, tn), lambda i, j: (i, j)),
        out_shape=jax.ShapeDtypeStruct((M, N), jnp.float32),
        compiler_params=pltpu.CompilerParams(
            dimension_semantics=("parallel", "parallel"),
            vmem_limit_bytes=VMEM_LIMIT_BYTES),
        name="dense_matmul",
    )(x.astype(jnp.bfloat16), w)


def mm3(h, w):
    B, T, K = h.shape
    return matmul(h.reshape(B * T, K), w).reshape(B, T, w.shape[1])


HD = HD_A
PAIR = 2 * HD
RWKV_CHUNK = 64
RWKV_PAIRS_PER_STEP = 4


def _bf(x):
    return x.astype(jnp.bfloat16)


def _dot(a, b):
    return jnp.dot(_bf(a), _bf(b), preferred_element_type=jnp.float32)


def _dot_nt(a, b):
    return lax.dot_general(_bf(a), _bf(b), (((1,), (1,)), ((), ())), preferred_element_type=jnp.float32)


def _split3(x):
    h = x.astype(jnp.bfloat16)
    r1 = x - h.astype(jnp.float32)
    m = r1.astype(jnp.bfloat16)
    l = (r1 - m.astype(jnp.float32)).astype(jnp.bfloat16)
    return h, m, l


def _rwkv_body(s0_ref, r_ref, lw_ref, k_ref, v_ref, kk_ref, b_ref, y_ref, sT_ref, s_scr, *, C, hp):
    c = pl.program_id(2)

    @pl.when(c == 0)
    def _():
        s_scr[...] = s0_ref[...]

    n = 2 * C
    row = lax.broadcasted_iota(jnp.int32, (n, n), 0)
    col = lax.broadcasted_iota(jnp.int32, (n, n), 1)
    same_head = (row >= C) == (col >= C)
    tt = row & (C - 1)
    ss = col & (C - 1)
    strict = same_head & (ss < tt)
    incl = same_head & (ss <= tt)
    trow = lax.broadcasted_iota(jnp.int32, (C, C), 0)
    tcol = lax.broadcasted_iota(jnp.int32, (C, C), 1)
    tri = jnp.where(tcol <= trow, 1.0, 0.0).astype(jnp.bfloat16)
    lane = lax.broadcasted_iota(jnp.int32, (1, PAIR), 1)
    lo = jnp.where(lane < HD, 1.0, 0.0)
    hi = 1.0 - lo
    vrow = lax.broadcasted_iota(jnp.int32, (PAIR, PAIR), 0)
    vcol = lax.broadcasted_iota(jnp.int32, (PAIR, PAIR), 1)
    bd = (vrow >= HD) == (vcol >= HD)
    eye = jnp.where(row == col, 1.0, 0.0)
    nlev = C.bit_length() - 1

    P = range(hp)
    sls = [slice(p * PAIR, (p + 1) * PAIR) for p in P]
    lw = [lw_ref[:, s] for s in sls]
    r = [r_ref[:, s] for s in sls]
    k = [k_ref[:, s] for s in sls]
    v = [v_ref[:, s] for s in sls]
    kk = [kk_ref[:, s] for s in sls]
    b = [b_ref[:, s] for s in sls]
    csum = lambda x: jnp.dot(tri, x, preferred_element_type=jnp.float32)
    w3 = [_split3(x) for x in lw]
    Lc = [csum(h) + (csum(m) + csum(l)) for h, m, l in w3]
    e_inc = [jnp.exp(x) for x in Lc]
    e_neg = [jnp.exp(-x) for x in Lc]
    e_end = [jnp.exp(x[C - 1:C, :] - x) for x in Lc]
    Kq = [kk[p] * jnp.exp(Lc[p] - lw[p]) for p in P]
    Rq = [r[p] * e_inc[p] for p in P]
    Bd = [b[p] * e_neg[p] for p in P]
    Kd = [k[p] * e_neg[p] for p in P]
    G = [_dot_nt(jnp.concatenate([Kq[p] * lo, Kq[p] * hi, Rq[p] * lo, Rq[p] * hi], axis=0),
                 jnp.concatenate([Bd[p], Bd[p], Kd[p], Kd[p]], axis=0)) for p in P]
    Mb = [jnp.where(strict, g[:n, :n], 0.0) for g in G]
    Mk = [jnp.where(strict, g[:n, n:], 0.0) for g in G]
    Nb = [jnp.where(incl, g[n:, :n], 0.0) for g in G]
    Nk = [jnp.where(incl, g[n:, n:], 0.0) for g in G]
    T = [eye - jnp.where((tt >> 1) == (ss >> 1), m, 0.0) for m in Mb]
    for j in range(2, nlev + 1):
        lower_left = ((tt >> j) == (ss >> j)) & ((tt >> (j - 1)) > (ss >> (j - 1)))
        TM = [_dot(T[p], jnp.where(lower_left, Mb[p], 0.0)) for p in P]
        T = [T[p] - _dot(TM[p], T[p]) for p in P]
    V2 = [jnp.concatenate([x * lo, x * hi], axis=0) for x in v]
    MkV = [_dot(Mk[p], V2[p]) for p in P]
    NkV = [_dot(Nk[p], V2[p]) for p in P]
    BK = [jnp.concatenate([b[p] * e_end[p], k[p] * e_end[p]], axis=0) for p in P]
    S = [s_scr[p] for p in P]
    W0 = [_dot_nt(jnp.concatenate([Kq[p], Rq[p]], axis=0), S[p]) for p in P]
    rhs_u = [-(W0[p][:C] + (MkV[p][:C] + MkV[p][C:])) for p in P]
    UU = [_dot(T[p], jnp.concatenate([rhs_u[p] * lo, rhs_u[p] * hi], axis=0)) for p in P]
    YY = [_dot(Nb[p], UU[p]) + NkV[p] for p in P]
    UV = [jnp.concatenate([UU[p][:C] + UU[p][C:], v[p]], axis=0) for p in P]
    upd = [_dot(UV[p].T, BK[p]) for p in P]
    for p in P:
        y_ref[:, sls[p]] = YY[p][:C] + YY[p][C:] + W0[p][C:]
        S_new = S[p] * e_inc[p][C - 1:C, :] + jnp.where(bd, upd[p], 0.0)
        s_scr[p] = S_new
        sT_ref[p] = S_new


def rwkv7_scan(S0, r, lw, k, v, kk, b):
    B, T, CA = r.shape
    H = CA // HD
    npair = H // 2
    C = RWKV_CHUNK
    hp = min(RWKV_PAIRS_PER_STEP, npair)
    Tp = -(-T // C) * C
    if Tp != T:
        padf = lambda x: jnp.pad(x, ((0, 0), (0, Tp - T), (0, 0)))
        r, lw, k, v, kk, b = map(padf, (r, lw, k, v, kk, b))
    S0p = S0.reshape(B, npair, 2, HD, HD)
    z = jnp.zeros_like(S0p[:, :, 0])
    S0bd = jnp.concatenate([jnp.concatenate([S0p[:, :, 0], z], axis=-1),
                            jnp.concatenate([z, S0p[:, :, 1]], axis=-1)], axis=-2)
    seq_spec = pl.BlockSpec((None, C, hp * PAIR), lambda bb, g, c: (bb, c, g))
    st_spec = pl.BlockSpec((None, hp, PAIR, PAIR), lambda bb, g, c: (bb, g, 0, 0))
    y, Sbd = pl.pallas_call(
        functools.partial(_rwkv_body, C=C, hp=hp),
        grid=(B, npair // hp, Tp // C),
        in_specs=[st_spec] + [seq_spec] * 6,
        out_specs=[seq_spec, st_spec],
        out_shape=[jax.ShapeDtypeStruct((B, Tp, CA), jnp.float32),
                   jax.ShapeDtypeStruct((B, npair, PAIR, PAIR), jnp.float32)],
        scratch_shapes=[pltpu.VMEM((hp, PAIR, PAIR), jnp.float32)],
        compiler_params=pltpu.CompilerParams(
            dimension_semantics=("parallel", "parallel", "arbitrary")),
        name="rwkv7_chunk_scan",
    )(S0bd, r, lw, k, v, kk, b)
    S = jnp.stack([Sbd[:, :, :HD, :HD], Sbd[:, :, HD:, HD:]], axis=2).reshape(B, H, HD, HD)
    return y[:, :T], S


NORM_ROWS = 256


def _norm_body(x_ref, w_ref, *rest):
    o_ref = rest[-1]
    x = x_ref[...]
    y = x * lax.rsqrt(jnp.mean(x * x, axis=-1, keepdims=True) + NORM_EPS) * w_ref[...]
    if len(rest) == 3:
        y = y * (1 + rest[0][...]) + rest[1][...]
    o_ref[...] = y.astype(o_ref.dtype)


def rmsnorm(x, w, scale=None, shift=None, out_dtype=jnp.float32):
    B, T, D = x.shape
    tt = min(NORM_ROWS, T)
    row_spec = pl.BlockSpec((None, tt, D), lambda b, t: (b, t, 0))
    mod_spec = pl.BlockSpec((None, 1, D), lambda b, t: (b, 0, 0))
    mods = [] if scale is None else [scale, shift]
    return pl.pallas_call(
        _norm_body,
        grid=(B, T // tt),
        in_specs=[row_spec, pl.BlockSpec((1, D), lambda b, t: (0, 0))] + [mod_spec] * len(mods),
        out_specs=row_spec,
        out_shape=jax.ShapeDtypeStruct((B, T, D), out_dtype),
        compiler_params=pltpu.CompilerParams(dimension_semantics=("parallel", "parallel")),
        name="rmsnorm_modulate",
    )(x, w.reshape(1, D), *mods)


def ada_params(c, w, layer, b):
    m = matmul(jax.nn.silu(c), w, layer) + b
    return jnp.split(m[:, None, :], 6, axis=-1)


def rope(x, pos):
    half = x.shape[-1] // 2
    inv = ROPE_THETA ** (-jnp.arange(half, dtype=jnp.float32) / half)
    ang = pos.astype(jnp.float32)[:, None] * inv[None, :]
    cos = jnp.cos(ang)[None, :, None, :]
    sin = jnp.sin(ang)[None, :, None, :]
    x1, x2 = x[..., :half], x[..., half:]
    return jnp.concatenate([x1 * cos - x2 * sin, x2 * cos + x1 * sin], axis=-1)


def rwkv7_mixer(pa, prev_row, S0, mu, w0, w_dec_up, a0, w_aaa_up, w_gate_up, k_k, k_a, r_k, lnx_w, lnx_b):
    B, T, _ = pa.shape
    shifted = jnp.concatenate([prev_row[:, None, :], pa[:, :-1]], axis=1)
    m = pa + (shifted - pa) * mu
    r, k, v, wd, ad, gd = jnp.split(m, RWKV_SPLIT, axis=-1)
    w = -jax.nn.softplus(-(w0 + mm3(jnp.tanh(wd), w_dec_up))) - 0.5
    a = jax.nn.sigmoid(a0 + mm3(ad, w_aaa_up))
    g = mm3(jax.nn.sigmoid(gd), w_gate_up)
    heads = lambda t: t.reshape(B, T, H_A, HD_A)
    kk = heads(k * k_k)
    kk = kk / jnp.maximum(jnp.sqrt(jnp.sum(kk * kk, axis=-1, keepdims=True)), 1e-12)
    k = k * (1 + (a - 1) * k_a)
    rh, kh, vh = heads(r), heads(k), heads(v)
    y, S = rwkv7_scan(S0, r, -jnp.exp(w), k, v, kk.reshape(B, T, C_A), kk.reshape(B, T, C_A) * a)
    y = heads(y)
    mean = jnp.mean(y, axis=-1, keepdims=True)
    var = jnp.mean(jnp.square(y - mean), axis=-1, keepdims=True)
    y = ((y - mean) * lax.rsqrt(var + GN_EPS_A)).reshape(B, T, C_A) * lnx_w + lnx_b
    bonus = (jnp.sum(rh * kh * r_k, axis=-1, keepdims=True) * vh).reshape(B, T, C_A)
    out = (y + bonus) * g
    return out, S, pa[:, -1]


def sink_softmax(s, sink):
    sk = jnp.broadcast_to(sink[:, :, None], s.shape[:-1])[..., None]
    return jax.nn.softmax(jnp.concatenate([s, sk], axis=-1), axis=-1)[..., :-1]


def swa_prompt(q, k, v, sinks):
    B, T = q.shape[:2]
    nb = T // WINDOW
    qb = q.reshape(B, nb, WINDOW, KVH_B, G_B, HD_B)
    kb = k.reshape(B, nb, WINDOW, KVH_B, HD_B)
    vb = v.reshape(B, nb, WINDOW, KVH_B, HD_B)
    pad = ((0, 0), (1, 0), (0, 0), (0, 0), (0, 0))
    kc = jnp.concatenate([jnp.pad(kb, pad)[:, :-1], kb], axis=2)
    vc = jnp.concatenate([jnp.pad(vb, pad)[:, :-1], vb], axis=2)
    s = jnp.einsum('bnqhgd,bnkhd->bnhgqk', qb, kc) * (HD_B ** -0.5)
    qi = jnp.arange(WINDOW)[:, None]
    ki = jnp.arange(2 * WINDOW)[None, :] - WINDOW
    band = (ki <= qi) & (qi - ki < WINDOW)
    valid = band[None] & ((jnp.arange(nb)[:, None, None] * WINDOW + ki[None]) >= 0)
    s = jnp.where(valid[None, :, None, None], s, -jnp.inf)
    p = sink_softmax(s, sinks.reshape(KVH_B, G_B))
    o = jnp.einsum('bnhgqk,bnkhd->bnqhgd', p, vc)
    return o.reshape(B, T, C_B), k[:, -WINDOW:], v[:, -WINDOW:]


def swa_sample(q, k, v, kbuf, vbuf, sinks):
    B, T = q.shape[:2]
    kc = jnp.concatenate([kbuf, k], axis=1)
    vc = jnp.concatenate([vbuf, v], axis=1)
    kpos = jnp.arange(WINDOW + T) - WINDOW
    qpos = jnp.arange(T)
    mask = (kpos[None, :] <= qpos[:, None]) & (qpos[:, None] - kpos[None, :] < WINDOW)
    s = jnp.einsum('bqhgd,bkhd->bhgqk', q.reshape(B, T, KVH_B, G_B, HD_B), kc) * (HD_B ** -0.5)
    s = jnp.where(mask, s, -jnp.inf)
    p = sink_softmax(s, sinks.reshape(KVH_B, G_B))
    o = jnp.einsum('bhgqk,bkhd->bqhgd', p, vc)
    return o.reshape(B, T, C_B), kc[:, -WINDOW:], vc[:, -WINDOW:]


def ab_mixer(h, pos, prev_row, S0, kbuf, vbuf, w_in, w_out, rwkv_params, sinks):
    B, T, _ = h.shape
    h2 = h.reshape(B * T, -1)
    w_parts = (w_in[:, :3 * C_A], w_in[:, 3 * C_A:P_A], w_in[:, P_A:P_A + C_B], w_in[:, P_A + C_B:])
    pa = jnp.concatenate([matmul(h2, w_parts[0]), matmul(h2, w_parts[1])], axis=-1).reshape(B, T, P_A)
    q = matmul(h2, w_parts[2]).reshape(B, T, C_B)
    kv = matmul(h2, w_parts[3]).reshape(B, T, 2 * KV_B)
    k, v = kv[..., :KV_B], kv[..., KV_B:]
    o_a, S, last_row = rwkv7_mixer(pa, prev_row, S0, *rwkv_params)
    q = rope(q.reshape(B, T, H_B, HD_B), pos)
    k = rope(k.reshape(B, T, KVH_B, HD_B), pos)
    v = v.reshape(B, T, KVH_B, HD_B)
    if kbuf is None:
        o_b, kw, vw = swa_prompt(q, k, v, sinks)
    else:
        o_b, kw, vw = swa_sample(q, k, v, kbuf, vbuf, sinks)
    out = mm3(jnp.concatenate([o_a, o_b], axis=-1), w_out)
    return out, S, last_row, kw, vw


def fox_project(h, w_in, b_f):
    B, T, _ = h.shape
    h2 = h.reshape(B * T, -1)
    q = matmul(h2, w_in[:, :C_C])
    kv = matmul(h2, w_in[:, C_C:C_C + 2 * KV_C])
    fl = matmul(h2, w_in[:, C_C + 2 * KV_C:])
    logf = jax.nn.log_sigmoid(fl + b_f).reshape(B, T, H_C)
    return (q.reshape(B, T, H_C, HD_C), kv[:, :KV_C].reshape(B, T, KVH_C, HD_C),
            kv[:, KV_C:].reshape(B, T, KVH_C, HD_C), logf)


NEG_BIG = -1e30
FOX_PAGES_PER_STEP = 8
FOX_Q_TILE = 256


def _dot3(parts, w):
    f = lambda x: jnp.dot(x, w, preferred_element_type=jnp.float32)
    return f(parts[0]) + (f(parts[1]) + f(parts[2]))


def _dot3_left(w, parts):
    f = lambda x: jnp.dot(w, x, preferred_element_type=jnp.float32)
    return f(parts[0]) + (f(parts[1]) + f(parts[2]))


def _upper_tri(n):
    r = lax.broadcasted_iota(jnp.int32, (n, n), 0)
    c = lax.broadcasted_iota(jnp.int32, (n, n), 1)
    return jnp.where(r <= c, 1.0, 0.0).astype(jnp.bfloat16)


def _online_softmax_step(s, V, m_ref, l_ref, acc_ref, rows):
    m_old = m_ref[rows]
    m_new = jnp.maximum(m_old, jnp.max(s, axis=1, keepdims=True))
    p = jnp.exp(s - jnp.concatenate([m_new] * (s.shape[1] // HD_C), axis=1))
    alpha = jnp.exp(m_old - m_new)
    l_ref[rows] = alpha * l_ref[rows] + jnp.sum(p, axis=1, keepdims=True)
    acc_ref[rows] = alpha * acc_ref[rows] + jnp.dot(_bf(p), V, preferred_element_type=jnp.float32)
    m_ref[rows] = m_new


def _fox_sample_body(pt_ref, q_ref, *rest, G, T_new):
    k_refs = rest[:G]
    v_refs = rest[G:2 * G]
    lf_refs = rest[2 * G:3 * G]
    knew_ref, vnew_ref, lfnew_ref, o_ref, m_ref, l_ref, acc_ref, carry_ref = rest[3 * G:]
    g = pl.program_id(1)
    n_groups = pl.num_programs(1) - 1
    R = G_C * T_new
    RQ = KVH_C * R

    @pl.when(g == 0)
    def _():
        m_ref[...] = jnp.full_like(m_ref, NEG_BIG)
        l_ref[...] = jnp.zeros_like(l_ref)
        acc_ref[...] = jnp.zeros_like(acc_ref)
        carry_ref[...] = jnp.zeros_like(carry_ref)

    q = _bf(q_ref[...] * (HD_C ** -0.5))
    tri = _upper_tri(PAGE_SIZE)
    er = lax.broadcasted_iota(jnp.int32, (RQ, H_C), 0)
    ec = lax.broadcasted_iota(jnp.int32, (RQ, H_C), 1)
    expand = jnp.where(er // T_new == ec, 1.0, 0.0).astype(jnp.bfloat16)

    def process(k_list, v_list, lf_list, mask):
        carry = carry_ref[...]
        Fs = []
        for lf_r in lf_list:
            Fp = _dot3(_split3(lf_r[...]), tri) + carry
            carry = Fp[:, PAGE_SIZE - 1:PAGE_SIZE]
            Fs.append(Fp)
        carry_ref[...] = carry
        F_all = Fs[0] if len(Fs) == 1 else jnp.concatenate(Fs, axis=1)
        bias = _dot3_left(expand, _split3(F_all))
        def head_rows(refs, kv):
            parts = [r[pl.ds(kv, PAGE_SIZE, stride=KVH_C), :] for r in refs]
            return _bf(parts[0] if len(parts) == 1 else jnp.concatenate(parts, axis=0))

        s = jnp.concatenate(
            [lax.dot_general(q[kv * R:(kv + 1) * R], head_rows(k_list, kv), (((1,), (1,)), ((), ())),
                             preferred_element_type=jnp.float32) for kv in range(KVH_C)], axis=0) - bias
        if mask is not None:
            s = jnp.where(mask, s, NEG_BIG)
        m_old = m_ref[...]
        m_new = jnp.maximum(m_old, jnp.max(s, axis=1, keepdims=True))
        p = jnp.exp(s - jnp.concatenate([m_new] * (s.shape[1] // HD_C), axis=1))
        alpha = jnp.exp(m_old - m_new)
        pb = _bf(p)
        pv = jnp.concatenate(
            [jnp.dot(pb[kv * R:(kv + 1) * R], head_rows(v_list, kv), preferred_element_type=jnp.float32)
             for kv in range(KVH_C)], axis=0)
        l_ref[...] = alpha * l_ref[...] + jnp.sum(p, axis=1, keepdims=True)
        acc_ref[...] = alpha * acc_ref[...] + pv
        m_ref[...] = m_new

    @pl.when(g < n_groups)
    def _():
        process(k_refs, v_refs, lf_refs, None)

    @pl.when(g == n_groups)
    def _():
        tq = lax.broadcasted_iota(jnp.int32, (RQ, PAGE_SIZE), 0) % T_new
        key = lax.broadcasted_iota(jnp.int32, (RQ, PAGE_SIZE), 1)
        process([knew_ref], [vnew_ref], [lfnew_ref], key <= tq)
        o_ref[...] = acc_ref[...] / l_ref[...]


def fox_sample(q, k, v, logf, cache_k, cache_v, cache_lf, layer, page_table):
    B, T = q.shape[:2]
    NP = page_table.shape[1]
    G = min(FOX_PAGES_PER_STEP, NP)
    n_groups = NP // G
    n_pool = cache_k.shape[1]
    rows = PAGE_SIZE * KVH_C
    ck = cache_k.reshape(cache_k.shape[0], n_pool, rows, HD_C)
    cv = cache_v.reshape(cache_v.shape[0], n_pool, rows, HD_C)
    clf = jnp.swapaxes(cache_lf, -1, -2)
    qr = jnp.swapaxes(q, 1, 2).reshape(B, H_C * T, HD_C)
    padt = lambda x: jnp.pad(x, ((0, 0), (0, PAGE_SIZE - T), (0, 0), (0, 0))).reshape(B, rows, HD_C)
    knew, vnew = padt(k), padt(v)
    lfnew = jnp.pad(jnp.swapaxes(logf, 1, 2), ((0, 0), (0, 0), (0, PAGE_SIZE - T)))

    def page_spec(i, shape):
        return pl.BlockSpec((None, None) + shape,
                            lambda b, g, pt: (layer, pt[b, jnp.minimum(g, n_groups - 1) * G + i], 0, 0))

    per_b = lambda shape: pl.BlockSpec((None,) + shape, lambda b, g, pt: (b, 0, 0))
    in_specs = ([per_b((H_C * T, HD_C))]
                + [page_spec(i, (rows, HD_C)) for i in range(G)]
                + [page_spec(i, (rows, HD_C)) for i in range(G)]
                + [page_spec(i, (H_C, PAGE_SIZE)) for i in range(G)]
                + [per_b((rows, HD_C)), per_b((rows, HD_C)), per_b((H_C, PAGE_SIZE))])
    RQ = H_C * T
    o = pl.pallas_call(
        functools.partial(_fox_sample_body, G=G, T_new=T),
        grid_spec=pltpu.PrefetchScalarGridSpec(
            num_scalar_prefetch=1,
            grid=(B, n_groups + 1),
            in_specs=in_specs,
            out_specs=per_b((RQ, HD_C)),
            scratch_shapes=[pltpu.VMEM((RQ, HD_C), jnp.float32), pltpu.VMEM((RQ, HD_C), jnp.float32),
                            pltpu.VMEM((RQ, HD_C), jnp.float32), pltpu.VMEM((H_C, 1), jnp.float32)]),
        out_shape=jax.ShapeDtypeStruct((B, RQ, HD_C), jnp.float32),
        compiler_params=pltpu.CompilerParams(dimension_semantics=("parallel", "arbitrary"),
                                             vmem_limit_bytes=VMEM_LIMIT_BYTES),
        name="fox_sample_paged_attention",
    )(page_table, qr, *([ck] * G), *([cv] * G), *([clf] * G), knew, vnew, lfnew)
    return jnp.swapaxes(o.reshape(B, H_C, T, HD_C), 1, 2).reshape(B, T, H_C * HD_C)


def _fox_cumsum_body(lf_ref, f_ref, *, blk):
    T = lf_ref.shape[-1]
    tri = _upper_tri(blk)
    carry = jnp.zeros((lf_ref.shape[0], 1), jnp.float32)
    for j in range(T // blk):
        sl = slice(j * blk, (j + 1) * blk)
        Fp = _dot3(_split3(lf_ref[:, sl]), tri) + carry
        f_ref[:, sl] = Fp
        carry = Fp[:, blk - 1:blk]


def _fox_prompt_body(q_ref, k_ref, v_ref, f_ref, o_ref, qs_ref, m_ref, l_ref, acc_ref, *, tq):
    qi = pl.program_id(1)
    kv = pl.program_id(2)
    for g in range(G_C):
        qs_ref[g * tq:(g + 1) * tq, :] = _bf(q_ref[pl.ds(kv * G_C + g, tq, stride=H_C), :] * (HD_C ** -0.5))
    m_ref[...] = jnp.full_like(m_ref, NEG_BIG)
    l_ref[...] = jnp.zeros_like(l_ref)
    acc_ref[...] = jnp.zeros_like(acc_ref)
    qpos = qi * tq + lax.broadcasted_iota(jnp.int32, (G_C * tq, tq), 0) % tq
    kcol = lax.broadcasted_iota(jnp.int32, (G_C * tq, tq), 1)

    def body(j, carry):
        ks = pl.multiple_of(j * tq, tq)
        K = _bf(k_ref[pl.ds(ks * KVH_C + kv, tq, stride=KVH_C), :])
        V = _bf(v_ref[pl.ds(ks * KVH_C + kv, tq, stride=KVH_C), :])
        s = lax.dot_general(qs_ref[...], K, (((1,), (1,)), ((), ())), preferred_element_type=jnp.float32)
        F = f_ref[:, pl.ds(ks, tq)]
        bias = jnp.concatenate([jnp.broadcast_to(F[g:g + 1, :], (tq, tq)) for g in range(G_C)], axis=0)
        s = jnp.where(kcol + ks <= qpos, s - bias, NEG_BIG)
        _online_softmax_step(s, V, m_ref, l_ref, acc_ref, slice(None))
        return carry

    lax.fori_loop(0, qi + 1, body, 0)
    o = acc_ref[...] / l_ref[...]
    for g in range(G_C):
        o_ref[pl.ds(kv * G_C + g, tq, stride=H_C), :] = o[g * tq:(g + 1) * tq, :]


def fox_prompt(q, k, v, logf):
    B, T = q.shape[:2]
    tq = min(FOX_Q_TILE, T)
    lfT = jnp.swapaxes(logf, 1, 2)
    F = pl.pallas_call(
        functools.partial(_fox_cumsum_body, blk=tq),
        grid=(B,),
        in_specs=[pl.BlockSpec((None, H_C, T), lambda b: (b, 0, 0))],
        out_specs=pl.BlockSpec((None, H_C, T), lambda b: (b, 0, 0)),
        out_shape=jax.ShapeDtypeStruct((B, H_C, T), jnp.float32),
        name="fox_logf_cumsum",
    )(lfT)
    o = pl.pallas_call(
        functools.partial(_fox_prompt_body, tq=tq),
        grid=(B, T // tq, KVH_C),
        in_specs=[pl.BlockSpec((None, tq * H_C, HD_C), lambda b, qi, kv: (b, qi, 0)),
                  pl.BlockSpec((None, T * KVH_C, HD_C), lambda b, qi, kv: (b, 0, 0)),
                  pl.BlockSpec((None, T * KVH_C, HD_C), lambda b, qi, kv: (b, 0, 0)),
                  pl.BlockSpec((None, None, G_C, T), lambda b, qi, kv: (b, kv, 0, 0))],
        out_specs=pl.BlockSpec((None, tq * H_C, HD_C), lambda b, qi, kv: (b, qi, 0)),
        out_shape=jax.ShapeDtypeStruct((B, T * H_C, HD_C), jnp.float32),
        scratch_shapes=[pltpu.VMEM((G_C * tq, HD_C), jnp.bfloat16),
                        pltpu.VMEM((G_C * tq, HD_C), jnp.float32), pltpu.VMEM((G_C * tq, HD_C), jnp.float32),
                        pltpu.VMEM((G_C * tq, HD_C), jnp.float32)],
        compiler_params=pltpu.CompilerParams(dimension_semantics=("parallel", "parallel", "arbitrary"),
                                             vmem_limit_bytes=VMEM_LIMIT_LARGE),
        name="fox_prompt_flash_attention",
    )(q.reshape(B, T * H_C, HD_C), k.reshape(B, T * KVH_C, HD_C), v.reshape(B, T * KVH_C, HD_C),
      F.reshape(B, KVH_C, G_C, T))
    return o.reshape(B, T, H_C * HD_C)


MOE_ROWS = 256
MOE_COMBINE_TOKENS = 64


def _moe_up_body(blk_exp_ref, nused_ref, row_tok_ref, roww_ref, x_hbm, wg_ref, wu_ref, act_ref,
                 xbuf, wg_bf, wu_bf, sem, *, RB):
    b = pl.program_id(0)
    nused = nused_ref[0]
    slot = b % 2

    def gather(blk, slot_):
        def body(i, carry):
            t = row_tok_ref[blk * RB + i]
            pltpu.make_async_copy(x_hbm.at[pl.ds(t, 1)], xbuf.at[slot_, pl.ds(i, 1)], sem.at[slot_]).start()
            return carry
        lax.fori_loop(0, RB, body, 0, unroll=8)

    @pl.when(b == 0)
    def _():
        gather(0, 0)

    @pl.when(b + 1 < nused)
    def _():
        gather(b + 1, 1 - slot)

    @pl.when(b < nused)
    def _():
        pltpu.make_async_copy(x_hbm.at[pl.ds(0, RB)], xbuf.at[slot], sem.at[slot]).wait()
        e = blk_exp_ref[b]
        e_prev = blk_exp_ref[jnp.maximum(b - 1, 0)]

        @pl.when((b == 0) | (e != e_prev))
        def _():
            wg_bf[...] = wg_ref[...].astype(jnp.bfloat16)
            wu_bf[...] = wu_ref[...].astype(jnp.bfloat16)

        x = xbuf[slot].astype(jnp.bfloat16)
        g = jnp.dot(x, wg_bf[...], preferred_element_type=jnp.float32)
        u = jnp.dot(x, wu_bf[...], preferred_element_type=jnp.float32)
        act_ref[...] = (g * jax.nn.sigmoid(g)) * u * roww_ref[...]

    @pl.when(b >= nused)
    def _():
        act_ref[...] = jnp.zeros_like(act_ref)


def _moe_down_body(blk_exp_ref, nused_ref, act_ref, wd_ref, y_ref, wd_bf):
    b = pl.program_id(0)
    nused = nused_ref[0]

    @pl.when(b < nused)
    def _():
        e = blk_exp_ref[b]
        e_prev = blk_exp_ref[jnp.maximum(b - 1, 0)]

        @pl.when((b == 0) | (e != e_prev))
        def _():
            wd_bf[...] = wd_ref[...].astype(jnp.bfloat16)

        y_ref[...] = jnp.dot(act_ref[...].astype(jnp.bfloat16), wd_bf[...], preferred_element_type=jnp.float32)

    @pl.when(b >= nused)
    def _():
        y_ref[...] = jnp.zeros_like(y_ref)


def _moe_combine_body(pos_ref, y_hbm, shared_ref, out_ref, buf, sem, *, TB):
    b = pl.program_id(0)
    nb = pl.num_programs(0)
    slot = b % 2

    def gather(blk, slot_):
        def body(i, carry):
            for kk in range(TOP_K):
                p = pos_ref[(blk * TB + i) * TOP_K + kk]
                pltpu.make_async_copy(y_hbm.at[pl.ds(p, 1)], buf.at[slot_, pl.ds(kk * TB + i, 1)],
                                      sem.at[slot_]).start()
            return carry
        lax.fori_loop(0, TB, body, 0, unroll=2)

    @pl.when(b == 0)
    def _():
        gather(0, 0)

    @pl.when(b + 1 < nb)
    def _():
        gather(b + 1, 1 - slot)

    pltpu.make_async_copy(y_hbm.at[pl.ds(0, TOP_K * TB)], buf.at[slot], sem.at[slot]).wait()
    acc = shared_ref[...]
    for kk in range(TOP_K):
        acc = acc + buf[slot, kk * TB:(kk + 1) * TB]
    out_ref[...] = acc


def _moe_invert_body(pos_ref, src_ref):
    def clear(i, carry):
        src_ref[i] = -1
        return carry
    lax.fori_loop(0, src_ref.shape[0], clear, 0, unroll=8)

    def put(i, carry):
        src_ref[pos_ref[i]] = i
        return carry
    lax.fori_loop(0, pos_ref.shape[0], put, 0, unroll=8)


ROUTE_TOKENS = 256
NEG_INF = float("-inf")


def _first_max(x, idx, axis, n):
    m = jnp.max(x, axis=axis, keepdims=True)
    first = jnp.min(jnp.where(x == m, idx, n), axis=axis, keepdims=True)
    return m, first


def _moe_router_body(wt_ref, h_ref, bias_ref, eidx_ref, gate_ref):
    tn = h_ref.shape[0]
    per_group = N_EXPERTS // N_GROUPS
    logits = lax.dot_general(_bf(wt_ref[...]), _bf(h_ref[...]), (((1,), (1,)), ((), ())),
                             preferred_element_type=jnp.float32)
    scores = jax.nn.sigmoid(logits)
    biased = scores + bias_ref[:, :1]
    b3 = biased.reshape(N_GROUPS, per_group, tn)
    i3 = lax.broadcasted_iota(jnp.int32, b3.shape, 1)
    m1, f1 = _first_max(b3, i3, 1, per_group)
    m2 = jnp.max(jnp.where(i3 == f1, NEG_INF, b3), axis=1, keepdims=True)
    grp = (m1 + m2).reshape(N_GROUPS, tn)
    gi = lax.broadcasted_iota(jnp.int32, grp.shape, 0)
    keep = jnp.zeros(grp.shape, jnp.float32)
    for _ in range(TOPK_GROUPS):
        _, f = _first_max(grp, gi, 0, N_GROUPS)
        keep = jnp.where(gi == f, 1.0, keep)
        grp = jnp.where(gi == f, NEG_INF, grp)
    cand = jnp.where(keep.reshape(N_GROUPS, 1, tn) > 0.5, b3, NEG_INF).reshape(N_EXPERTS, tn)
    ei = lax.broadcasted_iota(jnp.int32, cand.shape, 0)
    ids, gs = [], []
    for _ in range(TOP_K):
        _, f = _first_max(cand, ei, 0, N_EXPERTS)
        ids.append(f)
        gs.append(jnp.sum(jnp.where(ei == f, scores, 0.0), axis=0, keepdims=True))
        cand = jnp.where(ei == f, NEG_INF, cand)
    tot = gs[0]
    for x in gs[1:]:
        tot = tot + x
    pad = eidx_ref.shape[0] - TOP_K
    eidx_ref[...] = jnp.concatenate(ids + [jnp.zeros((pad, tn), jnp.int32)], axis=0)
    gate_ref[...] = jnp.concatenate([x / tot * ROUTED_SCALE for x in gs] + [jnp.zeros((pad, tn), jnp.float32)],
                                    axis=0)


def moe_route(h, w_router, b_router):
    T, D = h.shape
    tn = ROUTE_TOKENS
    rows = 8
    bias = jnp.broadcast_to(b_router[:, None], (N_EXPERTS, 128))
    eidx, gates = pl.pallas_call(
        _moe_router_body,
        grid=(T // tn,),
        in_specs=[pl.BlockSpec((N_EXPERTS, D), lambda i: (0, 0)),
                  pl.BlockSpec((tn, D), lambda i: (i, 0)),
                  pl.BlockSpec((N_EXPERTS, 128), lambda i: (0, 0))],
        out_specs=[pl.BlockSpec((rows, tn), lambda i: (0, i)), pl.BlockSpec((rows, tn), lambda i: (0, i))],
        out_shape=[jax.ShapeDtypeStruct((rows, T), jnp.int32), jax.ShapeDtypeStruct((rows, T), jnp.float32)],
        compiler_params=pltpu.CompilerParams(dimension_semantics=("parallel",)),
        name="moe_router_topk",
    )(w_router.T, h, bias)
    return eidx[:TOP_K].T, gates[:TOP_K].T


def moe_layout(eidx, gates, RB):
    T = eidx.shape[0]
    A = T * TOP_K
    E = N_EXPERTS
    e_flat = eidx.reshape(A).astype(jnp.int32)
    g_flat = gates.reshape(A)
    onehot = (e_flat[:, None] == jnp.arange(E, dtype=jnp.int32)[None, :]).astype(jnp.int32)
    rank_incl = jnp.cumsum(onehot, axis=0)
    rank = jnp.sum(onehot * rank_incl, axis=1) - 1
    counts = rank_incl[-1]
    padded = (counts + RB - 1) // RB * RB
    seg_start = jnp.cumsum(counts) - counts
    pad_end = jnp.cumsum(padded)
    pad_start = pad_end - padded
    pos = pad_start[e_flat] + rank
    n_blocks = -(-(A + E * (RB - 1)) // RB)
    n_rows = n_blocks * RB
    blk_exp = jnp.minimum(jnp.searchsorted(pad_end, jnp.arange(n_blocks, dtype=jnp.int32) * RB, side='right'),
                          E - 1).astype(jnp.int32)
    nused = (pad_end[-1] // RB).astype(jnp.int32).reshape(1)
    pos = pos.astype(jnp.int32)
    src = pl.pallas_call(
        _moe_invert_body,
        grid_spec=pltpu.PrefetchScalarGridSpec(
            num_scalar_prefetch=1, grid=(1,), in_specs=[],
            out_specs=pl.BlockSpec(memory_space=pltpu.SMEM)),
        out_shape=jax.ShapeDtypeStruct((n_rows,), jnp.int32),
        name="moe_invert_positions",
    )(pos)
    valid = src >= 0
    row_tok = jnp.where(valid, src // TOP_K, 0).astype(jnp.int32)
    row_w = jnp.where(valid, g_flat[jnp.maximum(src, 0)], 0.0).astype(jnp.float32)
    return pos, row_tok, row_w, blk_exp, nused, n_blocks


def moe_routed_plus_shared(h, layer, eidx, gates, w_gate, w_up, w_down, shared):
    T, D = h.shape
    F = w_gate.shape[-1]
    RB = MOE_ROWS
    TB = MOE_COMBINE_TOKENS
    pos, row_tok, row_w, blk_exp, nused, n_blocks = moe_layout(eidx, gates, RB)
    n_rows = n_blocks * RB
    wspec_up = pl.BlockSpec((None, None, D, F), lambda b, be, nu, rt: (layer, be[b], 0, 0))
    act = pl.pallas_call(
        functools.partial(_moe_up_body, RB=RB),
        grid_spec=pltpu.PrefetchScalarGridSpec(
            num_scalar_prefetch=3,
            grid=(n_blocks,),
            in_specs=[pl.BlockSpec((RB, 1), lambda b, be, nu, rt: (b, 0)),
                      pl.BlockSpec(memory_space=pl.ANY),
                      wspec_up, wspec_up],
            out_specs=pl.BlockSpec((RB, F), lambda b, be, nu, rt: (b, 0)),
            scratch_shapes=[pltpu.VMEM((2, RB, D), jnp.float32),
                            pltpu.VMEM((D, F), jnp.bfloat16),
                            pltpu.VMEM((D, F), jnp.bfloat16),
                            pltpu.SemaphoreType.DMA((2,))]),
        out_shape=jax.ShapeDtypeStruct((n_rows, F), jnp.float32),
        compiler_params=pltpu.CompilerParams(dimension_semantics=("arbitrary",),
                                             vmem_limit_bytes=VMEM_LIMIT_LARGE),
        name="moe_gate_up",
    )(blk_exp, nused, row_tok, row_w.reshape(n_rows, 1), h, w_gate, w_up)
    y = pl.pallas_call(
        _moe_down_body,
        grid_spec=pltpu.PrefetchScalarGridSpec(
            num_scalar_prefetch=2,
            grid=(n_blocks,),
            in_specs=[pl.BlockSpec((RB, F), lambda b, be, nu: (b, 0)),
                      pl.BlockSpec((None, None, F, D), lambda b, be, nu: (layer, be[b], 0, 0))],
            out_specs=pl.BlockSpec((RB, D), lambda b, be, nu: (b, 0)),
            scratch_shapes=[pltpu.VMEM((F, D), jnp.bfloat16)]),
        out_shape=jax.ShapeDtypeStruct((n_rows, D), jnp.float32),
        compiler_params=pltpu.CompilerParams(dimension_semantics=("arbitrary",),
                                             vmem_limit_bytes=VMEM_LIMIT_LARGE),
        name="moe_down",
    )(blk_exp, nused, act, w_down)
    return pl.pallas_call(
        functools.partial(_moe_combine_body, TB=TB),
        grid_spec=pltpu.PrefetchScalarGridSpec(
            num_scalar_prefetch=1,
            grid=(T // TB,),
            in_specs=[pl.BlockSpec(memory_space=pl.ANY),
                      pl.BlockSpec((TB, D), lambda b, ps: (b, 0))],
            out_specs=pl.BlockSpec((TB, D), lambda b, ps: (b, 0)),
            scratch_shapes=[pltpu.VMEM((2, TOP_K * TB, D), jnp.float32),
                            pltpu.SemaphoreType.DMA((2,))]),
        out_shape=jax.ShapeDtypeStruct((T, D), jnp.float32),
        compiler_params=pltpu.CompilerParams(dimension_semantics=("arbitrary",),
                                             vmem_limit_bytes=VMEM_LIMIT_LARGE),
        name="moe_combine",
    )(pos, y, shared)


def moe_ffn(h, layer, w_router, b_router, w_gate, w_up, w_down, ws_gate, ws_up, ws_down):
    eidx, gates = moe_route(h, w_router[layer], b_router[layer])
    shared = matmul(jax.nn.silu(matmul(h, ws_gate[layer])) * matmul(h, ws_up[layer]), ws_down[layer])
    return moe_routed_plus_shared(h, layer, eidx, gates, w_gate, w_up, w_down, shared)


def kernel(x_prompt, x_sample, c_prompt, c_sample, state_rwkv, state_rwkv_shift, cache_swa_k, cache_swa_v, cache_fox_k, cache_fox_v, cache_fox_logf, page_table, ada_w, ada_b, norm_w, final_norm_w, ab_w_in, ab_w_out, rwkv_mu, rwkv_w0, rwkv_w_decay_up, rwkv_a0, rwkv_w_aaa_up, rwkv_w_gate_up, rwkv_k_k, rwkv_k_a, rwkv_r_k, rwkv_lnx_w, rwkv_lnx_b, swa_sinks, fox_w_in, fox_b_f, fox_w_out, moe_w_router, moe_b_router, moe_w_gate, moe_w_up, moe_w_down, shared_w_gate, shared_w_up, shared_w_down):
    Bp, Tp, D = x_prompt.shape
    Bs, Ts, _ = x_sample.shape
    depth = ada_w.shape[0]
    past_len = page_table.shape[1] * PAGE_SIZE
    pos_p = jnp.arange(Tp)
    pos_s = past_len + jnp.arange(Ts)
    xp, xs = x_prompt, x_sample
    rw_S_p, rw_sh_p, sw_k_p, sw_v_p, fk_p, fv_p, flf_p = [], [], [], [], [], [], []
    rw_S_s, rw_sh_s, sw_k_s, sw_v_s, fk_s, fv_s, flf_s = [], [], [], [], [], [], []
    for l in range(depth):
        mods = ada_params(jnp.concatenate([c_prompt, c_sample], axis=0), ada_w, l, ada_b[l])
        sh1_p, sc1_p, g1_p, sh2_p, sc2_p, g2_p = [m[:Bp] for m in mods]
        sh1_s, sc1_s, g1_s, sh2_s, sc2_s, g2_s = [m[Bp:] for m in mods]
        hp = rmsnorm(xp, norm_w[l, 0], sc1_p, sh1_p, jnp.bfloat16)
        hs = rmsnorm(xs, norm_w[l, 0], sc1_s, sh1_s, jnp.bfloat16)
        if l % 2 == 0:
            i = l // 2
            rw = (rwkv_mu[i], rwkv_w0[i], rwkv_w_decay_up[i], rwkv_a0[i], rwkv_w_aaa_up[i], rwkv_w_gate_up[i],
                  rwkv_k_k[i], rwkv_k_a[i], rwkv_r_k[i], rwkv_lnx_w[i], rwkv_lnx_b[i])
            o_p, S_p, row_p, kw_p, vw_p = ab_mixer(
                hp, pos_p, jnp.zeros((Bp, P_A), jnp.float32), jnp.zeros((Bp, H_A, HD_A, HD_A), jnp.float32),
                None, None, ab_w_in[i], ab_w_out[i], rw, swa_sinks[i])
            o_s, S_s, row_s, kw_s, vw_s = ab_mixer(
                hs, pos_s, state_rwkv_shift[i], state_rwkv[i], cache_swa_k[i], cache_swa_v[i],
                ab_w_in[i], ab_w_out[i], rw, swa_sinks[i])
            rw_S_p.append(S_p); rw_sh_p.append(row_p); sw_k_p.append(kw_p); sw_v_p.append(vw_p)
            rw_S_s.append(S_s); rw_sh_s.append(row_s); sw_k_s.append(kw_s); sw_v_s.append(vw_s)
        else:
            j = l // 2
            q_p, k_p, v_p, lf_p = fox_project(hp, fox_w_in[j], fox_b_f[j])
            q_s, k_s, v_s, lf_s = fox_project(hs, fox_w_in[j], fox_b_f[j])
            o_p = mm3(fox_prompt(q_p, k_p, v_p, lf_p), fox_w_out[j])
            o_s = mm3(fox_sample(q_s, k_s, v_s, lf_s, cache_fox_k, cache_fox_v, cache_fox_logf, j, page_table),
                      fox_w_out[j])
            fk_p.append(k_p); fv_p.append(v_p); flf_p.append(lf_p)
            fk_s.append(k_s); fv_s.append(v_s); flf_s.append(lf_s)
        xp = xp + g1_p * o_p
        xs = xs + g1_s * o_s
        h2p = rmsnorm(xp, norm_w[l, 1], sc2_p, sh2_p)
        h2s = rmsnorm(xs, norm_w[l, 1], sc2_s, sh2_s)
        tok = jnp.concatenate([h2p.reshape(Bp * Tp, D), h2s.reshape(Bs * Ts, D)], axis=0)
        y = moe_ffn(tok, l, moe_w_router, moe_b_router, moe_w_gate, moe_w_up, moe_w_down,
                    shared_w_gate, shared_w_up, shared_w_down)
        xp = xp + g2_p * y[:Bp * Tp].reshape(Bp, Tp, D)
        xs = xs + g2_s * y[Bp * Tp:].reshape(Bs, Ts, D)
    y_prompt = rmsnorm(xp, final_norm_w)
    y_sample = rmsnorm(xs, final_norm_w)
    return (y_prompt, y_sample,
            jnp.stack(rw_S_p), jnp.stack(rw_sh_p), jnp.stack(sw_k_p), jnp.stack(sw_v_p),
            jnp.stack(fk_p), jnp.stack(fv_p), jnp.stack(flf_p),
            jnp.stack(rw_S_s), jnp.stack(rw_sh_s), jnp.stack(sw_k_s), jnp.stack(sw_v_s),
            jnp.stack(fk_s), jnp.stack(fv_s), jnp.stack(flf_s))
```

```python
import functools

import jax
import jax.numpy as jnp
from jax import lax
from jax.experimental import pallas as pl
from jax.experimental.pallas import tpu as pltpu

D_MODEL = 4096
PAGE_SIZE = 128

H_A = 32
HD_A = 64
C_A = H_A * HD_A
LORA_DECAY = 128
LORA_AAA = 128
LORA_GATE = 480
P_A = 3 * C_A + LORA_DECAY + LORA_AAA + LORA_GATE
RWKV_SPLIT = (C_A, 2 * C_A, 3 * C_A, 3 * C_A + LORA_DECAY, 3 * C_A + LORA_DECAY + LORA_AAA)
GN_EPS_A = 64e-5

H_B = 32
KVH_B = 4
G_B = H_B // KVH_B
HD_B = 64
C_B = H_B * HD_B
KV_B = KVH_B * HD_B
WINDOW = 128
ROPE_THETA = 10000.0

H_C = 32
KVH_C = 8
G_C = H_C // KVH_C
HD_C = 128
C_C = H_C * HD_C
KV_C = KVH_C * HD_C
Q_BLOCK = 128

N_EXPERTS = 64
TOP_K = 6
N_GROUPS = 8
TOPK_GROUPS = 4
ROUTED_SCALE = 2.5
MOE_BLOCK = 128

NORM_EPS = 1e-6

VMEM_LIMIT_BYTES = 48 * 1024 * 1024
VMEM_LIMIT_LARGE = 56 * 1024 * 1024


MM_VMEM_BUDGET = 40 * 1024 * 1024


def _mm_body(x_ref, w_ref, o_ref):
    o_ref[...] = jnp.dot(x_ref[...], w_ref[...].astype(jnp.bfloat16), preferred_element_type=jnp.float32)


def _pick_tile(n, candidates):
    for c in candidates:
        if n % c == 0:
            return c
    return n


def _mm_tiles(M, K, N):
    tn = _pick_tile(N, (512, 256, 128))
    for tm in (1024, 768, 512, 256, 128):
        if M % tm == 0 and 2 * (tm * K * 2 + K * tn * 4 + tm * tn * 4) <= MM_VMEM_BUDGET:
            return tm, tn
    return M, tn


def matmul(x, w, layer=None):
    M, K = x.shape
    N = w.shape[-1]
    if layer is None:
        w_spec = lambda tn: pl.BlockSpec((K, tn), lambda i, j: (0, j))
    else:
        w_spec = lambda tn: pl.BlockSpec((None, K, tn), lambda i, j: (layer, 0, j))
    tm, tn = _mm_tiles(M, K, N)
    return pl.pallas_call(
        _mm_body,
        grid=(M // tm, N // tn),
        in_specs=[pl.BlockSpec((tm, K), lambda i, j: (i, 0)), w_spec(tn)],
        out_specs=pl.BlockSpec((tm, tn), lambda i, j: (i, j)),
        out_shape=jax.ShapeDtypeStruct((M, N), jnp.float32),
        compiler_params=pltpu.CompilerParams(
            dimension_semantics=("parallel", "parallel"),
            vmem_limit_bytes=VMEM_LIMIT_BYTES),
        name="dense_matmul",
    )(x.astype(jnp.bfloat16), w)


def mm3(h, w):
    B, T, K = h.shape
    return matmul(h.reshape(B * T, K), w).reshape(B, T, w.shape[1])


HD = HD_A
PAIR = 2 * HD
RWKV_CHUNK = 64
RWKV_PAIRS_PER_STEP = 4


def _bf(x):
    return x.astype(jnp.bfloat16)


def _dot(a, b):
    return jnp.dot(_bf(a), _bf(b), preferred_element_type=jnp.float32)


def _dot_nt(a, b):
    return lax.dot_general(_bf(a), _bf(b), (((1,), (1,)), ((), ())), preferred_element_type=jnp.float32)


def _split3(x):
    h = x.astype(jnp.bfloat16)
    r1 = x - h.astype(jnp.float32)
    m = r1.astype(jnp.bfloat16)
    l = (r1 - m.astype(jnp.float32)).astype(jnp.bfloat16)
    return h, m, l


def _rwkv_body(s0_ref, r_ref, lw_ref, k_ref, v_ref, kk_ref, b_ref, y_ref, sT_ref, s_scr, *, C, hp):
    c = pl.program_id(2)

    @pl.when(c == 0)
    def _():
        s_scr[...] = s0_ref[...]

    n = 2 * C
    row = lax.broadcasted_iota(jnp.int32, (n, n), 0)
    col = lax.broadcasted_iota(jnp.int32, (n, n), 1)
    same_head = (row >= C) == (col >= C)
    tt = row & (C - 1)
    ss = col & (C - 1)
    strict = same_head & (ss < tt)
    incl = same_head & (ss <= tt)
    trow = lax.broadcasted_iota(jnp.int32, (C, C), 0)
    tcol = lax.broadcasted_iota(jnp.int32, (C, C), 1)
    tri = jnp.where(tcol <= trow, 1.0, 0.0).astype(jnp.bfloat16)
    lane = lax.broadcasted_iota(jnp.int32, (1, PAIR), 1)
    lo = jnp.where(lane < HD, 1.0, 0.0)
    hi = 1.0 - lo
    vrow = lax.broadcasted_iota(jnp.int32, (PAIR, PAIR), 0)
    vcol = lax.broadcasted_iota(jnp.int32, (PAIR, PAIR), 1)
    bd = (vrow >= HD) == (vcol >= HD)
    eye = jnp.where(row == col, 1.0, 0.0)
    nlev = C.bit_length() - 1

    P = range(hp)
    sls = [slice(p * PAIR, (p + 1) * PAIR) for p in P]
    lw = [lw_ref[:, s] for s in sls]
    r = [r_ref[:, s] for s in sls]
    k = [k_ref[:, s] for s in sls]
    v = [v_ref[:, s] for s in sls]
    kk = [kk_ref[:, s] for s in sls]
    b = [b_ref[:, s] for s in sls]
    csum = lambda x: jnp.dot(tri, x, preferred_element_type=jnp.float32)
    w3 = [_split3(x) for x in lw]
    Lc = [csum(h) + (csum(m) + csum(l)) for h, m, l in w3]
    e_inc = [jnp.exp(x) for x in Lc]
    e_neg = [jnp.exp(-x) for x in Lc]
    e_end = [jnp.exp(x[C - 1:C, :] - x) for x in Lc]
    Kq = [kk[p] * jnp.exp(Lc[p] - lw[p]) for p in P]
    Rq = [r[p] * e_inc[p] for p in P]
    Bd = [b[p] * e_neg[p] for p in P]
    Kd = [k[p] * e_neg[p] for p in P]
    G = [_dot_nt(jnp.concatenate([Kq[p] * lo, Kq[p] * hi, Rq[p] * lo, Rq[p] * hi], axis=0),
                 jnp.concatenate([Bd[p], Bd[p], Kd[p], Kd[p]], axis=0)) for p in P]
    Mb = [jnp.where(strict, g[:n, :n], 0.0) for g in G]
    Mk = [jnp.where(strict, g[:n, n:], 0.0) for g in G]
    Nb = [jnp.where(incl, g[n:, :n], 0.0) for g in G]
    Nk = [jnp.where(incl, g[n:, n:], 0.0) for g in G]
    T = [eye - jnp.where((tt >> 1) == (ss >> 1), m, 0.0) for m in Mb]
    for j in range(2, nlev + 1):
        lower_left = ((tt >> j) == (ss >> j)) & ((tt >> (j - 1)) > (ss >> (j - 1)))
        TM = [_dot(T[p], jnp.where(lower_left, Mb[p], 0.0)) for p in P]
        T = [T[p] - _dot(TM[p], T[p]) for p in P]
    V2 = [jnp.concatenate([x * lo, x * hi], axis=0) for x in v]
    MkV = [_dot(Mk[p], V2[p]) for p in P]
    NkV = [_dot(Nk[p], V2[p]) for p in P]
    BK = [jnp.concatenate([b[p] * e_end[p], k[p] * e_end[p]], axis=0) for p in P]
    S = [s_scr[p] for p in P]
    W0 = [_dot_nt(jnp.concatenate([Kq[p], Rq[p]], axis=0), S[p]) for p in P]
    rhs_u = [-(W0[p][:C] + (MkV[p][:C] + MkV[p][C:])) for p in P]
    UU = [_dot(T[p], jnp.concatenate([rhs_u[p] * lo, rhs_u[p] * hi], axis=0)) for p in P]
    YY = [_dot(Nb[p], UU[p]) + NkV[p] for p in P]
    UV = [jnp.concatenate([UU[p][:C] + UU[p][C:], v[p]], axis=0) for p in P]
    upd = [_dot(UV[p].T, BK[p]) for p in P]
    for p in P:
        y_ref[:, sls[p]] = YY[p][:C] + YY[p][C:] + W0[p][C:]
        S_new = S[p] * e_inc[p][C - 1:C, :] + jnp.where(bd, upd[p], 0.0)
        s_scr[p] = S_new
        sT_ref[p] = S_new


def rwkv7_scan(S0, r, lw, k, v, kk, b):
    B, T, CA = r.shape
    H = CA // HD
    npair = H // 2
    C = RWKV_CHUNK
    hp = min(RWKV_PAIRS_PER_STEP, npair)
    Tp = -(-T // C) * C
    if Tp != T:
        padf = lambda x: jnp.pad(x, ((0, 0), (0, Tp - T), (0, 0)))
        r, lw, k, v, kk, b = map(padf, (r, lw, k, v, kk, b))
    S0p = S0.reshape(B, npair, 2, HD, HD)
    z = jnp.zeros_like(S0p[:, :, 0])
    S0bd = jnp.concatenate([jnp.concatenate([S0p[:, :, 0], z], axis=-1),
                            jnp.concatenate([z, S0p[:, :, 1]], axis=-1)], axis=-2)
    seq_spec = pl.BlockSpec((None, C, hp * PAIR), lambda bb, g, c: (bb, c, g))
    st_spec = pl.BlockSpec((None, hp, PAIR, PAIR), lambda bb, g, c: (bb, g, 0, 0))
    y, Sbd = pl.pallas_call(
        functools.partial(_rwkv_body, C=C, hp=hp),
        grid=(B, npair // hp, Tp // C),
        in_specs=[st_spec] + [seq_spec] * 6,
        out_specs=[seq_spec, st_spec],
        out_shape=[jax.ShapeDtypeStruct((B, Tp, CA), jnp.float32),
                   jax.ShapeDtypeStruct((B, npair, PAIR, PAIR), jnp.float32)],
        scratch_shapes=[pltpu.VMEM((hp, PAIR, PAIR), jnp.float32)],
        compiler_params=pltpu.CompilerParams(
            dimension_semantics=("parallel", "parallel", "arbitrary")),
        name="rwkv7_chunk_scan",
    )(S0bd, r, lw, k, v, kk, b)
    S = jnp.stack([Sbd[:, :, :HD, :HD], Sbd[:, :, HD:, HD:]], axis=2).reshape(B, H, HD, HD)
    return y[:, :T], S


NORM_ROWS = 256


def _norm_body(x_ref, w_ref, *rest):
    o_ref = rest[-1]
    x = x_ref[...]
    y = x * lax.rsqrt(jnp.mean(x * x, axis=-1, keepdims=True) + NORM_EPS) * w_ref[...]
    if len(rest) == 3:
        y = y * (1 + rest[0][...]) + rest[1][...]
    o_ref[...] = y.astype(o_ref.dtype)


def rmsnorm(x, w, scale=None, shift=None, out_dtype=jnp.float32):
    B, T, D = x.shape
    tt = min(NORM_ROWS, T)
    row_spec = pl.BlockSpec((None, tt, D), lambda b, t: (b, t, 0))
    mod_spec = pl.BlockSpec((None, 1, D), lambda b, t: (b, 0, 0))
    mods = [] if scale is None else [scale, shift]
    return pl.pallas_call(
        _norm_body,
        grid=(B, T // tt),
        in_specs=[row_spec, pl.BlockSpec((1, D), lambda b, t: (0, 0))] + [mod_spec] * len(mods),
        out_specs=row_spec,
        out_shape=jax.ShapeDtypeStruct((B, T, D), out_dtype),
        compiler_params=pltpu.CompilerParams(dimension_semantics=("parallel", "parallel")),
        name="rmsnorm_modulate",
    )(x, w.reshape(1, D), *mods)


def ada_params(c, w, layer, b):
    m = matmul(jax.nn.silu(c), w, layer) + b
    return jnp.split(m[:, None, :], 6, axis=-1)


def rwkv7_mixer(pa, prev_row, S0, mu, w0, w_dec_up, a0, w_aaa_up, w_gate_up, k_k, k_a, r_k, lnx_w, lnx_b):
    B, T, _ = pa.shape
    shifted = jnp.concatenate([prev_row[:, None, :], pa[:, :-1]], axis=1)
    m = pa + (shifted - pa) * mu
    r, k, v, wd, ad, gd = jnp.split(m, RWKV_SPLIT, axis=-1)
    w = -jax.nn.softplus(-(w0 + mm3(jnp.tanh(wd), w_dec_up))) - 0.5
    a = jax.nn.sigmoid(a0 + mm3(ad, w_aaa_up))
    g = mm3(jax.nn.sigmoid(gd), w_gate_up)
    heads = lambda t: t.reshape(B, T, H_A, HD_A)
    kk = heads(k * k_k)
    kk = kk / jnp.maximum(jnp.sqrt(jnp.sum(kk * kk, axis=-1, keepdims=True)), 1e-12)
    k = k * (1 + (a - 1) * k_a)
    rh, kh, vh = heads(r), heads(k), heads(v)
    y, S = rwkv7_scan(S0, r, -jnp.exp(w), k, v, kk.reshape(B, T, C_A), kk.reshape(B, T, C_A) * a)
    y = heads(y)
    mean = jnp.mean(y, axis=-1, keepdims=True)
    var = jnp.mean(jnp.square(y - mean), axis=-1, keepdims=True)
    y = ((y - mean) * lax.rsqrt(var + GN_EPS_A)).reshape(B, T, C_A) * lnx_w + lnx_b
    bonus = (jnp.sum(rh * kh * r_k, axis=-1, keepdims=True) * vh).reshape(B, T, C_A)
    out = (y + bonus) * g
    return out, S, pa[:, -1]


LANES = 128


def _swa_body(sink_ref, q_ref, kv_ref, cos_ref, sin_ref, kp0_ref, vp0_ref, o_ref, krot_ref, kprev, vprev, *,
              has_cache):
    n = pl.program_id(1)
    W = WINDOW

    @pl.when(n == 0)
    def _():
        kprev[...] = kp0_ref[...]
        vprev[...] = vp0_ref[...]

    cos = cos_ref[...]
    sin = sin_ref[...]
    lane = lax.broadcasted_iota(jnp.int32, (W, LANES), 1)
    first_half = (lane % HD_B) < (HD_B // 2)
    lo = lane < HD_B

    def rope(x):
        rot = jnp.where(first_half, pltpu.roll(x, LANES - HD_B // 2, 1), pltpu.roll(x, HD_B // 2, 1))
        return x * cos + rot * sin

    kcur = jnp.concatenate([rope(kv_ref[:, t * LANES:(t + 1) * LANES]) for t in range(KV_B // LANES)], axis=1)
    vcur = kv_ref[:, KV_B:2 * KV_B]
    krot_ref[...] = kcur
    K2 = jnp.concatenate([kprev[...], kcur], axis=0)
    V2 = jnp.concatenate([vprev[...], vcur], axis=0)
    lo2 = lax.broadcasted_iota(jnp.int32, (2 * W, LANES), 1) < HD_B

    def both_halves(x2, j):
        tile = x2[:, (j // 2) * LANES:(j // 2 + 1) * LANES]
        rolled = pltpu.roll(tile, HD_B, 1)
        return jnp.where(lo2, tile, rolled) if j % 2 == 0 else jnp.where(lo2, rolled, tile)

    R = G_B * W
    t_row = lax.broadcasted_iota(jnp.int32, (R, 2 * W), 0) % W
    c_col = lax.broadcasted_iota(jnp.int32, (R, 2 * W), 1)
    valid = (c_col > t_row) & (c_col <= t_row + W)
    if not has_cache:
        valid = valid & ((c_col >= W) | (n > 0))
    for j in range(KVH_B):
        Kj = _bf(both_halves(K2, j))
        Vj = _bf(both_halves(V2, j))
        rows = []
        for t in range(G_B // 2):
            c0 = (j * (G_B // 2) + t) * LANES
            x = rope(q_ref[:, c0:c0 + LANES]) * (HD_B ** -0.5)
            rows += [jnp.where(lo, x, 0.0), jnp.where(lo, 0.0, x)]
        Q = _bf(jnp.concatenate(rows, axis=0))
        s = lax.dot_general(Q, Kj, (((1,), (1,)), ((), ())), preferred_element_type=jnp.float32)
        s = jnp.where(valid, s, NEG_BIG)
        sk = jnp.concatenate([jnp.full((W, LANES), sink_ref[j * G_B + i], jnp.float32) for i in range(G_B)], axis=0)
        m = jnp.maximum(jnp.max(s, axis=1, keepdims=True), sk)
        p = jnp.exp(s - jnp.concatenate([m, m], axis=1))
        den = jnp.sum(p, axis=1, keepdims=True) + jnp.exp(sk - m)
        o = jnp.dot(_bf(p), Vj, preferred_element_type=jnp.float32) / den
        for t in range(G_B // 2):
            c0 = (j * (G_B // 2) + t) * LANES
            o_ref[:, c0:c0 + LANES] = jnp.where(lo, o[(2 * t) * W:(2 * t + 1) * W], o[(2 * t + 1) * W:(2 * t + 2) * W])
    kprev[...] = kcur
    vprev[...] = vcur


def swa_attention(q, kv, pos, kbuf, vbuf, sinks):
    B, T, _ = q.shape
    W = WINDOW
    Tp = -(-T // W) * W
    if Tp != T:
        q = jnp.pad(q, ((0, 0), (0, Tp - T), (0, 0)))
        kv = jnp.pad(kv, ((0, 0), (0, Tp - T), (0, 0)))
        pos = jnp.pad(pos, (0, Tp - T))
    half = HD_B // 2
    inv = ROPE_THETA ** (-jnp.arange(half, dtype=jnp.float32) / half)
    ang = pos.astype(jnp.float32)[:, None] * inv[None, :]
    cos = jnp.tile(jnp.cos(ang), (1, LANES // half))
    sin = jnp.tile(jnp.concatenate([-jnp.sin(ang), jnp.sin(ang)], axis=1), (1, LANES // HD_B))
    has_cache = kbuf is not None
    if has_cache:
        kp0, vp0 = kbuf.reshape(B, W, KV_B), vbuf.reshape(B, W, KV_B)
    else:
        kp0 = vp0 = jnp.zeros((B, W, KV_B), jnp.float32)
    blk = lambda c: pl.BlockSpec((None, W, c), lambda b, n, s: (b, n, 0))
    tab = pl.BlockSpec((W, LANES), lambda b, n, s: (n, 0))
    per_b = pl.BlockSpec((None, W, KV_B), lambda b, n, s: (b, 0, 0))
    o, krot = pl.pallas_call(
        functools.partial(_swa_body, has_cache=has_cache),
        grid_spec=pltpu.PrefetchScalarGridSpec(
            num_scalar_prefetch=1,
            grid=(B, Tp // W),
            in_specs=[blk(C_B), blk(2 * KV_B), tab, tab, per_b, per_b],
            out_specs=[blk(C_B), blk(KV_B)],
            scratch_shapes=[pltpu.VMEM((W, KV_B), jnp.float32), pltpu.VMEM((W, KV_B), jnp.float32)]),
        out_shape=[jax.ShapeDtypeStruct((B, Tp, C_B), jnp.float32), jax.ShapeDtypeStruct((B, Tp, KV_B), jnp.float32)],
        compiler_params=pltpu.CompilerParams(dimension_semantics=("parallel", "arbitrary")),
        name="swa_sink_attention",
    )(sinks, q, kv, cos, sin, kp0, vp0)
    return o[:, :T], krot[:, :T]


def ab_mixer(h, pos, prev_row, S0, kbuf, vbuf, w_in, w_out, rwkv_params, sinks):
    B, T, _ = h.shape
    h2 = h.reshape(B * T, -1)
    w_parts = (w_in[:, :3 * C_A], w_in[:, 3 * C_A:P_A], w_in[:, P_A:P_A + C_B], w_in[:, P_A + C_B:])
    pa = jnp.concatenate([matmul(h2, w_parts[0]), matmul(h2, w_parts[1])], axis=-1).reshape(B, T, P_A)
    q = matmul(h2, w_parts[2]).reshape(B, T, C_B)
    kv = matmul(h2, w_parts[3]).reshape(B, T, 2 * KV_B)
    o_a, S, last_row = rwkv7_mixer(pa, prev_row, S0, *rwkv_params)
    o_b, k_rot = swa_attention(q, kv, pos, kbuf, vbuf, sinks)
    k = k_rot.reshape(B, T, KVH_B, HD_B)
    v = kv[..., KV_B:].reshape(B, T, KVH_B, HD_B)
    if kbuf is not None:
        k = jnp.concatenate([kbuf, k], axis=1)
        v = jnp.concatenate([vbuf, v], axis=1)
    out = mm3(jnp.concatenate([o_a, o_b], axis=-1), w_out)
    return out, S, last_row, k[:, -WINDOW:], v[:, -WINDOW:]


def fox_project(h, w_in, b_f):
    B, T, _ = h.shape
    h2 = h.reshape(B * T, -1)
    q = matmul(h2, w_in[:, :C_C])
    kv = matmul(h2, w_in[:, C_C:C_C + 2 * KV_C])
    fl = matmul(h2, w_in[:, C_C + 2 * KV_C:])
    logf = jax.nn.log_sigmoid(fl + b_f).reshape(B, T, H_C)
    return q.reshape(B, T, C_C), kv.reshape(B, T, 2 * KV_C), logf


NEG_BIG = -1e30
FOX_PAGES_PER_STEP = 8
FOX_Q_TILE = 256


def _dot3(parts, w):
    f = lambda x: jnp.dot(x, w, preferred_element_type=jnp.float32)
    return f(parts[0]) + (f(parts[1]) + f(parts[2]))


def _dot3_left(w, parts):
    f = lambda x: jnp.dot(w, x, preferred_element_type=jnp.float32)
    return f(parts[0]) + (f(parts[1]) + f(parts[2]))


def _upper_tri(n):
    r = lax.broadcasted_iota(jnp.int32, (n, n), 0)
    c = lax.broadcasted_iota(jnp.int32, (n, n), 1)
    return jnp.where(r <= c, 1.0, 0.0).astype(jnp.bfloat16)


def _online_softmax_step(s, V, m_ref, l_ref, acc_ref, rows):
    m_old = m_ref[rows]
    m_new = jnp.maximum(m_old, jnp.max(s, axis=1, keepdims=True))
    p = jnp.exp(s - jnp.concatenate([m_new] * (s.shape[1] // HD_C), axis=1))
    alpha = jnp.exp(m_old - m_new)
    l_ref[rows] = alpha * l_ref[rows] + jnp.sum(p, axis=1, keepdims=True)
    acc_ref[rows] = alpha * acc_ref[rows] + jnp.dot(_bf(p), V, preferred_element_type=jnp.float32)
    m_ref[rows] = m_new


def _fox_sample_body(pt_ref, q_ref, *rest, G, T_new):
    k_refs = rest[:G]
    v_refs = rest[G:2 * G]
    lf_refs = rest[2 * G:3 * G]
    knew_ref, vnew_ref, lfnew_ref, o_ref, m_ref, l_ref, acc_ref, carry_ref = rest[3 * G:]
    g = pl.program_id(1)
    n_groups = pl.num_programs(1) - 1
    R = G_C * T_new
    RQ = KVH_C * R

    @pl.when(g == 0)
    def _():
        m_ref[...] = jnp.full_like(m_ref, NEG_BIG)
        l_ref[...] = jnp.zeros_like(l_ref)
        acc_ref[...] = jnp.zeros_like(acc_ref)
        carry_ref[...] = jnp.zeros_like(carry_ref)

    q = _bf(q_ref[...] * (HD_C ** -0.5))
    tri = _upper_tri(PAGE_SIZE)
    er = lax.broadcasted_iota(jnp.int32, (RQ, H_C), 0)
    ec = lax.broadcasted_iota(jnp.int32, (RQ, H_C), 1)
    expand = jnp.where(er // T_new == ec, 1.0, 0.0).astype(jnp.bfloat16)

    def process(k_list, v_list, lf_list, mask):
        carry = carry_ref[...]
        Fs = []
        for lf_r in lf_list:
            Fp = _dot3(_split3(lf_r[...]), tri) + carry
            carry = Fp[:, PAGE_SIZE - 1:PAGE_SIZE]
            Fs.append(Fp)
        carry_ref[...] = carry
        F_all = Fs[0] if len(Fs) == 1 else jnp.concatenate(Fs, axis=1)
        bias = _dot3_left(expand, _split3(F_all))
        def head_rows(refs, kv):
            parts = [r[pl.ds(kv, PAGE_SIZE, stride=KVH_C), :] for r in refs]
            return _bf(parts[0] if len(parts) == 1 else jnp.concatenate(parts, axis=0))

        s = jnp.concatenate(
            [lax.dot_general(q[kv * R:(kv + 1) * R], head_rows(k_list, kv), (((1,), (1,)), ((), ())),
                             preferred_element_type=jnp.float32) for kv in range(KVH_C)], axis=0) - bias
        if mask is not None:
            s = jnp.where(mask, s, NEG_BIG)
        m_old = m_ref[...]
        m_new = jnp.maximum(m_old, jnp.max(s, axis=1, keepdims=True))
        p = jnp.exp(s - jnp.concatenate([m_new] * (s.shape[1] // HD_C), axis=1))
        alpha = jnp.exp(m_old - m_new)
        pb = _bf(p)
        pv = jnp.concatenate(
            [jnp.dot(pb[kv * R:(kv + 1) * R], head_rows(v_list, kv), preferred_element_type=jnp.float32)
             for kv in range(KVH_C)], axis=0)
        l_ref[...] = alpha * l_ref[...] + jnp.sum(p, axis=1, keepdims=True)
        acc_ref[...] = alpha * acc_ref[...] + pv
        m_ref[...] = m_new

    @pl.when(g < n_groups)
    def _():
        process(k_refs, v_refs, lf_refs, None)

    @pl.when(g == n_groups)
    def _():
        tq = lax.broadcasted_iota(jnp.int32, (RQ, PAGE_SIZE), 0) % T_new
        key = lax.broadcasted_iota(jnp.int32, (RQ, PAGE_SIZE), 1)
        process([knew_ref], [vnew_ref], [lfnew_ref], key <= tq)
        o_ref[...] = acc_ref[...] / l_ref[...]


def fox_sample(q, k, v, logf, cache_k, cache_v, cache_lf, layer, page_table):
    B, T = q.shape[:2]
    NP = page_table.shape[1]
    G = min(FOX_PAGES_PER_STEP, NP)
    n_groups = NP // G
    n_pool = cache_k.shape[1]
    rows = PAGE_SIZE * KVH_C
    ck = cache_k.reshape(cache_k.shape[0], n_pool, rows, HD_C)
    cv = cache_v.reshape(cache_v.shape[0], n_pool, rows, HD_C)
    clf = jnp.swapaxes(cache_lf, -1, -2)
    qr = jnp.swapaxes(q, 1, 2).reshape(B, H_C * T, HD_C)
    padt = lambda x: jnp.pad(x, ((0, 0), (0, PAGE_SIZE - T), (0, 0), (0, 0))).reshape(B, rows, HD_C)
    knew, vnew = padt(k), padt(v)
    lfnew = jnp.pad(jnp.swapaxes(logf, 1, 2), ((0, 0), (0, 0), (0, PAGE_SIZE - T)))

    def page_spec(i, shape):
        return pl.BlockSpec((None, None) + shape,
                            lambda b, g, pt: (layer, pt[b, jnp.minimum(g, n_groups - 1) * G + i], 0, 0))

    per_b = lambda shape: pl.BlockSpec((None,) + shape, lambda b, g, pt: (b, 0, 0))
    in_specs = ([per_b((H_C * T, HD_C))]
                + [page_spec(i, (rows, HD_C)) for i in range(G)]
                + [page_spec(i, (rows, HD_C)) for i in range(G)]
                + [page_spec(i, (H_C, PAGE_SIZE)) for i in range(G)]
                + [per_b((rows, HD_C)), per_b((rows, HD_C)), per_b((H_C, PAGE_SIZE))])
    RQ = H_C * T
    o = pl.pallas_call(
        functools.partial(_fox_sample_body, G=G, T_new=T),
        grid_spec=pltpu.PrefetchScalarGridSpec(
            num_scalar_prefetch=1,
            grid=(B, n_groups + 1),
            in_specs=in_specs,
            out_specs=per_b((RQ, HD_C)),
            scratch_shapes=[pltpu.VMEM((RQ, HD_C), jnp.float32), pltpu.VMEM((RQ, HD_C), jnp.float32),
                            pltpu.VMEM((RQ, HD_C), jnp.float32), pltpu.VMEM((H_C, 1), jnp.float32)]),
        out_shape=jax.ShapeDtypeStruct((B, RQ, HD_C), jnp.float32),
        compiler_params=pltpu.CompilerParams(dimension_semantics=("parallel", "arbitrary"),
                                             vmem_limit_bytes=VMEM_LIMIT_BYTES),
        name="fox_sample_paged_attention",
    )(page_table, qr, *([ck] * G), *([cv] * G), *([clf] * G), knew, vnew, lfnew)
    return jnp.swapaxes(o.reshape(B, H_C, T, HD_C), 1, 2).reshape(B, T, H_C * HD_C)


def _fox_cumsum_body(lf_ref, f_ref, *, blk):
    T = lf_ref.shape[-1]
    tri = _upper_tri(blk)
    carry = jnp.zeros((lf_ref.shape[0], 1), jnp.float32)
    for j in range(T // blk):
        sl = slice(j * blk, (j + 1) * blk)
        Fp = _dot3(_split3(lf_ref[:, sl]), tri) + carry
        f_ref[:, sl] = Fp
        carry = Fp[:, blk - 1:blk]


def _fox_prompt_body(q_ref, kv_ref, f_ref, o_ref, qs_ref, m_ref, l_ref, acc_ref, *, tq):
    qi = pl.program_id(1)
    kv = pl.program_id(2)
    head_cols = lambda g: pl.ds(pl.multiple_of((kv * G_C + g) * HD_C, HD_C), HD_C)
    k_cols = pl.ds(pl.multiple_of(kv * HD_C, HD_C), HD_C)
    v_cols = pl.ds(pl.multiple_of(KV_C + kv * HD_C, HD_C), HD_C)
    for g in range(G_C):
        qs_ref[g * tq:(g + 1) * tq, :] = _bf(q_ref[:, head_cols(g)] * (HD_C ** -0.5))
    m_ref[...] = jnp.full_like(m_ref, NEG_BIG)
    l_ref[...] = jnp.zeros_like(l_ref)
    acc_ref[...] = jnp.zeros_like(acc_ref)
    qpos = qi * tq + lax.broadcasted_iota(jnp.int32, (G_C * tq, tq), 0) % tq
    kcol = lax.broadcasted_iota(jnp.int32, (G_C * tq, tq), 1)

    def body(j, carry):
        ks = pl.multiple_of(j * tq, tq)
        K = _bf(kv_ref[pl.ds(ks, tq), k_cols])
        V = _bf(kv_ref[pl.ds(ks, tq), v_cols])
        s = lax.dot_general(qs_ref[...], K, (((1,), (1,)), ((), ())), preferred_element_type=jnp.float32)
        F = f_ref[:, pl.ds(ks, tq)]
        bias = jnp.concatenate([jnp.broadcast_to(F[g:g + 1, :], (tq, tq)) for g in range(G_C)], axis=0)
        s = jnp.where(kcol + ks <= qpos, s - bias, NEG_BIG)
        _online_softmax_step(s, V, m_ref, l_ref, acc_ref, slice(None))
        return carry

    lax.fori_loop(0, qi + 1, body, 0)
    o = acc_ref[...] / l_ref[...]
    for g in range(G_C):
        o_ref[:, head_cols(g)] = o[g * tq:(g + 1) * tq, :]


def fox_prompt(q, kv, logf):
    B, T = q.shape[:2]
    tq = min(FOX_Q_TILE, T)
    lfT = jnp.swapaxes(logf, 1, 2)
    F = pl.pallas_call(
        functools.partial(_fox_cumsum_body, blk=tq),
        grid=(B,),
        in_specs=[pl.BlockSpec((None, H_C, T), lambda b: (b, 0, 0))],
        out_specs=pl.BlockSpec((None, H_C, T), lambda b: (b, 0, 0)),
        out_shape=jax.ShapeDtypeStruct((B, H_C, T), jnp.float32),
        name="fox_logf_cumsum",
    )(lfT)
    o = pl.pallas_call(
        functools.partial(_fox_prompt_body, tq=tq),
        grid=(B, T // tq, KVH_C),
        in_specs=[pl.BlockSpec((None, tq, C_C), lambda b, qi, kv: (b, qi, 0)),
                  pl.BlockSpec((None, T, 2 * KV_C), lambda b, qi, kv: (b, 0, 0)),
                  pl.BlockSpec((None, None, G_C, T), lambda b, qi, kv: (b, kv, 0, 0))],
        out_specs=pl.BlockSpec((None, tq, C_C), lambda b, qi, kv: (b, qi, 0)),
        out_shape=jax.ShapeDtypeStruct((B, T, C_C), jnp.float32),
        scratch_shapes=[pltpu.VMEM((G_C * tq, HD_C), jnp.bfloat16),
                        pltpu.VMEM((G_C * tq, HD_C), jnp.float32), pltpu.VMEM((G_C * tq, HD_C), jnp.float32),
                        pltpu.VMEM((G_C * tq, HD_C), jnp.float32)],
        compiler_params=pltpu.CompilerParams(dimension_semantics=("parallel", "parallel", "arbitrary"),
                                             vmem_limit_bytes=VMEM_LIMIT_LARGE),
        name="fox_prompt_flash_attention",
    )(q, kv, F.reshape(B, KVH_C, G_C, T))
    return o


MOE_ROWS = 256
MOE_COMBINE_TOKENS = 64


def _moe_up_body(blk_exp_ref, nused_ref, row_tok_ref, roww_ref, x_hbm, wg_ref, wu_ref, act_ref,
                 xbuf, wg_bf, wu_bf, sem, *, RB):
    b = pl.program_id(0)
    nused = nused_ref[0]
    slot = b % 2

    def gather(blk, slot_):
        def body(i, carry):
            t = row_tok_ref[blk * RB + i]
            pltpu.make_async_copy(x_hbm.at[pl.ds(t, 1)], xbuf.at[slot_, pl.ds(i, 1)], sem.at[slot_]).start()
            return carry
        lax.fori_loop(0, RB, body, 0, unroll=8)

    @pl.when(b == 0)
    def _():
        gather(0, 0)

    @pl.when(b + 1 < nused)
    def _():
        gather(b + 1, 1 - slot)

    @pl.when(b < nused)
    def _():
        pltpu.make_async_copy(x_hbm.at[pl.ds(0, RB)], xbuf.at[slot], sem.at[slot]).wait()
        e = blk_exp_ref[b]
        e_prev = blk_exp_ref[jnp.maximum(b - 1, 0)]

        @pl.when((b == 0) | (e != e_prev))
        def _():
            wg_bf[...] = wg_ref[...].astype(jnp.bfloat16)
            wu_bf[...] = wu_ref[...].astype(jnp.bfloat16)

        x = xbuf[slot].astype(jnp.bfloat16)
        g = jnp.dot(x, wg_bf[...], preferred_element_type=jnp.float32)
        u = jnp.dot(x, wu_bf[...], preferred_element_type=jnp.float32)
        act_ref[...] = (g * jax.nn.sigmoid(g)) * u * roww_ref[...]

    @pl.when(b >= nused)
    def _():
        act_ref[...] = jnp.zeros_like(act_ref)


def _moe_down_body(blk_exp_ref, nused_ref, act_ref, wd_ref, y_ref, wd_bf):
    b = pl.program_id(0)
    nused = nused_ref[0]

    @pl.when(b < nused)
    def _():
        e = blk_exp_ref[b]
        e_prev = blk_exp_ref[jnp.maximum(b - 1, 0)]

        @pl.when((b == 0) | (e != e_prev))
        def _():
            wd_bf[...] = wd_ref[...].astype(jnp.bfloat16)

        y_ref[...] = jnp.dot(act_ref[...].astype(jnp.bfloat16), wd_bf[...], preferred_element_type=jnp.float32)

    @pl.when(b >= nused)
    def _():
        y_ref[...] = jnp.zeros_like(y_ref)


def _moe_combine_body(pos_ref, y_hbm, shared_ref, out_ref, buf, sem, *, TB):
    b = pl.program_id(0)
    nb = pl.num_programs(0)
    slot = b % 2

    def gather(blk, slot_):
        def body(i, carry):
            for kk in range(TOP_K):
                p = pos_ref[(blk * TB + i) * TOP_K + kk]
                pltpu.make_async_copy(y_hbm.at[pl.ds(p, 1)], buf.at[slot_, pl.ds(kk * TB + i, 1)],
                                      sem.at[slot_]).start()
            return carry
        lax.fori_loop(0, TB, body, 0, unroll=2)

    @pl.when(b == 0)
    def _():
        gather(0, 0)

    @pl.when(b + 1 < nb)
    def _():
        gather(b + 1, 1 - slot)

    pltpu.make_async_copy(y_hbm.at[pl.ds(0, TOP_K * TB)], buf.at[slot], sem.at[slot]).wait()
    acc = shared_ref[...]
    for kk in range(TOP_K):
        acc = acc + buf[slot, kk * TB:(kk + 1) * TB]
    out_ref[...] = acc


def _moe_invert_body(pos_ref, src_ref):
    def clear(i, carry):
        src_ref[i] = -1
        return carry
    lax.fori_loop(0, src_ref.shape[0], clear, 0, unroll=8)

    def put(i, carry):
        src_ref[pos_ref[i]] = i
        return carry
    lax.fori_loop(0, pos_ref.shape[0], put, 0, unroll=8)


ROUTE_TOKENS = 256
NEG_INF = float("-inf")


def _first_max(x, idx, axis, n):
    m = jnp.max(x, axis=axis, keepdims=True)
    first = jnp.min(jnp.where(x == m, idx, n), axis=axis, keepdims=True)
    return m, first


def _moe_router_body(wt_ref, h_ref, bias_ref, eidx_ref, gate_ref):
    tn = h_ref.shape[0]
    per_group = N_EXPERTS // N_GROUPS
    logits = lax.dot_general(_bf(wt_ref[...]), _bf(h_ref[...]), (((1,), (1,)), ((), ())),
                             preferred_element_type=jnp.float32)
    scores = jax.nn.sigmoid(logits)
    biased = scores + bias_ref[:, :1]
    b3 = biased.reshape(N_GROUPS, per_group, tn)
    i3 = lax.broadcasted_iota(jnp.int32, b3.shape, 1)
    m1, f1 = _first_max(b3, i3, 1, per_group)
    m2 = jnp.max(jnp.where(i3 == f1, NEG_INF, b3), axis=1, keepdims=True)
    grp = (m1 + m2).reshape(N_GROUPS, tn)
    gi = lax.broadcasted_iota(jnp.int32, grp.shape, 0)
    keep = jnp.zeros(grp.shape, jnp.float32)
    for _ in range(TOPK_GROUPS):
        _, f = _first_max(grp, gi, 0, N_GROUPS)
        keep = jnp.where(gi == f, 1.0, keep)
        grp = jnp.where(gi == f, NEG_INF, grp)
    cand = jnp.where(keep.reshape(N_GROUPS, 1, tn) > 0.5, b3, NEG_INF).reshape(N_EXPERTS, tn)
    ei = lax.broadcasted_iota(jnp.int32, cand.shape, 0)
    ids, gs = [], []
    for _ in range(TOP_K):
        _, f = _first_max(cand, ei, 0, N_EXPERTS)
        ids.append(f)
        gs.append(jnp.sum(jnp.where(ei == f, scores, 0.0), axis=0, keepdims=True))
        cand = jnp.where(ei == f, NEG_INF, cand)
    tot = gs[0]
    for x in gs[1:]:
        tot = tot + x
    pad = eidx_ref.shape[0] - TOP_K
    eidx_ref[...] = jnp.concatenate(ids + [jnp.zeros((pad, tn), jnp.int32)], axis=0)
    gate_ref[...] = jnp.concatenate([x / tot * ROUTED_SCALE for x in gs] + [jnp.zeros((pad, tn), jnp.float32)],
                                    axis=0)


def moe_route(h, w_router, b_router):
    T, D = h.shape
    tn = ROUTE_TOKENS
    rows = 8
    bias = jnp.broadcast_to(b_router[:, None], (N_EXPERTS, 128))
    eidx, gates = pl.pallas_call(
        _moe_router_body,
        grid=(T // tn,),
        in_specs=[pl.BlockSpec((N_EXPERTS, D), lambda i: (0, 0)),
                  pl.BlockSpec((tn, D), lambda i: (i, 0)),
                  pl.BlockSpec((N_EXPERTS, 128), lambda i: (0, 0))],
        out_specs=[pl.BlockSpec((rows, tn), lambda i: (0, i)), pl.BlockSpec((rows, tn), lambda i: (0, i))],
        out_shape=[jax.ShapeDtypeStruct((rows, T), jnp.int32), jax.ShapeDtypeStruct((rows, T), jnp.float32)],
        compiler_params=pltpu.CompilerParams(dimension_semantics=("parallel",)),
        name="moe_router_topk",
    )(w_router.T, h, bias)
    return eidx[:TOP_K].T, gates[:TOP_K].T


def moe_layout(eidx, gates, RB):
    T = eidx.shape[0]
    A = T * TOP_K
    E = N_EXPERTS
    e_flat = eidx.reshape(A).astype(jnp.int32)
    g_flat = gates.reshape(A)
    onehot = (e_flat[:, None] == jnp.arange(E, dtype=jnp.int32)[None, :]).astype(jnp.int32)
    rank_incl = jnp.cumsum(onehot, axis=0)
    rank = jnp.sum(onehot * rank_incl, axis=1) - 1
    counts = rank_incl[-1]
    padded = (counts + RB - 1) // RB * RB
    seg_start = jnp.cumsum(counts) - counts
    pad_end = jnp.cumsum(padded)
    pad_start = pad_end - padded
    pos = pad_start[e_flat] + rank
    n_blocks = -(-(A + E * (RB - 1)) // RB)
    n_rows = n_blocks * RB
    blk_exp = jnp.minimum(jnp.searchsorted(pad_end, jnp.arange(n_blocks, dtype=jnp.int32) * RB, side='right'),
                          E - 1).astype(jnp.int32)
    nused = (pad_end[-1] // RB).astype(jnp.int32).reshape(1)
    pos = pos.astype(jnp.int32)
    src = pl.pallas_call(
        _moe_invert_body,
        grid_spec=pltpu.PrefetchScalarGridSpec(
            num_scalar_prefetch=1, grid=(1,), in_specs=[],
            out_specs=pl.BlockSpec(memory_space=pltpu.SMEM)),
        out_shape=jax.ShapeDtypeStruct((n_rows,), jnp.int32),
        name="moe_invert_positions",
    )(pos)
    valid = src >= 0
    row_tok = jnp.where(valid, src // TOP_K, 0).astype(jnp.int32)
    row_w = jnp.where(valid, g_flat[jnp.maximum(src, 0)], 0.0).astype(jnp.float32)
    return pos, row_tok, row_w, blk_exp, nused, n_blocks


def moe_routed_plus_shared(h, layer, eidx, gates, w_gate, w_up, w_down, shared):
    T, D = h.shape
    F = w_gate.shape[-1]
    RB = MOE_ROWS
    TB = MOE_COMBINE_TOKENS
    pos, row_tok, row_w, blk_exp, nused, n_blocks = moe_layout(eidx, gates, RB)
    n_rows = n_blocks * RB
    wspec_up = pl.BlockSpec((None, None, D, F), lambda b, be, nu, rt: (layer, be[b], 0, 0))
    act = pl.pallas_call(
        functools.partial(_moe_up_body, RB=RB),
        grid_spec=pltpu.PrefetchScalarGridSpec(
            num_scalar_prefetch=3,
            grid=(n_blocks,),
            in_specs=[pl.BlockSpec((RB, 1), lambda b, be, nu, rt: (b, 0)),
                      pl.BlockSpec(memory_space=pl.ANY),
                      wspec_up, wspec_up],
            out_specs=pl.BlockSpec((RB, F), lambda b, be, nu, rt: (b, 0)),
            scratch_shapes=[pltpu.VMEM((2, RB, D), jnp.float32),
                            pltpu.VMEM((D, F), jnp.bfloat16),
                            pltpu.VMEM((D, F), jnp.bfloat16),
                            pltpu.SemaphoreType.DMA((2,))]),
        out_shape=jax.ShapeDtypeStruct((n_rows, F), jnp.float32),
        compiler_params=pltpu.CompilerParams(dimension_semantics=("arbitrary",),
                                             vmem_limit_bytes=VMEM_LIMIT_LARGE),
        name="moe_gate_up",
    )(blk_exp, nused, row_tok, row_w.reshape(n_rows, 1), h, w_gate, w_up)
    y = pl.pallas_call(
        _moe_down_body,
        grid_spec=pltpu.PrefetchScalarGridSpec(
            num_scalar_prefetch=2,
            grid=(n_blocks,),
            in_specs=[pl.BlockSpec((RB, F), lambda b, be, nu: (b, 0)),
                      pl.BlockSpec((None, None, F, D), lambda b, be, nu: (layer, be[b], 0, 0))],
            out_specs=pl.BlockSpec((RB, D), lambda b, be, nu: (b, 0)),
            scratch_shapes=[pltpu.VMEM((F, D), jnp.bfloat16)]),
        out_shape=jax.ShapeDtypeStruct((n_rows, D), jnp.float32),
        compiler_params=pltpu.CompilerParams(dimension_semantics=("arbitrary",),
                                             vmem_limit_bytes=VMEM_LIMIT_LARGE),
        name="moe_down",
    )(blk_exp, nused, act, w_down)
    return pl.pallas_call(
        functools.partial(_moe_combine_body, TB=TB),
        grid_spec=pltpu.PrefetchScalarGridSpec(
            num_scalar_prefetch=1,
            grid=(T // TB,),
            in_specs=[pl.BlockSpec(memory_space=pl.ANY),
                      pl.BlockSpec((TB, D), lambda b, ps: (b, 0))],
            out_specs=pl.BlockSpec((TB, D), lambda b, ps: (b, 0)),
            scratch_shapes=[pltpu.VMEM((2, TOP_K * TB, D), jnp.float32),
                            pltpu.SemaphoreType.DMA((2,))]),
        out_shape=jax.ShapeDtypeStruct((T, D), jnp.float32),
        compiler_params=pltpu.CompilerParams(dimension_semantics=("arbitrary",),
                                             vmem_limit_bytes=VMEM_LIMIT_LARGE),
        name="moe_combine",
    )(pos, y, shared)


def moe_ffn(h, layer, w_router, b_router, w_gate, w_up, w_down, ws_gate, ws_up, ws_down):
    eidx, gates = moe_route(h, w_router[layer], b_router[layer])
    shared = matmul(jax.nn.silu(matmul(h, ws_gate[layer])) * matmul(h, ws_up[layer]), ws_down[layer])
    return moe_routed_plus_shared(h, layer, eidx, gates, w_gate, w_up, w_down, shared)


def kernel(x_prompt, x_sample, c_prompt, c_sample, state_rwkv, state_rwkv_shift, cache_swa_k, cache_swa_v, cache_fox_k, cache_fox_v, cache_fox_logf, page_table, ada_w, ada_b, norm_w, final_norm_w, ab_w_in, ab_w_out, rwkv_mu, rwkv_w0, rwkv_w_decay_up, rwkv_a0, rwkv_w_aaa_up, rwkv_w_gate_up, rwkv_k_k, rwkv_k_a, rwkv_r_k, rwkv_lnx_w, rwkv_lnx_b, swa_sinks, fox_w_in, fox_b_f, fox_w_out, moe_w_router, moe_b_router, moe_w_gate, moe_w_up, moe_w_down, shared_w_gate, shared_w_up, shared_w_down):
    Bp, Tp, D = x_prompt.shape
    Bs, Ts, _ = x_sample.shape
    depth = ada_w.shape[0]
    past_len = page_table.shape[1] * PAGE_SIZE
    pos_p = jnp.arange(Tp)
    pos_s = past_len + jnp.arange(Ts)
    xp, xs = x_prompt, x_sample
    rw_S_p, rw_sh_p, sw_k_p, sw_v_p, fk_p, fv_p, flf_p = [], [], [], [], [], [], []
    rw_S_s, rw_sh_s, sw_k_s, sw_v_s, fk_s, fv_s, flf_s = [], [], [], [], [], [], []
    for l in range(depth):
        mods = ada_params(jnp.concatenate([c_prompt, c_sample], axis=0), ada_w, l, ada_b[l])
        sh1_p, sc1_p, g1_p, sh2_p, sc2_p, g2_p = [m[:Bp] for m in mods]
        sh1_s, sc1_s, g1_s, sh2_s, sc2_s, g2_s = [m[Bp:] for m in mods]
        hp = rmsnorm(xp, norm_w[l, 0], sc1_p, sh1_p, jnp.bfloat16)
        hs = rmsnorm(xs, norm_w[l, 0], sc1_s, sh1_s, jnp.bfloat16)
        if l % 2 == 0:
            i = l // 2
            rw = (rwkv_mu[i], rwkv_w0[i], rwkv_w_decay_up[i], rwkv_a0[i], rwkv_w_aaa_up[i], rwkv_w_gate_up[i],
                  rwkv_k_k[i], rwkv_k_a[i], rwkv_r_k[i], rwkv_lnx_w[i], rwkv_lnx_b[i])
            o_p, S_p, row_p, kw_p, vw_p = ab_mixer(
                hp, pos_p, jnp.zeros((Bp, P_A), jnp.float32), jnp.zeros((Bp, H_A, HD_A, HD_A), jnp.float32),
                None, None, ab_w_in[i], ab_w_out[i], rw, swa_sinks[i])
            o_s, S_s, row_s, kw_s, vw_s = ab_mixer(
                hs, pos_s, state_rwkv_shift[i], state_rwkv[i], cache_swa_k[i], cache_swa_v[i],
                ab_w_in[i], ab_w_out[i], rw, swa_sinks[i])
            rw_S_p.append(S_p); rw_sh_p.append(row_p); sw_k_p.append(kw_p); sw_v_p.append(vw_p)
            rw_S_s.append(S_s); rw_sh_s.append(row_s); sw_k_s.append(kw_s); sw_v_s.append(vw_s)
        else:
            j = l // 2
            q_p, kv_p, lf_p = fox_project(hp, fox_w_in[j], fox_b_f[j])
            q_s, kv_s, lf_s = fox_project(hs, fox_w_in[j], fox_b_f[j])
            k_p, v_p = (kv_p[..., c:c + KV_C].reshape(Bp, Tp, KVH_C, HD_C) for c in (0, KV_C))
            k_s, v_s = (kv_s[..., c:c + KV_C].reshape(Bs, Ts, KVH_C, HD_C) for c in (0, KV_C))
            o_p = mm3(fox_prompt(q_p, kv_p, lf_p), fox_w_out[j])
            o_s = mm3(fox_sample(q_s.reshape(Bs, Ts, H_C, HD_C), k_s, v_s, lf_s,
                                 cache_fox_k, cache_fox_v, cache_fox_logf, j, page_table),
                      fox_w_out[j])
            fk_p.append(k_p); fv_p.append(v_p); flf_p.append(lf_p)
            fk_s.append(k_s); fv_s.append(v_s); flf_s.append(lf_s)
        xp = xp + g1_p * o_p
        xs = xs + g1_s * o_s
        h2p = rmsnorm(xp, norm_w[l, 1], sc2_p, sh2_p)
        h2s = rmsnorm(xs, norm_w[l, 1], sc2_s, sh2_s)
        tok = jnp.concatenate([h2p.reshape(Bp * Tp, D), h2s.reshape(Bs * Ts, D)], axis=0)
        y = moe_ffn(tok, l, moe_w_router, moe_b_router, moe_w_gate, moe_w_up, moe_w_down,
                    shared_w_gate, shared_w_up, shared_w_down)
        xp = xp + g2_p * y[:Bp * Tp].reshape(Bp, Tp, D)
        xs = xs + g2_s * y[Bp * Tp:].reshape(Bs, Ts, D)
    y_prompt = rmsnorm(xp, final_norm_w)
    y_sample = rmsnorm(xs, final_norm_w)
    return (y_prompt, y_sample,
            jnp.stack(rw_S_p), jnp.stack(rw_sh_p), jnp.stack(sw_k_p), jnp.stack(sw_v_p),
            jnp.stack(fk_p), jnp.stack(fv_p), jnp.stack(flf_p),
            jnp.stack(rw_S_s), jnp.stack(rw_sh_s), jnp.stack(sw_k_s), jnp.stack(sw_v_s),
            jnp.stack(fk_s), jnp.stack(fv_s), jnp.stack(flf_s))
```

```python
import functools

import jax
import jax.numpy as jnp
from jax import lax
from jax.experimental import pallas as pl
from jax.experimental.pallas import tpu as pltpu

D_MODEL = 4096
PAGE_SIZE = 128

H_A = 32
HD_A = 64
C_A = H_A * HD_A
LORA_DECAY = 128
LORA_AAA = 128
LORA_GATE = 480
P_A = 3 * C_A + LORA_DECAY + LORA_AAA + LORA_GATE
RWKV_SPLIT = (C_A, 2 * C_A, 3 * C_A, 3 * C_A + LORA_DECAY, 3 * C_A + LORA_DECAY + LORA_AAA)
GN_EPS_A = 64e-5

H_B = 32
KVH_B = 4
G_B = H_B // KVH_B
HD_B = 64
C_B = H_B * HD_B
KV_B = KVH_B * HD_B
WINDOW = 128
ROPE_THETA = 10000.0

H_C = 32
KVH_C = 8
G_C = H_C // KVH_C
HD_C = 128
C_C = H_C * HD_C
KV_C = KVH_C * HD_C
Q_BLOCK = 128

N_EXPERTS = 64
TOP_K = 6
N_GROUPS = 8
TOPK_GROUPS = 4
ROUTED_SCALE = 2.5
MOE_BLOCK = 128

NORM_EPS = 1e-6

VMEM_LIMIT_BYTES = 48 * 1024 * 1024
VMEM_LIMIT_LARGE = 56 * 1024 * 1024


MM_VMEM_BUDGET = 40 * 1024 * 1024


def _mm_body(x_ref, w_ref, o_ref):
    o_ref[...] = jnp.dot(x_ref[...], w_ref[...].astype(jnp.bfloat16), preferred_element_type=jnp.float32)


def _pick_tile(n, candidates):
    for c in candidates:
        if n % c == 0:
            return c
    return n


def _mm_tiles(M, K, N):
    tn = _pick_tile(N, (512, 256, 128))
    for tm in (1024, 768, 512, 256, 128):
        if M % tm == 0 and 2 * (tm * K * 2 + K * tn * 4 + tm * tn * 4) <= MM_VMEM_BUDGET:
            return tm, tn
    return M, tn


def matmul(x, w, layer=None):
    M, K = x.shape
    N = w.shape[-1]
    if layer is None:
        w_spec = lambda tn: pl.BlockSpec((K, tn), lambda i, j: (0, j))
    else:
        w_spec = lambda tn: pl.BlockSpec((None, K, tn), lambda i, j: (layer, 0, j))
    tm, tn = _mm_tiles(M, K, N)
    return pl.pallas_call(
        _mm_body,
        grid=(M // tm, N // tn),
        in_specs=[pl.BlockSpec((tm, K), lambda i, j: (i, 0)), w_spec(tn)],
        out_specs=pl.BlockSpec((tm, tn), lambda i, j: (i, j)),
        out_shape=jax.ShapeDtypeStruct((M, N), jnp.float32),
        compiler_params=pltpu.CompilerParams(
            dimension_semantics=("parallel", "parallel"),
            vmem_limit_bytes=VMEM_LIMIT_BYTES),
        name="dense_matmul",
    )(x.astype(jnp.bfloat16), w)


def mm3(h, w):
    B, T, K = h.shape
    return matmul(h.reshape(B * T, K), w).reshape(B, T, w.shape[1])


HD = HD_A
PAIR = 2 * HD
RWKV_CHUNK = 64
RWKV_PAIRS_PER_STEP = 4


def _bf(x):
    return x.astype(jnp.bfloat16)


def _dot(a, b):
    return jnp.dot(_bf(a), _bf(b), preferred_element_type=jnp.float32)


def _dot_nt(a, b):
    return lax.dot_general(_bf(a), _bf(b), (((1,), (1,)), ((), ())), preferred_element_type=jnp.float32)


def _split3(x):
    h = x.astype(jnp.bfloat16)
    r1 = x - h.astype(jnp.float32)
    m = r1.astype(jnp.bfloat16)
    l = (r1 - m.astype(jnp.float32)).astype(jnp.bfloat16)
    return h, m, l


def _rwkv_body(s0_ref, r_ref, lw_ref, k_ref, v_ref, kk_ref, b_ref, y_ref, sT_ref, s_scr, *, C, hp):
    c = pl.program_id(2)

    @pl.when(c == 0)
    def _():
        s_scr[...] = s0_ref[...]

    n = 2 * C
    row = lax.broadcasted_iota(jnp.int32, (n, n), 0)
    col = lax.broadcasted_iota(jnp.int32, (n, n), 1)
    same_head = (row >= C) == (col >= C)
    tt = row & (C - 1)
    ss = col & (C - 1)
    strict = same_head & (ss < tt)
    incl = same_head & (ss <= tt)
    trow = lax.broadcasted_iota(jnp.int32, (C, C), 0)
    tcol = lax.broadcasted_iota(jnp.int32, (C, C), 1)
    tri = jnp.where(tcol <= trow, 1.0, 0.0).astype(jnp.bfloat16)
    lane = lax.broadcasted_iota(jnp.int32, (1, PAIR), 1)
    lo = jnp.where(lane < HD, 1.0, 0.0)
    hi = 1.0 - lo
    vrow = lax.broadcasted_iota(jnp.int32, (PAIR, PAIR), 0)
    vcol = lax.broadcasted_iota(jnp.int32, (PAIR, PAIR), 1)
    bd = (vrow >= HD) == (vcol >= HD)
    eye = jnp.where(row == col, 1.0, 0.0)
    nlev = C.bit_length() - 1

    P = range(hp)
    sls = [slice(p * PAIR, (p + 1) * PAIR) for p in P]
    lw = [lw_ref[:, s] for s in sls]
    r = [r_ref[:, s] for s in sls]
    k = [k_ref[:, s] for s in sls]
    v = [v_ref[:, s] for s in sls]
    kk = [kk_ref[:, s] for s in sls]
    b = [b_ref[:, s] for s in sls]
    csum = lambda x: jnp.dot(tri, x, preferred_element_type=jnp.float32)
    w3 = [_split3(x) for x in lw]
    Lc = [csum(h) + (csum(m) + csum(l)) for h, m, l in w3]
    e_inc = [jnp.exp(x) for x in Lc]
    e_neg = [jnp.exp(-x) for x in Lc]
    e_end = [jnp.exp(x[C - 1:C, :] - x) for x in Lc]
    Kq = [kk[p] * jnp.exp(Lc[p] - lw[p]) for p in P]
    Rq = [r[p] * e_inc[p] for p in P]
    Bd = [b[p] * e_neg[p] for p in P]
    Kd = [k[p] * e_neg[p] for p in P]
    G = [_dot_nt(jnp.concatenate([Kq[p] * lo, Kq[p] * hi, Rq[p] * lo, Rq[p] * hi], axis=0),
                 jnp.concatenate([Bd[p], Bd[p], Kd[p], Kd[p]], axis=0)) for p in P]
    Mb = [jnp.where(strict, g[:n, :n], 0.0) for g in G]
    Mk = [jnp.where(strict, g[:n, n:], 0.0) for g in G]
    Nb = [jnp.where(incl, g[n:, :n], 0.0) for g in G]
    Nk = [jnp.where(incl, g[n:, n:], 0.0) for g in G]
    T = [eye - jnp.where((tt >> 1) == (ss >> 1), m, 0.0) for m in Mb]
    for j in range(2, nlev + 1):
        lower_left = ((tt >> j) == (ss >> j)) & ((tt >> (j - 1)) > (ss >> (j - 1)))
        TM = [_dot(T[p], jnp.where(lower_left, Mb[p], 0.0)) for p in P]
        T = [T[p] - _dot(TM[p], T[p]) for p in P]
    V2 = [jnp.concatenate([x * lo, x * hi], axis=0) for x in v]
    MkV = [_dot(Mk[p], V2[p]) for p in P]
    NkV = [_dot(Nk[p], V2[p]) for p in P]
    BK = [jnp.concatenate([b[p] * e_end[p], k[p] * e_end[p]], axis=0) for p in P]
    S = [s_scr[p] for p in P]
    W0 = [_dot_nt(jnp.concatenate([Kq[p], Rq[p]], axis=0), S[p]) for p in P]
    rhs_u = [-(W0[p][:C] + (MkV[p][:C] + MkV[p][C:])) for p in P]
    UU = [_dot(T[p], jnp.concatenate([rhs_u[p] * lo, rhs_u[p] * hi], axis=0)) for p in P]
    YY = [_dot(Nb[p], UU[p]) + NkV[p] for p in P]
    UV = [jnp.concatenate([UU[p][:C] + UU[p][C:], v[p]], axis=0) for p in P]
    upd = [_dot(UV[p].T, BK[p]) for p in P]
    for p in P:
        y_ref[:, sls[p]] = YY[p][:C] + YY[p][C:] + W0[p][C:]
        S_new = S[p] * e_inc[p][C - 1:C, :] + jnp.where(bd, upd[p], 0.0)
        s_scr[p] = S_new
        sT_ref[p] = S_new


def rwkv7_scan(S0, r, lw, k, v, kk, b):
    B, T, CA = r.shape
    H = CA // HD
    npair = H // 2
    C = RWKV_CHUNK
    hp = min(RWKV_PAIRS_PER_STEP, npair)
    Tp = -(-T // C) * C
    if Tp != T:
        padf = lambda x: jnp.pad(x, ((0, 0), (0, Tp - T), (0, 0)))
        r, lw, k, v, kk, b = map(padf, (r, lw, k, v, kk, b))
    S0p = S0.reshape(B, npair, 2, HD, HD)
    z = jnp.zeros_like(S0p[:, :, 0])
    S0bd = jnp.concatenate([jnp.concatenate([S0p[:, :, 0], z], axis=-1),
                            jnp.concatenate([z, S0p[:, :, 1]], axis=-1)], axis=-2)
    seq_spec = pl.BlockSpec((None, C, hp * PAIR), lambda bb, g, c: (bb, c, g))
    st_spec = pl.BlockSpec((None, hp, PAIR, PAIR), lambda bb, g, c: (bb, g, 0, 0))
    y, Sbd = pl.pallas_call(
        functools.partial(_rwkv_body, C=C, hp=hp),
        grid=(B, npair // hp, Tp // C),
        in_specs=[st_spec] + [seq_spec] * 6,
        out_specs=[seq_spec, st_spec],
        out_shape=[jax.ShapeDtypeStruct((B, Tp, CA), jnp.float32),
                   jax.ShapeDtypeStruct((B, npair, PAIR, PAIR), jnp.float32)],
        scratch_shapes=[pltpu.VMEM((hp, PAIR, PAIR), jnp.float32)],
        compiler_params=pltpu.CompilerParams(
            dimension_semantics=("parallel", "parallel", "arbitrary")),
        name="rwkv7_chunk_scan",
    )(S0bd, r, lw, k, v, kk, b)
    S = jnp.stack([Sbd[:, :, :HD, :HD], Sbd[:, :, HD:, HD:]], axis=2).reshape(B, H, HD, HD)
    return y[:, :T], S


NORM_ROWS = 256


def _norm_body(*refs, modulated, residual, keep_x):
    it = iter(refs)
    x_ref, w_ref = next(it), next(it)
    sc_ref, sh_ref = (next(it), next(it)) if modulated else (None, None)
    r_ref, g_ref = (next(it), next(it)) if residual else (None, None)
    xo_ref = next(it) if keep_x else None
    o_ref = next(it)
    x = x_ref[...]
    if residual:
        x = x + g_ref[...] * r_ref[...]
    if keep_x:
        xo_ref[...] = x
    y = x * lax.rsqrt(jnp.mean(x * x, axis=-1, keepdims=True) + NORM_EPS) * w_ref[...]
    if modulated:
        y = y * (1 + sc_ref[...]) + sh_ref[...]
    o_ref[...] = y.astype(o_ref.dtype)


def rmsnorm(x, w, scale=None, shift=None, out_dtype=jnp.float32, resid=None, gate=None, resid_row0=0,
            keep_x=False):
    B, T, D = x.shape
    tt = min(NORM_ROWS, T)
    row_spec = pl.BlockSpec((None, tt, D), lambda b, t: (b, t, 0))
    mod_spec = pl.BlockSpec((None, 1, D), lambda b, t: (b, 0, 0))
    operands = [x, w.reshape(1, D)]
    in_specs = [row_spec, pl.BlockSpec((1, D), lambda b, t: (0, 0))]
    if scale is not None:
        operands += [scale, shift]
        in_specs += [mod_spec, mod_spec]
    if resid is not None:
        blk0, per_b = resid_row0 // tt, T // tt
        operands += [resid, gate]
        in_specs += [pl.BlockSpec((tt, D), lambda b, t: (blk0 + b * per_b + t, 0)), mod_spec]
    h_shape = jax.ShapeDtypeStruct((B, T, D), out_dtype)
    out = pl.pallas_call(
        functools.partial(_norm_body, modulated=scale is not None, residual=resid is not None, keep_x=keep_x),
        grid=(B, T // tt),
        in_specs=in_specs,
        out_specs=[row_spec, row_spec] if keep_x else row_spec,
        out_shape=[jax.ShapeDtypeStruct((B, T, D), jnp.float32), h_shape] if keep_x else h_shape,
        compiler_params=pltpu.CompilerParams(dimension_semantics=("parallel", "parallel")),
        name="rmsnorm_modulate",
    )(*operands)
    return tuple(out) if keep_x else out


def ada_params(c, w, layer, b):
    m = matmul(jax.nn.silu(c), w, layer) + b
    return jnp.split(m[:, None, :], 6, axis=-1)


def rwkv7_mixer(pa, prev_row, S0, mu, w0, w_dec_up, a0, w_aaa_up, w_gate_up, k_k, k_a, r_k, lnx_w, lnx_b):
    B, T, _ = pa.shape
    shifted = jnp.concatenate([prev_row[:, None, :], pa[:, :-1]], axis=1)
    m = pa + (shifted - pa) * mu
    r, k, v, wd, ad, gd = jnp.split(m, RWKV_SPLIT, axis=-1)
    w = -jax.nn.softplus(-(w0 + mm3(jnp.tanh(wd), w_dec_up))) - 0.5
    a = jax.nn.sigmoid(a0 + mm3(ad, w_aaa_up))
    g = mm3(jax.nn.sigmoid(gd), w_gate_up)
    heads = lambda t: t.reshape(B, T, H_A, HD_A)
    kk = heads(k * k_k)
    kk = kk / jnp.maximum(jnp.sqrt(jnp.sum(kk * kk, axis=-1, keepdims=True)), 1e-12)
    k = k * (1 + (a - 1) * k_a)
    rh, kh, vh = heads(r), heads(k), heads(v)
    y, S = rwkv7_scan(S0, r, -jnp.exp(w), k, v, kk.reshape(B, T, C_A), kk.reshape(B, T, C_A) * a)
    y = heads(y)
    mean = jnp.mean(y, axis=-1, keepdims=True)
    var = jnp.mean(jnp.square(y - mean), axis=-1, keepdims=True)
    y = ((y - mean) * lax.rsqrt(var + GN_EPS_A)).reshape(B, T, C_A) * lnx_w + lnx_b
    bonus = (jnp.sum(rh * kh * r_k, axis=-1, keepdims=True) * vh).reshape(B, T, C_A)
    out = (y + bonus) * g
    return out, S, pa[:, -1]


LANES = 128


def _swa_body(sink_ref, q_ref, kv_ref, cos_ref, sin_ref, kp0_ref, vp0_ref, o_ref, krot_ref, kprev, vprev, *,
              has_cache):
    n = pl.program_id(1)
    W = WINDOW

    @pl.when(n == 0)
    def _():
        kprev[...] = kp0_ref[...]
        vprev[...] = vp0_ref[...]

    cos = cos_ref[...]
    sin = sin_ref[...]
    lane = lax.broadcasted_iota(jnp.int32, (W, LANES), 1)
    first_half = (lane % HD_B) < (HD_B // 2)
    lo = lane < HD_B

    def rope(x):
        rot = jnp.where(first_half, pltpu.roll(x, LANES - HD_B // 2, 1), pltpu.roll(x, HD_B // 2, 1))
        return x * cos + rot * sin

    kcur = jnp.concatenate([rope(kv_ref[:, t * LANES:(t + 1) * LANES]) for t in range(KV_B // LANES)], axis=1)
    vcur = kv_ref[:, KV_B:2 * KV_B]
    krot_ref[...] = kcur
    K2 = jnp.concatenate([kprev[...], kcur], axis=0)
    V2 = jnp.concatenate([vprev[...], vcur], axis=0)
    lo2 = lax.broadcasted_iota(jnp.int32, (2 * W, LANES), 1) < HD_B

    def both_halves(x2, j):
        tile = x2[:, (j // 2) * LANES:(j // 2 + 1) * LANES]
        rolled = pltpu.roll(tile, HD_B, 1)
        return jnp.where(lo2, tile, rolled) if j % 2 == 0 else jnp.where(lo2, rolled, tile)

    R = G_B * W
    t_row = lax.broadcasted_iota(jnp.int32, (R, 2 * W), 0) % W
    c_col = lax.broadcasted_iota(jnp.int32, (R, 2 * W), 1)
    valid = (c_col > t_row) & (c_col <= t_row + W)
    if not has_cache:
        valid = valid & ((c_col >= W) | (n > 0))
    for j in range(KVH_B):
        Kj = _bf(both_halves(K2, j))
        Vj = _bf(both_halves(V2, j))
        rows = []
        for t in range(G_B // 2):
            c0 = (j * (G_B // 2) + t) * LANES
            x = rope(q_ref[:, c0:c0 + LANES]) * (HD_B ** -0.5)
            rows += [jnp.where(lo, x, 0.0), jnp.where(lo, 0.0, x)]
        Q = _bf(jnp.concatenate(rows, axis=0))
        s = lax.dot_general(Q, Kj, (((1,), (1,)), ((), ())), preferred_element_type=jnp.float32)
        s = jnp.where(valid, s, NEG_BIG)
        sk = jnp.concatenate([jnp.full((W, LANES), sink_ref[j * G_B + i], jnp.float32) for i in range(G_B)], axis=0)
        m = jnp.maximum(jnp.max(s, axis=1, keepdims=True), sk)
        p = jnp.exp(s - jnp.concatenate([m, m], axis=1))
        den = jnp.sum(p, axis=1, keepdims=True) + jnp.exp(sk - m)
        o = jnp.dot(_bf(p), Vj, preferred_element_type=jnp.float32) / den
        for t in range(G_B // 2):
            c0 = (j * (G_B // 2) + t) * LANES
            o_ref[:, c0:c0 + LANES] = jnp.where(lo, o[(2 * t) * W:(2 * t + 1) * W], o[(2 * t + 1) * W:(2 * t + 2) * W])
    kprev[...] = kcur
    vprev[...] = vcur


def swa_attention(q, kv, pos, kbuf, vbuf, sinks):
    B, T, _ = q.shape
    W = WINDOW
    Tp = -(-T // W) * W
    if Tp != T:
        q = jnp.pad(q, ((0, 0), (0, Tp - T), (0, 0)))
        kv = jnp.pad(kv, ((0, 0), (0, Tp - T), (0, 0)))
        pos = jnp.pad(pos, (0, Tp - T))
    half = HD_B // 2
    inv = ROPE_THETA ** (-jnp.arange(half, dtype=jnp.float32) / half)
    ang = pos.astype(jnp.float32)[:, None] * inv[None, :]
    cos = jnp.tile(jnp.cos(ang), (1, LANES // half))
    sin = jnp.tile(jnp.concatenate([-jnp.sin(ang), jnp.sin(ang)], axis=1), (1, LANES // HD_B))
    has_cache = kbuf is not None
    if has_cache:
        kp0, vp0 = kbuf.reshape(B, W, KV_B), vbuf.reshape(B, W, KV_B)
    else:
        kp0 = vp0 = jnp.zeros((B, W, KV_B), jnp.float32)
    blk = lambda c: pl.BlockSpec((None, W, c), lambda b, n, s: (b, n, 0))
    tab = pl.BlockSpec((W, LANES), lambda b, n, s: (n, 0))
    per_b = pl.BlockSpec((None, W, KV_B), lambda b, n, s: (b, 0, 0))
    o, krot = pl.pallas_call(
        functools.partial(_swa_body, has_cache=has_cache),
        grid_spec=pltpu.PrefetchScalarGridSpec(
            num_scalar_prefetch=1,
            grid=(B, Tp // W),
            in_specs=[blk(C_B), blk(2 * KV_B), tab, tab, per_b, per_b],
            out_specs=[blk(C_B), blk(KV_B)],
            scratch_shapes=[pltpu.VMEM((W, KV_B), jnp.float32), pltpu.VMEM((W, KV_B), jnp.float32)]),
        out_shape=[jax.ShapeDtypeStruct((B, Tp, C_B), jnp.float32), jax.ShapeDtypeStruct((B, Tp, KV_B), jnp.float32)],
        compiler_params=pltpu.CompilerParams(dimension_semantics=("parallel", "arbitrary")),
        name="swa_sink_attention",
    )(sinks, q, kv, cos, sin, kp0, vp0)
    return o[:, :T], krot[:, :T]


def ab_mixer(h, pos, prev_row, S0, kbuf, vbuf, w_in, w_out, rwkv_params, sinks):
    B, T, _ = h.shape
    h2 = h.reshape(B * T, -1)
    w_parts = (w_in[:, :3 * C_A], w_in[:, 3 * C_A:P_A], w_in[:, P_A:P_A + C_B], w_in[:, P_A + C_B:])
    pa = jnp.concatenate([matmul(h2, w_parts[0]), matmul(h2, w_parts[1])], axis=-1).reshape(B, T, P_A)
    q = matmul(h2, w_parts[2]).reshape(B, T, C_B)
    kv = matmul(h2, w_parts[3]).reshape(B, T, 2 * KV_B)
    o_a, S, last_row = rwkv7_mixer(pa, prev_row, S0, *rwkv_params)
    o_b, k_rot = swa_attention(q, kv, pos, kbuf, vbuf, sinks)
    k = k_rot.reshape(B, T, KVH_B, HD_B)
    v = kv[..., KV_B:].reshape(B, T, KVH_B, HD_B)
    if kbuf is not None:
        k = jnp.concatenate([kbuf, k], axis=1)
        v = jnp.concatenate([vbuf, v], axis=1)
    out = mm3(jnp.concatenate([o_a, o_b], axis=-1), w_out)
    return out, S, last_row, k[:, -WINDOW:], v[:, -WINDOW:]


def fox_project(h, w_in, b_f):
    B, T, _ = h.shape
    h2 = h.reshape(B * T, -1)
    q = matmul(h2, w_in[:, :C_C])
    kv = matmul(h2, w_in[:, C_C:C_C + 2 * KV_C])
    fl = matmul(h2, w_in[:, C_C + 2 * KV_C:])
    logf = jax.nn.log_sigmoid(fl + b_f).reshape(B, T, H_C)
    return q.reshape(B, T, C_C), kv.reshape(B, T, 2 * KV_C), logf


NEG_BIG = -1e30
FOX_PAGES_PER_STEP = 8
FOX_Q_TILE = 256


def _dot3(parts, w):
    f = lambda x: jnp.dot(x, w, preferred_element_type=jnp.float32)
    return f(parts[0]) + (f(parts[1]) + f(parts[2]))


def _dot3_left(w, parts):
    f = lambda x: jnp.dot(w, x, preferred_element_type=jnp.float32)
    return f(parts[0]) + (f(parts[1]) + f(parts[2]))


def _upper_tri(n):
    r = lax.broadcasted_iota(jnp.int32, (n, n), 0)
    c = lax.broadcasted_iota(jnp.int32, (n, n), 1)
    return jnp.where(r <= c, 1.0, 0.0).astype(jnp.bfloat16)


def _online_softmax_step(s, V, m_ref, l_ref, acc_ref, rows):
    m_old = m_ref[rows]
    m_new = jnp.maximum(m_old, jnp.max(s, axis=1, keepdims=True))
    p = jnp.exp(s - jnp.concatenate([m_new] * (s.shape[1] // HD_C), axis=1))
    alpha = jnp.exp(m_old - m_new)
    l_ref[rows] = alpha * l_ref[rows] + jnp.sum(p, axis=1, keepdims=True)
    acc_ref[rows] = alpha * acc_ref[rows] + jnp.dot(_bf(p), V, preferred_element_type=jnp.float32)
    m_ref[rows] = m_new


def _fox_sample_body(pt_ref, q_ref, *rest, G, T_new):
    k_refs = rest[:G]
    v_refs = rest[G:2 * G]
    lf_refs = rest[2 * G:3 * G]
    knew_ref, vnew_ref, lfnew_ref, o_ref, m_ref, l_ref, acc_ref, carry_ref = rest[3 * G:]
    g = pl.program_id(1)
    n_groups = pl.num_programs(1) - 1
    R = G_C * T_new
    RQ = KVH_C * R

    @pl.when(g == 0)
    def _():
        m_ref[...] = jnp.full_like(m_ref, NEG_BIG)
        l_ref[...] = jnp.zeros_like(l_ref)
        acc_ref[...] = jnp.zeros_like(acc_ref)
        carry_ref[...] = jnp.zeros_like(carry_ref)

    q = _bf(q_ref[...] * (HD_C ** -0.5))
    tri = _upper_tri(PAGE_SIZE)
    er = lax.broadcasted_iota(jnp.int32, (RQ, H_C), 0)
    ec = lax.broadcasted_iota(jnp.int32, (RQ, H_C), 1)
    expand = jnp.where(er // T_new == ec, 1.0, 0.0).astype(jnp.bfloat16)

    def process(k_list, v_list, lf_list, mask):
        carry = carry_ref[...]
        Fs = []
        for lf_r in lf_list:
            Fp = _dot3(_split3(lf_r[...]), tri) + carry
            carry = Fp[:, PAGE_SIZE - 1:PAGE_SIZE]
            Fs.append(Fp)
        carry_ref[...] = carry
        F_all = Fs[0] if len(Fs) == 1 else jnp.concatenate(Fs, axis=1)
        bias = _dot3_left(expand, _split3(F_all))
        def head_rows(refs, kv):
            parts = [r[pl.ds(kv, PAGE_SIZE, stride=KVH_C), :] for r in refs]
            return _bf(parts[0] if len(parts) == 1 else jnp.concatenate(parts, axis=0))

        s = jnp.concatenate(
            [lax.dot_general(q[kv * R:(kv + 1) * R], head_rows(k_list, kv), (((1,), (1,)), ((), ())),
                             preferred_element_type=jnp.float32) for kv in range(KVH_C)], axis=0) - bias
        if mask is not None:
            s = jnp.where(mask, s, NEG_BIG)
        m_old = m_ref[...]
        m_new = jnp.maximum(m_old, jnp.max(s, axis=1, keepdims=True))
        p = jnp.exp(s - jnp.concatenate([m_new] * (s.shape[1] // HD_C), axis=1))
        alpha = jnp.exp(m_old - m_new)
        pb = _bf(p)
        pv = jnp.concatenate(
            [jnp.dot(pb[kv * R:(kv + 1) * R], head_rows(v_list, kv), preferred_element_type=jnp.float32)
             for kv in range(KVH_C)], axis=0)
        l_ref[...] = alpha * l_ref[...] + jnp.sum(p, axis=1, keepdims=True)
        acc_ref[...] = alpha * acc_ref[...] + pv
        m_ref[...] = m_new

    @pl.when(g < n_groups)
    def _():
        process(k_refs, v_refs, lf_refs, None)

    @pl.when(g == n_groups)
    def _():
        tq = lax.broadcasted_iota(jnp.int32, (RQ, PAGE_SIZE), 0) % T_new
        key = lax.broadcasted_iota(jnp.int32, (RQ, PAGE_SIZE), 1)
        process([knew_ref], [vnew_ref], [lfnew_ref], key <= tq)
        o_ref[...] = acc_ref[...] / l_ref[...]


def fox_sample(q, k, v, logf, cache_k, cache_v, cache_lf, layer, page_table):
    B, T = q.shape[:2]
    NP = page_table.shape[1]
    G = min(FOX_PAGES_PER_STEP, NP)
    n_groups = NP // G
    n_pool = cache_k.shape[1]
    rows = PAGE_SIZE * KVH_C
    ck = cache_k.reshape(cache_k.shape[0], n_pool, rows, HD_C)
    cv = cache_v.reshape(cache_v.shape[0], n_pool, rows, HD_C)
    clf = jnp.swapaxes(cache_lf, -1, -2)
    qr = jnp.swapaxes(q, 1, 2).reshape(B, H_C * T, HD_C)
    padt = lambda x: jnp.pad(x, ((0, 0), (0, PAGE_SIZE - T), (0, 0), (0, 0))).reshape(B, rows, HD_C)
    knew, vnew = padt(k), padt(v)
    lfnew = jnp.pad(jnp.swapaxes(logf, 1, 2), ((0, 0), (0, 0), (0, PAGE_SIZE - T)))

    def page_spec(i, shape):
        return pl.BlockSpec((None, None) + shape,
                            lambda b, g, pt: (layer, pt[b, jnp.minimum(g, n_groups - 1) * G + i], 0, 0))

    per_b = lambda shape: pl.BlockSpec((None,) + shape, lambda b, g, pt: (b, 0, 0))
    in_specs = ([per_b((H_C * T, HD_C))]
                + [page_spec(i, (rows, HD_C)) for i in range(G)]
                + [page_spec(i, (rows, HD_C)) for i in range(G)]
                + [page_spec(i, (H_C, PAGE_SIZE)) for i in range(G)]
                + [per_b((rows, HD_C)), per_b((rows, HD_C)), per_b((H_C, PAGE_SIZE))])
    RQ = H_C * T
    o = pl.pallas_call(
        functools.partial(_fox_sample_body, G=G, T_new=T),
        grid_spec=pltpu.PrefetchScalarGridSpec(
            num_scalar_prefetch=1,
            grid=(B, n_groups + 1),
            in_specs=in_specs,
            out_specs=per_b((RQ, HD_C)),
            scratch_shapes=[pltpu.VMEM((RQ, HD_C), jnp.float32), pltpu.VMEM((RQ, HD_C), jnp.float32),
                            pltpu.VMEM((RQ, HD_C), jnp.float32), pltpu.VMEM((H_C, 1), jnp.float32)]),
        out_shape=jax.ShapeDtypeStruct((B, RQ, HD_C), jnp.float32),
        compiler_params=pltpu.CompilerParams(dimension_semantics=("parallel", "arbitrary"),
                                             vmem_limit_bytes=VMEM_LIMIT_BYTES),
        name="fox_sample_paged_attention",
    )(page_table, qr, *([ck] * G), *([cv] * G), *([clf] * G), knew, vnew, lfnew)
    return jnp.swapaxes(o.reshape(B, H_C, T, HD_C), 1, 2).reshape(B, T, H_C * HD_C)


def _fox_cumsum_body(lf_ref, f_ref, *, blk):
    T = lf_ref.shape[-1]
    tri = _upper_tri(blk)
    carry = jnp.zeros((lf_ref.shape[0], 1), jnp.float32)
    for j in range(T // blk):
        sl = slice(j * blk, (j + 1) * blk)
        Fp = _dot3(_split3(lf_ref[:, sl]), tri) + carry
        f_ref[:, sl] = Fp
        carry = Fp[:, blk - 1:blk]


def _fox_prompt_body(q_ref, kv_ref, f_ref, o_ref, qs_ref, m_ref, l_ref, acc_ref, *, tq):
    qi = pl.program_id(1)
    kv = pl.program_id(2)
    head_cols = lambda g: pl.ds(pl.multiple_of((kv * G_C + g) * HD_C, HD_C), HD_C)
    k_cols = pl.ds(pl.multiple_of(kv * HD_C, HD_C), HD_C)
    v_cols = pl.ds(pl.multiple_of(KV_C + kv * HD_C, HD_C), HD_C)
    for g in range(G_C):
        qs_ref[g * tq:(g + 1) * tq, :] = _bf(q_ref[:, head_cols(g)] * (HD_C ** -0.5))
    m_ref[...] = jnp.full_like(m_ref, NEG_BIG)
    l_ref[...] = jnp.zeros_like(l_ref)
    acc_ref[...] = jnp.zeros_like(acc_ref)
    qpos = qi * tq + lax.broadcasted_iota(jnp.int32, (G_C * tq, tq), 0) % tq
    kcol = lax.broadcasted_iota(jnp.int32, (G_C * tq, tq), 1)

    def body(j, carry):
        ks = pl.multiple_of(j * tq, tq)
        K = _bf(kv_ref[pl.ds(ks, tq), k_cols])
        V = _bf(kv_ref[pl.ds(ks, tq), v_cols])
        s = lax.dot_general(qs_ref[...], K, (((1,), (1,)), ((), ())), preferred_element_type=jnp.float32)
        F = f_ref[:, pl.ds(ks, tq)]
        bias = jnp.concatenate([jnp.broadcast_to(F[g:g + 1, :], (tq, tq)) for g in range(G_C)], axis=0)
        s = jnp.where(kcol + ks <= qpos, s - bias, NEG_BIG)
        _online_softmax_step(s, V, m_ref, l_ref, acc_ref, slice(None))
        return carry

    lax.fori_loop(0, qi + 1, body, 0)
    o = acc_ref[...] / l_ref[...]
    for g in range(G_C):
        o_ref[:, head_cols(g)] = o[g * tq:(g + 1) * tq, :]


def fox_prompt(q, kv, logf):
    B, T = q.shape[:2]
    tq = min(FOX_Q_TILE, T)
    lfT = jnp.swapaxes(logf, 1, 2)
    F = pl.pallas_call(
        functools.partial(_fox_cumsum_body, blk=tq),
        grid=(B,),
        in_specs=[pl.BlockSpec((None, H_C, T), lambda b: (b, 0, 0))],
        out_specs=pl.BlockSpec((None, H_C, T), lambda b: (b, 0, 0)),
        out_shape=jax.ShapeDtypeStruct((B, H_C, T), jnp.float32),
        name="fox_logf_cumsum",
    )(lfT)
    o = pl.pallas_call(
        functools.partial(_fox_prompt_body, tq=tq),
        grid=(B, T // tq, KVH_C),
        in_specs=[pl.BlockSpec((None, tq, C_C), lambda b, qi, kv: (b, qi, 0)),
                  pl.BlockSpec((None, T, 2 * KV_C), lambda b, qi, kv: (b, 0, 0)),
                  pl.BlockSpec((None, None, G_C, T), lambda b, qi, kv: (b, kv, 0, 0))],
        out_specs=pl.BlockSpec((None, tq, C_C), lambda b, qi, kv: (b, qi, 0)),
        out_shape=jax.ShapeDtypeStruct((B, T, C_C), jnp.float32),
        scratch_shapes=[pltpu.VMEM((G_C * tq, HD_C), jnp.bfloat16),
                        pltpu.VMEM((G_C * tq, HD_C), jnp.float32), pltpu.VMEM((G_C * tq, HD_C), jnp.float32),
                        pltpu.VMEM((G_C * tq, HD_C), jnp.float32)],
        compiler_params=pltpu.CompilerParams(dimension_semantics=("parallel", "parallel", "arbitrary"),
                                             vmem_limit_bytes=VMEM_LIMIT_LARGE),
        name="fox_prompt_flash_attention",
    )(q, kv, F.reshape(B, KVH_C, G_C, T))
    return o


MOE_ROWS = 256
MOE_COMBINE_TOKENS = 64


def _moe_up_body(blk_exp_ref, nused_ref, row_tok_ref, roww_ref, x_hbm, wg_ref, wu_ref, act_ref,
                 xbuf, wg_bf, wu_bf, sem, *, RB):
    b = pl.program_id(0)
    nused = nused_ref[0]
    slot = b % 2

    def gather(blk, slot_):
        def body(i, carry):
            t = row_tok_ref[blk * RB + i]
            pltpu.make_async_copy(x_hbm.at[pl.ds(t, 1)], xbuf.at[slot_, pl.ds(i, 1)], sem.at[slot_]).start()
            return carry
        lax.fori_loop(0, RB, body, 0, unroll=8)

    @pl.when(b == 0)
    def _():
        gather(0, 0)

    @pl.when(b + 1 < nused)
    def _():
        gather(b + 1, 1 - slot)

    @pl.when(b < nused)
    def _():
        pltpu.make_async_copy(x_hbm.at[pl.ds(0, RB)], xbuf.at[slot], sem.at[slot]).wait()
        e = blk_exp_ref[b]
        e_prev = blk_exp_ref[jnp.maximum(b - 1, 0)]

        @pl.when((b == 0) | (e != e_prev))
        def _():
            wg_bf[...] = wg_ref[...].astype(jnp.bfloat16)
            wu_bf[...] = wu_ref[...].astype(jnp.bfloat16)

        x = xbuf[slot].astype(jnp.bfloat16)
        g = jnp.dot(x, wg_bf[...], preferred_element_type=jnp.float32)
        u = jnp.dot(x, wu_bf[...], preferred_element_type=jnp.float32)
        act_ref[...] = (g * jax.nn.sigmoid(g)) * u * roww_ref[...]

    @pl.when(b >= nused)
    def _():
        act_ref[...] = jnp.zeros_like(act_ref)


def _moe_down_body(blk_exp_ref, nused_ref, act_ref, wd_ref, y_ref, wd_bf):
    b = pl.program_id(0)
    nused = nused_ref[0]

    @pl.when(b < nused)
    def _():
        e = blk_exp_ref[b]
        e_prev = blk_exp_ref[jnp.maximum(b - 1, 0)]

        @pl.when((b == 0) | (e != e_prev))
        def _():
            wd_bf[...] = wd_ref[...].astype(jnp.bfloat16)

        y_ref[...] = jnp.dot(act_ref[...].astype(jnp.bfloat16), wd_bf[...], preferred_element_type=jnp.float32)

    @pl.when(b >= nused)
    def _():
        y_ref[...] = jnp.zeros_like(y_ref)


def _moe_combine_body(pos_ref, y_hbm, shared_ref, out_ref, buf, sem, *, TB):
    b = pl.program_id(0)
    nb = pl.num_programs(0)
    slot = b % 2

    def gather(blk, slot_):
        def body(i, carry):
            for kk in range(TOP_K):
                p = pos_ref[(blk * TB + i) * TOP_K + kk]
                pltpu.make_async_copy(y_hbm.at[pl.ds(p, 1)], buf.at[slot_, pl.ds(kk * TB + i, 1)],
                                      sem.at[slot_]).start()
            return carry
        lax.fori_loop(0, TB, body, 0, unroll=2)

    @pl.when(b == 0)
    def _():
        gather(0, 0)

    @pl.when(b + 1 < nb)
    def _():
        gather(b + 1, 1 - slot)

    pltpu.make_async_copy(y_hbm.at[pl.ds(0, TOP_K * TB)], buf.at[slot], sem.at[slot]).wait()
    acc = shared_ref[...]
    for kk in range(TOP_K):
        acc = acc + buf[slot, kk * TB:(kk + 1) * TB]
    out_ref[...] = acc


def _moe_invert_body(pos_ref, src_ref):
    def clear(i, carry):
        src_ref[i] = -1
        return carry
    lax.fori_loop(0, src_ref.shape[0], clear, 0, unroll=8)

    def put(i, carry):
        src_ref[pos_ref[i]] = i
        return carry
    lax.fori_loop(0, pos_ref.shape[0], put, 0, unroll=8)


ROUTE_TOKENS = 256
NEG_INF = float("-inf")


def _first_max(x, idx, axis, n):
    m = jnp.max(x, axis=axis, keepdims=True)
    first = jnp.min(jnp.where(x == m, idx, n), axis=axis, keepdims=True)
    return m, first


def _moe_router_body(wt_ref, h_ref, bias_ref, eidx_ref, gate_ref):
    tn = h_ref.shape[0]
    per_group = N_EXPERTS // N_GROUPS
    logits = lax.dot_general(_bf(wt_ref[...]), _bf(h_ref[...]), (((1,), (1,)), ((), ())),
                             preferred_element_type=jnp.float32)
    scores = jax.nn.sigmoid(logits)
    biased = scores + bias_ref[:, :1]
    b3 = biased.reshape(N_GROUPS, per_group, tn)
    i3 = lax.broadcasted_iota(jnp.int32, b3.shape, 1)
    m1, f1 = _first_max(b3, i3, 1, per_group)
    m2 = jnp.max(jnp.where(i3 == f1, NEG_INF, b3), axis=1, keepdims=True)
    grp = (m1 + m2).reshape(N_GROUPS, tn)
    gi = lax.broadcasted_iota(jnp.int32, grp.shape, 0)
    keep = jnp.zeros(grp.shape, jnp.float32)
    for _ in range(TOPK_GROUPS):
        _, f = _first_max(grp, gi, 0, N_GROUPS)
        keep = jnp.where(gi == f, 1.0, keep)
        grp = jnp.where(gi == f, NEG_INF, grp)
    cand = jnp.where(keep.reshape(N_GROUPS, 1, tn) > 0.5, b3, NEG_INF).reshape(N_EXPERTS, tn)
    ei = lax.broadcasted_iota(jnp.int32, cand.shape, 0)
    ids, gs = [], []
    for _ in range(TOP_K):
        _, f = _first_max(cand, ei, 0, N_EXPERTS)
        ids.append(f)
        gs.append(jnp.sum(jnp.where(ei == f, scores, 0.0), axis=0, keepdims=True))
        cand = jnp.where(ei == f, NEG_INF, cand)
    tot = gs[0]
    for x in gs[1:]:
        tot = tot + x
    pad = eidx_ref.shape[0] - TOP_K
    eidx_ref[...] = jnp.concatenate(ids + [jnp.zeros((pad, tn), jnp.int32)], axis=0)
    gate_ref[...] = jnp.concatenate([x / tot * ROUTED_SCALE for x in gs] + [jnp.zeros((pad, tn), jnp.float32)],
                                    axis=0)


def moe_route(h, w_router, b_router):
    T, D = h.shape
    tn = ROUTE_TOKENS
    rows = 8
    bias = jnp.broadcast_to(b_router[:, None], (N_EXPERTS, 128))
    eidx, gates = pl.pallas_call(
        _moe_router_body,
        grid=(T // tn,),
        in_specs=[pl.BlockSpec((N_EXPERTS, D), lambda i: (0, 0)),
                  pl.BlockSpec((tn, D), lambda i: (i, 0)),
                  pl.BlockSpec((N_EXPERTS, 128), lambda i: (0, 0))],
        out_specs=[pl.BlockSpec((rows, tn), lambda i: (0, i)), pl.BlockSpec((rows, tn), lambda i: (0, i))],
        out_shape=[jax.ShapeDtypeStruct((rows, T), jnp.int32), jax.ShapeDtypeStruct((rows, T), jnp.float32)],
        compiler_params=pltpu.CompilerParams(dimension_semantics=("parallel",)),
        name="moe_router_topk",
    )(w_router.T, h, bias)
    return eidx[:TOP_K].T, gates[:TOP_K].T


def moe_layout(eidx, gates, RB):
    T = eidx.shape[0]
    A = T * TOP_K
    E = N_EXPERTS
    e_flat = eidx.reshape(A).astype(jnp.int32)
    g_flat = gates.reshape(A)
    onehot = (e_flat[:, None] == jnp.arange(E, dtype=jnp.int32)[None, :]).astype(jnp.int32)
    rank_incl = jnp.cumsum(onehot, axis=0)
    rank = jnp.sum(onehot * rank_incl, axis=1) - 1
    counts = rank_incl[-1]
    padded = (counts + RB - 1) // RB * RB
    seg_start = jnp.cumsum(counts) - counts
    pad_end = jnp.cumsum(padded)
    pad_start = pad_end - padded
    pos = pad_start[e_flat] + rank
    n_blocks = -(-(A + E * (RB - 1)) // RB)
    n_rows = n_blocks * RB
    blk_exp = jnp.minimum(jnp.searchsorted(pad_end, jnp.arange(n_blocks, dtype=jnp.int32) * RB, side='right'),
                          E - 1).astype(jnp.int32)
    nused = (pad_end[-1] // RB).astype(jnp.int32).reshape(1)
    pos = pos.astype(jnp.int32)
    src = pl.pallas_call(
        _moe_invert_body,
        grid_spec=pltpu.PrefetchScalarGridSpec(
            num_scalar_prefetch=1, grid=(1,), in_specs=[],
            out_specs=pl.BlockSpec(memory_space=pltpu.SMEM)),
        out_shape=jax.ShapeDtypeStruct((n_rows,), jnp.int32),
        name="moe_invert_positions",
    )(pos)
    valid = src >= 0
    row_tok = jnp.where(valid, src // TOP_K, 0).astype(jnp.int32)
    row_w = jnp.where(valid, g_flat[jnp.maximum(src, 0)], 0.0).astype(jnp.float32)
    return pos, row_tok, row_w, blk_exp, nused, n_blocks


def moe_routed_plus_shared(h, layer, eidx, gates, w_gate, w_up, w_down, shared):
    T, D = h.shape
    F = w_gate.shape[-1]
    RB = MOE_ROWS
    TB = MOE_COMBINE_TOKENS
    pos, row_tok, row_w, blk_exp, nused, n_blocks = moe_layout(eidx, gates, RB)
    n_rows = n_blocks * RB
    wspec_up = pl.BlockSpec((None, None, D, F), lambda b, be, nu, rt: (layer, be[b], 0, 0))
    act = pl.pallas_call(
        functools.partial(_moe_up_body, RB=RB),
        grid_spec=pltpu.PrefetchScalarGridSpec(
            num_scalar_prefetch=3,
            grid=(n_blocks,),
            in_specs=[pl.BlockSpec((RB, 1), lambda b, be, nu, rt: (b, 0)),
                      pl.BlockSpec(memory_space=pl.ANY),
                      wspec_up, wspec_up],
            out_specs=pl.BlockSpec((RB, F), lambda b, be, nu, rt: (b, 0)),
            scratch_shapes=[pltpu.VMEM((2, RB, D), jnp.float32),
                            pltpu.VMEM((D, F), jnp.bfloat16),
                            pltpu.VMEM((D, F), jnp.bfloat16),
                            pltpu.SemaphoreType.DMA((2,))]),
        out_shape=jax.ShapeDtypeStruct((n_rows, F), jnp.float32),
        compiler_params=pltpu.CompilerParams(dimension_semantics=("arbitrary",),
                                             vmem_limit_bytes=VMEM_LIMIT_LARGE),
        name="moe_gate_up",
    )(blk_exp, nused, row_tok, row_w.reshape(n_rows, 1), h, w_gate, w_up)
    y = pl.pallas_call(
        _moe_down_body,
        grid_spec=pltpu.PrefetchScalarGridSpec(
            num_scalar_prefetch=2,
            grid=(n_blocks,),
            in_specs=[pl.BlockSpec((RB, F), lambda b, be, nu: (b, 0)),
                      pl.BlockSpec((None, None, F, D), lambda b, be, nu: (layer, be[b], 0, 0))],
            out_specs=pl.BlockSpec((RB, D), lambda b, be, nu: (b, 0)),
            scratch_shapes=[pltpu.VMEM((F, D), jnp.bfloat16)]),
        out_shape=jax.ShapeDtypeStruct((n_rows, D), jnp.float32),
        compiler_params=pltpu.CompilerParams(dimension_semantics=("arbitrary",),
                                             vmem_limit_bytes=VMEM_LIMIT_LARGE),
        name="moe_down",
    )(blk_exp, nused, act, w_down)
    return pl.pallas_call(
        functools.partial(_moe_combine_body, TB=TB),
        grid_spec=pltpu.PrefetchScalarGridSpec(
            num_scalar_prefetch=1,
            grid=(T // TB,),
            in_specs=[pl.BlockSpec(memory_space=pl.ANY),
                      pl.BlockSpec((TB, D), lambda b, ps: (b, 0))],
            out_specs=pl.BlockSpec((TB, D), lambda b, ps: (b, 0)),
            scratch_shapes=[pltpu.VMEM((2, TOP_K * TB, D), jnp.float32),
                            pltpu.SemaphoreType.DMA((2,))]),
        out_shape=jax.ShapeDtypeStruct((T, D), jnp.float32),
        compiler_params=pltpu.CompilerParams(dimension_semantics=("arbitrary",),
                                             vmem_limit_bytes=VMEM_LIMIT_LARGE),
        name="moe_combine",
    )(pos, y, shared)


def moe_ffn(h, layer, w_router, b_router, w_gate, w_up, w_down, ws_gate, ws_up, ws_down):
    eidx, gates = moe_route(h, w_router[layer], b_router[layer])
    shared = matmul(jax.nn.silu(matmul(h, ws_gate[layer])) * matmul(h, ws_up[layer]), ws_down[layer])
    return moe_routed_plus_shared(h, layer, eidx, gates, w_gate, w_up, w_down, shared)


def kernel(x_prompt, x_sample, c_prompt, c_sample, state_rwkv, state_rwkv_shift, cache_swa_k, cache_swa_v, cache_fox_k, cache_fox_v, cache_fox_logf, page_table, ada_w, ada_b, norm_w, final_norm_w, ab_w_in, ab_w_out, rwkv_mu, rwkv_w0, rwkv_w_decay_up, rwkv_a0, rwkv_w_aaa_up, rwkv_w_gate_up, rwkv_k_k, rwkv_k_a, rwkv_r_k, rwkv_lnx_w, rwkv_lnx_b, swa_sinks, fox_w_in, fox_b_f, fox_w_out, moe_w_router, moe_b_router, moe_w_gate, moe_w_up, moe_w_down, shared_w_gate, shared_w_up, shared_w_down):
    Bp, Tp, D = x_prompt.shape
    Bs, Ts, _ = x_sample.shape
    depth = ada_w.shape[0]
    past_len = page_table.shape[1] * PAGE_SIZE
    pos_p = jnp.arange(Tp)
    pos_s = past_len + jnp.arange(Ts)
    xp, xs = x_prompt, x_sample
    rw_S_p, rw_sh_p, sw_k_p, sw_v_p, fk_p, fv_p, flf_p = [], [], [], [], [], [], []
    rw_S_s, rw_sh_s, sw_k_s, sw_v_s, fk_s, fv_s, flf_s = [], [], [], [], [], [], []
    for l in range(depth):
        mods = ada_params(jnp.concatenate([c_prompt, c_sample], axis=0), ada_w, l, ada_b[l])
        sh1_p, sc1_p, g1_p, sh2_p, sc2_p, g2_p = [m[:Bp] for m in mods]
        sh1_s, sc1_s, g1_s, sh2_s, sc2_s, g2_s = [m[Bp:] for m in mods]
        if l == 0:
            hp = rmsnorm(xp, norm_w[l, 0], sc1_p, sh1_p, jnp.bfloat16)
            hs = rmsnorm(xs, norm_w[l, 0], sc1_s, sh1_s, jnp.bfloat16)
        else:
            xp, hp = rmsnorm(xp, norm_w[l, 0], sc1_p, sh1_p, jnp.bfloat16, y, gy_p, 0, True)
            xs, hs = rmsnorm(xs, norm_w[l, 0], sc1_s, sh1_s, jnp.bfloat16, y, gy_s, Bp * Tp, True)
        if l % 2 == 0:
            i = l // 2
            rw = (rwkv_mu[i], rwkv_w0[i], rwkv_w_decay_up[i], rwkv_a0[i], rwkv_w_aaa_up[i], rwkv_w_gate_up[i],
                  rwkv_k_k[i], rwkv_k_a[i], rwkv_r_k[i], rwkv_lnx_w[i], rwkv_lnx_b[i])
            o_p, S_p, row_p, kw_p, vw_p = ab_mixer(
                hp, pos_p, jnp.zeros((Bp, P_A), jnp.float32), jnp.zeros((Bp, H_A, HD_A, HD_A), jnp.float32),
                None, None, ab_w_in[i], ab_w_out[i], rw, swa_sinks[i])
            o_s, S_s, row_s, kw_s, vw_s = ab_mixer(
                hs, pos_s, state_rwkv_shift[i], state_rwkv[i], cache_swa_k[i], cache_swa_v[i],
                ab_w_in[i], ab_w_out[i], rw, swa_sinks[i])
            rw_S_p.append(S_p); rw_sh_p.append(row_p); sw_k_p.append(kw_p); sw_v_p.append(vw_p)
            rw_S_s.append(S_s); rw_sh_s.append(row_s); sw_k_s.append(kw_s); sw_v_s.append(vw_s)
        else:
            j = l // 2
            q_p, kv_p, lf_p = fox_project(hp, fox_w_in[j], fox_b_f[j])
            q_s, kv_s, lf_s = fox_project(hs, fox_w_in[j], fox_b_f[j])
            k_p, v_p = (kv_p[..., c:c + KV_C].reshape(Bp, Tp, KVH_C, HD_C) for c in (0, KV_C))
            k_s, v_s = (kv_s[..., c:c + KV_C].reshape(Bs, Ts, KVH_C, HD_C) for c in (0, KV_C))
            o_p = mm3(fox_prompt(q_p, kv_p, lf_p), fox_w_out[j])
            o_s = mm3(fox_sample(q_s.reshape(Bs, Ts, H_C, HD_C), k_s, v_s, lf_s,
                                 cache_fox_k, cache_fox_v, cache_fox_logf, j, page_table),
                      fox_w_out[j])
            fk_p.append(k_p); fv_p.append(v_p); flf_p.append(lf_p)
            fk_s.append(k_s); fv_s.append(v_s); flf_s.append(lf_s)
        xp, h2p = rmsnorm(xp, norm_w[l, 1], sc2_p, sh2_p, jnp.float32, o_p.reshape(Bp * Tp, D), g1_p, 0, True)
        xs, h2s = rmsnorm(xs, norm_w[l, 1], sc2_s, sh2_s, jnp.float32, o_s.reshape(Bs * Ts, D), g1_s, 0, True)
        tok = jnp.concatenate([h2p.reshape(Bp * Tp, D), h2s.reshape(Bs * Ts, D)], axis=0)
        y = moe_ffn(tok, l, moe_w_router, moe_b_router, moe_w_gate, moe_w_up, moe_w_down,
                    shared_w_gate, shared_w_up, shared_w_down)
        gy_p, gy_s = g2_p, g2_s
    y_prompt = rmsnorm(xp, final_norm_w, resid=y, gate=gy_p, resid_row0=0)
    y_sample = rmsnorm(xs, final_norm_w, resid=y, gate=gy_s, resid_row0=Bp * Tp)
    return (y_prompt, y_sample,
            jnp.stack(rw_S_p), jnp.stack(rw_sh_p), jnp.stack(sw_k_p), jnp.stack(sw_v_p),
            jnp.stack(fk_p), jnp.stack(fv_p), jnp.stack(flf_p),
            jnp.stack(rw_S_s), jnp.stack(rw_sh_s), jnp.stack(sw_k_s), jnp.stack(sw_v_s),
            jnp.stack(fk_s), jnp.stack(fv_s), jnp.stack(flf_s))
```

```python
import functools

import jax
import jax.numpy as jnp
from jax import lax
from jax.experimental import pallas as pl
from jax.experimental.pallas import tpu as pltpu

D_MODEL = 4096
PAGE_SIZE = 128

H_A = 32
HD_A = 64
C_A = H_A * HD_A
LORA_DECAY = 128
LORA_AAA = 128
LORA_GATE = 480
P_A = 3 * C_A + LORA_DECAY + LORA_AAA + LORA_GATE
RWKV_SPLIT = (C_A, 2 * C_A, 3 * C_A, 3 * C_A + LORA_DECAY, 3 * C_A + LORA_DECAY + LORA_AAA)
GN_EPS_A = 64e-5

H_B = 32
KVH_B = 4
G_B = H_B // KVH_B
HD_B = 64
C_B = H_B * HD_B
KV_B = KVH_B * HD_B
WINDOW = 128
ROPE_THETA = 10000.0

H_C = 32
KVH_C = 8
G_C = H_C // KVH_C
HD_C = 128
C_C = H_C * HD_C
KV_C = KVH_C * HD_C
Q_BLOCK = 128

N_EXPERTS = 64
TOP_K = 6
N_GROUPS = 8
TOPK_GROUPS = 4
ROUTED_SCALE = 2.5
MOE_BLOCK = 128

NORM_EPS = 1e-6

VMEM_LIMIT_BYTES = 48 * 1024 * 1024
VMEM_LIMIT_LARGE = 56 * 1024 * 1024


MM_VMEM_BUDGET = 40 * 1024 * 1024


def _mm_body(x_ref, w_ref, o_ref):
    o_ref[...] = jnp.dot(x_ref[...], w_ref[...].astype(jnp.bfloat16), preferred_element_type=jnp.float32)


def _pick_tile(n, candidates):
    for c in candidates:
        if n % c == 0:
            return c
    return n


def _mm_tiles(M, K, N):
    tn = _pick_tile(N, (512, 256, 128))
    for tm in (1024, 768, 512, 256, 128):
        if M % tm == 0 and 2 * (tm * K * 2 + K * tn * 4 + tm * tn * 4) <= MM_VMEM_BUDGET:
            return tm, tn
    return M, tn


def matmul(x, w, layer=None):
    M, K = x.shape
    N = w.shape[-1]
    if layer is None:
        w_spec = lambda tn: pl.BlockSpec((K, tn), lambda i, j: (0, j))
    else:
        w_spec = lambda tn: pl.BlockSpec((None, K, tn), lambda i, j: (layer, 0, j))
    tm, tn = _mm_tiles(M, K, N)
    return pl.pallas_call(
        _mm_body,
        grid=(M // tm, N // tn),
        in_specs=[pl.BlockSpec((tm, K), lambda i, j: (i, 0)), w_spec(tn)],
        out_specs=pl.BlockSpec((tm, tn), lambda i, j: (i, j)),
        out_shape=jax.ShapeDtypeStruct((M, N), jnp.float32),
        compiler_params=pltpu.CompilerParams(
            dimension_semantics=("parallel", "parallel"),
            vmem_limit_bytes=VMEM_LIMIT_BYTES),
        name="dense_matmul",
    )(x.astype(jnp.bfloat16), w)


def mm3(h, w):
    B, T, K = h.shape
    return matmul(h.reshape(B * T, K), w).reshape(B, T, w.shape[1])


HD = HD_A
PAIR = 2 * HD
RWKV_CHUNK = 64
RWKV_PAIRS_PER_STEP = 16


def _bf(x):
    return x.astype(jnp.bfloat16)


def _dot(a, b):
    return jnp.dot(_bf(a), _bf(b), preferred_element_type=jnp.float32)


def _dot_nt(a, b):
    return lax.dot_general(_bf(a), _bf(b), (((1,), (1,)), ((), ())), preferred_element_type=jnp.float32)


def _split3(x):
    h = x.astype(jnp.bfloat16)
    r1 = x - h.astype(jnp.float32)
    m = r1.astype(jnp.bfloat16)
    l = (r1 - m.astype(jnp.float32)).astype(jnp.bfloat16)
    return h, m, l


def _rwkv_body(s0_ref, r_ref, lw_ref, k_ref, v_ref, kk_ref, b_ref, y_ref, sT_ref, s_scr, *, C, hp):
    c = pl.program_id(2)

    @pl.when(c == 0)
    def _():
        s_scr[...] = s0_ref[...]

    n = 2 * C
    row = lax.broadcasted_iota(jnp.int32, (n, n), 0)
    col = lax.broadcasted_iota(jnp.int32, (n, n), 1)
    same_head = (row >= C) == (col >= C)
    tt = row & (C - 1)
    ss = col & (C - 1)
    strict = same_head & (ss < tt)
    incl = same_head & (ss <= tt)
    trow = lax.broadcasted_iota(jnp.int32, (C, C), 0)
    tcol = lax.broadcasted_iota(jnp.int32, (C, C), 1)
    tri = jnp.where(tcol <= trow, 1.0, 0.0).astype(jnp.bfloat16)
    lane = lax.broadcasted_iota(jnp.int32, (1, PAIR), 1)
    lo = jnp.where(lane < HD, 1.0, 0.0)
    hi = 1.0 - lo
    vrow = lax.broadcasted_iota(jnp.int32, (PAIR, PAIR), 0)
    vcol = lax.broadcasted_iota(jnp.int32, (PAIR, PAIR), 1)
    bd = (vrow >= HD) == (vcol >= HD)
    eye = jnp.where(row == col, 1.0, 0.0)
    nlev = C.bit_length() - 1

    P = range(hp)
    sls = [slice(p * PAIR, (p + 1) * PAIR) for p in P]
    lw = [lw_ref[:, s] for s in sls]
    r = [r_ref[:, s] for s in sls]
    k = [k_ref[:, s] for s in sls]
    v = [v_ref[:, s] for s in sls]
    kk = [kk_ref[:, s] for s in sls]
    b = [b_ref[:, s] for s in sls]
    csum = lambda x: jnp.dot(tri, x, preferred_element_type=jnp.float32)
    w3 = [_split3(x) for x in lw]
    Lc = [csum(h) + (csum(m) + csum(l)) for h, m, l in w3]
    e_inc = [jnp.exp(x) for x in Lc]
    e_neg = [jnp.exp(-x) for x in Lc]
    e_end = [jnp.exp(x[C - 1:C, :] - x) for x in Lc]
    Kq = [kk[p] * jnp.exp(Lc[p] - lw[p]) for p in P]
    Rq = [r[p] * e_inc[p] for p in P]
    Bd = [b[p] * e_neg[p] for p in P]
    Kd = [k[p] * e_neg[p] for p in P]
    G = [_dot_nt(jnp.concatenate([Kq[p] * lo, Kq[p] * hi, Rq[p] * lo, Rq[p] * hi], axis=0),
                 jnp.concatenate([Bd[p], Bd[p], Kd[p], Kd[p]], axis=0)) for p in P]
    Mb = [jnp.where(strict, g[:n, :n], 0.0) for g in G]
    Mk = [jnp.where(strict, g[:n, n:], 0.0) for g in G]
    Nb = [jnp.where(incl, g[n:, :n], 0.0) for g in G]
    Nk = [jnp.where(incl, g[n:, n:], 0.0) for g in G]
    T = [eye - jnp.where((tt >> 1) == (ss >> 1), m, 0.0) for m in Mb]
    for j in range(2, nlev + 1):
        lower_left = ((tt >> j) == (ss >> j)) & ((tt >> (j - 1)) > (ss >> (j - 1)))
        TM = [_dot(T[p], jnp.where(lower_left, Mb[p], 0.0)) for p in P]
        T = [T[p] - _dot(TM[p], T[p]) for p in P]
    V2 = [jnp.concatenate([x * lo, x * hi], axis=0) for x in v]
    MkV = [_dot(Mk[p], V2[p]) for p in P]
    NkV = [_dot(Nk[p], V2[p]) for p in P]
    BK = [jnp.concatenate([b[p] * e_end[p], k[p] * e_end[p]], axis=0) for p in P]
    S = [s_scr[p] for p in P]
    W0 = [_dot_nt(jnp.concatenate([Kq[p], Rq[p]], axis=0), S[p]) for p in P]
    rhs_u = [-(W0[p][:C] + (MkV[p][:C] + MkV[p][C:])) for p in P]
    UU = [_dot(T[p], jnp.concatenate([rhs_u[p] * lo, rhs_u[p] * hi], axis=0)) for p in P]
    YY = [_dot(Nb[p], UU[p]) + NkV[p] for p in P]
    UV = [jnp.concatenate([UU[p][:C] + UU[p][C:], v[p]], axis=0) for p in P]
    upd = [_dot(UV[p].T, BK[p]) for p in P]
    for p in P:
        y_ref[:, sls[p]] = YY[p][:C] + YY[p][C:] + W0[p][C:]
        S_new = S[p] * e_inc[p][C - 1:C, :] + jnp.where(bd, upd[p], 0.0)
        s_scr[p] = S_new
        sT_ref[p] = S_new


def rwkv7_scan(S0, r, lw, k, v, kk, b):
    B, T, CA = r.shape
    H = CA // HD
    npair = H // 2
    C = RWKV_CHUNK
    hp = min(RWKV_PAIRS_PER_STEP, npair)
    Tp = -(-T // C) * C
    if Tp != T:
        padf = lambda x: jnp.pad(x, ((0, 0), (0, Tp - T), (0, 0)))
        r, lw, k, v, kk, b = map(padf, (r, lw, k, v, kk, b))
    S0p = S0.reshape(B, npair, 2, HD, HD)
    z = jnp.zeros_like(S0p[:, :, 0])
    S0bd = jnp.concatenate([jnp.concatenate([S0p[:, :, 0], z], axis=-1),
                            jnp.concatenate([z, S0p[:, :, 1]], axis=-1)], axis=-2)
    seq_spec = pl.BlockSpec((None, C, hp * PAIR), lambda bb, g, c: (bb, c, g))
    st_spec = pl.BlockSpec((None, hp, PAIR, PAIR), lambda bb, g, c: (bb, g, 0, 0))
    y, Sbd = pl.pallas_call(
        functools.partial(_rwkv_body, C=C, hp=hp),
        grid=(B, npair // hp, Tp // C),
        in_specs=[st_spec] + [seq_spec] * 6,
        out_specs=[seq_spec, st_spec],
        out_shape=[jax.ShapeDtypeStruct((B, Tp, CA), jnp.float32),
                   jax.ShapeDtypeStruct((B, npair, PAIR, PAIR), jnp.float32)],
        scratch_shapes=[pltpu.VMEM((hp, PAIR, PAIR), jnp.float32)],
        compiler_params=pltpu.CompilerParams(
            dimension_semantics=("parallel", "parallel", "arbitrary")),
        name="rwkv7_chunk_scan",
    )(S0bd, r, lw, k, v, kk, b)
    S = jnp.stack([Sbd[:, :, :HD, :HD], Sbd[:, :, HD:, HD:]], axis=2).reshape(B, H, HD, HD)
    return y[:, :T], S


NORM_ROWS = 256


def _norm_body(*refs, modulated, residual, keep_x):
    it = iter(refs)
    x_ref, w_ref = next(it), next(it)
    sc_ref, sh_ref = (next(it), next(it)) if modulated else (None, None)
    r_ref, g_ref = (next(it), next(it)) if residual else (None, None)
    xo_ref = next(it) if keep_x else None
    o_ref = next(it)
    x = x_ref[...]
    if residual:
        x = x + g_ref[...] * r_ref[...]
    if keep_x:
        xo_ref[...] = x
    y = x * lax.rsqrt(jnp.mean(x * x, axis=-1, keepdims=True) + NORM_EPS) * w_ref[...]
    if modulated:
        y = y * (1 + sc_ref[...]) + sh_ref[...]
    o_ref[...] = y.astype(o_ref.dtype)


def rmsnorm(x, w, scale=None, shift=None, out_dtype=jnp.float32, resid=None, gate=None, resid_row0=0,
            keep_x=False):
    B, T, D = x.shape
    tt = min(NORM_ROWS, T)
    row_spec = pl.BlockSpec((None, tt, D), lambda b, t: (b, t, 0))
    mod_spec = pl.BlockSpec((None, 1, D), lambda b, t: (b, 0, 0))
    operands = [x, w.reshape(1, D)]
    in_specs = [row_spec, pl.BlockSpec((1, D), lambda b, t: (0, 0))]
    if scale is not None:
        operands += [scale, shift]
        in_specs += [mod_spec, mod_spec]
    if resid is not None:
        blk0, per_b = resid_row0 // tt, T // tt
        operands += [resid, gate]
        in_specs += [pl.BlockSpec((tt, D), lambda b, t: (blk0 + b * per_b + t, 0)), mod_spec]
    h_shape = jax.ShapeDtypeStruct((B, T, D), out_dtype)
    out = pl.pallas_call(
        functools.partial(_norm_body, modulated=scale is not None, residual=resid is not None, keep_x=keep_x),
        grid=(B, T // tt),
        in_specs=in_specs,
        out_specs=[row_spec, row_spec] if keep_x else row_spec,
        out_shape=[jax.ShapeDtypeStruct((B, T, D), jnp.float32), h_shape] if keep_x else h_shape,
        compiler_params=pltpu.CompilerParams(dimension_semantics=("parallel", "parallel")),
        name="rmsnorm_modulate",
    )(*operands)
    return tuple(out) if keep_x else out


def ada_params(c, w, layer, b):
    m = matmul(jax.nn.silu(c), w, layer) + b
    return jnp.split(m[:, None, :], 6, axis=-1)


def rwkv7_mixer(pa, prev_row, S0, mu, w0, w_dec_up, a0, w_aaa_up, w_gate_up, k_k, k_a, r_k, lnx_w, lnx_b):
    B, T, _ = pa.shape
    shifted = jnp.concatenate([prev_row[:, None, :], pa[:, :-1]], axis=1)
    m = pa + (shifted - pa) * mu
    r, k, v, wd, ad, gd = jnp.split(m, RWKV_SPLIT, axis=-1)
    w = -jax.nn.softplus(-(w0 + mm3(jnp.tanh(wd), w_dec_up))) - 0.5
    a = jax.nn.sigmoid(a0 + mm3(ad, w_aaa_up))
    g = mm3(jax.nn.sigmoid(gd), w_gate_up)
    heads = lambda t: t.reshape(B, T, H_A, HD_A)
    kk = heads(k * k_k)
    kk = kk / jnp.maximum(jnp.sqrt(jnp.sum(kk * kk, axis=-1, keepdims=True)), 1e-12)
    k = k * (1 + (a - 1) * k_a)
    rh, kh, vh = heads(r), heads(k), heads(v)
    y, S = rwkv7_scan(S0, r, -jnp.exp(w), k, v, kk.reshape(B, T, C_A), kk.reshape(B, T, C_A) * a)
    y = heads(y)
    mean = jnp.mean(y, axis=-1, keepdims=True)
    var = jnp.mean(jnp.square(y - mean), axis=-1, keepdims=True)
    y = ((y - mean) * lax.rsqrt(var + GN_EPS_A)).reshape(B, T, C_A) * lnx_w + lnx_b
    bonus = (jnp.sum(rh * kh * r_k, axis=-1, keepdims=True) * vh).reshape(B, T, C_A)
    out = (y + bonus) * g
    return out, S, pa[:, -1]


LANES = 128


def _swa_body(sink_ref, q_ref, kv_ref, cos_ref, sin_ref, kp0_ref, vp0_ref, o_ref, krot_ref, kprev, vprev, *,
              has_cache):
    n = pl.program_id(1)
    W = WINDOW

    @pl.when(n == 0)
    def _():
        kprev[...] = kp0_ref[...]
        vprev[...] = vp0_ref[...]

    cos = cos_ref[...]
    sin = sin_ref[...]
    lane = lax.broadcasted_iota(jnp.int32, (W, LANES), 1)
    first_half = (lane % HD_B) < (HD_B // 2)
    lo = lane < HD_B

    def rope(x):
        rot = jnp.where(first_half, pltpu.roll(x, LANES - HD_B // 2, 1), pltpu.roll(x, HD_B // 2, 1))
        return x * cos + rot * sin

    kcur = jnp.concatenate([rope(kv_ref[:, t * LANES:(t + 1) * LANES]) for t in range(KV_B // LANES)], axis=1)
    vcur = kv_ref[:, KV_B:2 * KV_B]
    krot_ref[...] = kcur
    K2 = jnp.concatenate([kprev[...], kcur], axis=0)
    V2 = jnp.concatenate([vprev[...], vcur], axis=0)
    lo2 = lax.broadcasted_iota(jnp.int32, (2 * W, LANES), 1) < HD_B

    def both_halves(x2, j):
        tile = x2[:, (j // 2) * LANES:(j // 2 + 1) * LANES]
        rolled = pltpu.roll(tile, HD_B, 1)
        return jnp.where(lo2, tile, rolled) if j % 2 == 0 else jnp.where(lo2, rolled, tile)

    R = G_B * W
    t_row = lax.broadcasted_iota(jnp.int32, (R, 2 * W), 0) % W
    c_col = lax.broadcasted_iota(jnp.int32, (R, 2 * W), 1)
    valid = (c_col > t_row) & (c_col <= t_row + W)
    if not has_cache:
        valid = valid & ((c_col >= W) | (n > 0))
    for j in range(KVH_B):
        Kj = _bf(both_halves(K2, j))
        Vj = _bf(both_halves(V2, j))
        rows = []
        for t in range(G_B // 2):
            c0 = (j * (G_B // 2) + t) * LANES
            x = rope(q_ref[:, c0:c0 + LANES]) * (HD_B ** -0.5)
            rows += [jnp.where(lo, x, 0.0), jnp.where(lo, 0.0, x)]
        Q = _bf(jnp.concatenate(rows, axis=0))
        s = lax.dot_general(Q, Kj, (((1,), (1,)), ((), ())), preferred_element_type=jnp.float32)
        s = jnp.where(valid, s, NEG_BIG)
        sk = jnp.concatenate([jnp.full((W, LANES), sink_ref[j * G_B + i], jnp.float32) for i in range(G_B)], axis=0)
        m = jnp.maximum(jnp.max(s, axis=1, keepdims=True), sk)
        p = jnp.exp(s - jnp.concatenate([m, m], axis=1))
        den = jnp.sum(p, axis=1, keepdims=True) + jnp.exp(sk - m)
        o = jnp.dot(_bf(p), Vj, preferred_element_type=jnp.float32) / den
        for t in range(G_B // 2):
            c0 = (j * (G_B // 2) + t) * LANES
            o_ref[:, c0:c0 + LANES] = jnp.where(lo, o[(2 * t) * W:(2 * t + 1) * W], o[(2 * t + 1) * W:(2 * t + 2) * W])
    kprev[...] = kcur
    vprev[...] = vcur


def swa_attention(q, kv, pos, kbuf, vbuf, sinks):
    B, T, _ = q.shape
    W = WINDOW
    Tp = -(-T // W) * W
    if Tp != T:
        q = jnp.pad(q, ((0, 0), (0, Tp - T), (0, 0)))
        kv = jnp.pad(kv, ((0, 0), (0, Tp - T), (0, 0)))
        pos = jnp.pad(pos, (0, Tp - T))
    half = HD_B // 2
    inv = ROPE_THETA ** (-jnp.arange(half, dtype=jnp.float32) / half)
    ang = pos.astype(jnp.float32)[:, None] * inv[None, :]
    cos = jnp.tile(jnp.cos(ang), (1, LANES // half))
    sin = jnp.tile(jnp.concatenate([-jnp.sin(ang), jnp.sin(ang)], axis=1), (1, LANES // HD_B))
    has_cache = kbuf is not None
    if has_cache:
        kp0, vp0 = kbuf.reshape(B, W, KV_B), vbuf.reshape(B, W, KV_B)
    else:
        kp0 = vp0 = jnp.zeros((B, W, KV_B), jnp.float32)
    blk = lambda c: pl.BlockSpec((None, W, c), lambda b, n, s: (b, n, 0))
    tab = pl.BlockSpec((W, LANES), lambda b, n, s: (n, 0))
    per_b = pl.BlockSpec((None, W, KV_B), lambda b, n, s: (b, 0, 0))
    o, krot = pl.pallas_call(
        functools.partial(_swa_body, has_cache=has_cache),
        grid_spec=pltpu.PrefetchScalarGridSpec(
            num_scalar_prefetch=1,
            grid=(B, Tp // W),
            in_specs=[blk(C_B), blk(2 * KV_B), tab, tab, per_b, per_b],
            out_specs=[blk(C_B), blk(KV_B)],
            scratch_shapes=[pltpu.VMEM((W, KV_B), jnp.float32), pltpu.VMEM((W, KV_B), jnp.float32)]),
        out_shape=[jax.ShapeDtypeStruct((B, Tp, C_B), jnp.float32), jax.ShapeDtypeStruct((B, Tp, KV_B), jnp.float32)],
        compiler_params=pltpu.CompilerParams(dimension_semantics=("parallel", "arbitrary")),
        name="swa_sink_attention",
    )(sinks, q, kv, cos, sin, kp0, vp0)
    return o[:, :T], krot[:, :T]


def ab_mixer(h, pos, prev_row, S0, kbuf, vbuf, w_in, w_out, rwkv_params, sinks):
    B, T, _ = h.shape
    h2 = h.reshape(B * T, -1)
    w_parts = (w_in[:, :3 * C_A], w_in[:, 3 * C_A:P_A], w_in[:, P_A:P_A + C_B], w_in[:, P_A + C_B:])
    pa = jnp.concatenate([matmul(h2, w_parts[0]), matmul(h2, w_parts[1])], axis=-1).reshape(B, T, P_A)
    q = matmul(h2, w_parts[2]).reshape(B, T, C_B)
    kv = matmul(h2, w_parts[3]).reshape(B, T, 2 * KV_B)
    o_a, S, last_row = rwkv7_mixer(pa, prev_row, S0, *rwkv_params)
    o_b, k_rot = swa_attention(q, kv, pos, kbuf, vbuf, sinks)
    k = k_rot.reshape(B, T, KVH_B, HD_B)
    v = kv[..., KV_B:].reshape(B, T, KVH_B, HD_B)
    if kbuf is not None:
        k = jnp.concatenate([kbuf, k], axis=1)
        v = jnp.concatenate([vbuf, v], axis=1)
    out = mm3(jnp.concatenate([o_a, o_b], axis=-1), w_out)
    return out, S, last_row, k[:, -WINDOW:], v[:, -WINDOW:]


def fox_project(h, w_in, b_f):
    B, T, _ = h.shape
    h2 = h.reshape(B * T, -1)
    q = matmul(h2, w_in[:, :C_C])
    kv = matmul(h2, w_in[:, C_C:C_C + 2 * KV_C])
    fl = matmul(h2, w_in[:, C_C + 2 * KV_C:])
    logf = jax.nn.log_sigmoid(fl + b_f).reshape(B, T, H_C)
    return q.reshape(B, T, C_C), kv.reshape(B, T, 2 * KV_C), logf


NEG_BIG = -1e30
FOX_PAGES_PER_STEP = 8
FOX_Q_TILE = 256


def _dot3(parts, w):
    f = lambda x: jnp.dot(x, w, preferred_element_type=jnp.float32)
    return f(parts[0]) + (f(parts[1]) + f(parts[2]))


def _dot3_left(w, parts):
    f = lambda x: jnp.dot(w, x, preferred_element_type=jnp.float32)
    return f(parts[0]) + (f(parts[1]) + f(parts[2]))


def _upper_tri(n):
    r = lax.broadcasted_iota(jnp.int32, (n, n), 0)
    c = lax.broadcasted_iota(jnp.int32, (n, n), 1)
    return jnp.where(r <= c, 1.0, 0.0).astype(jnp.bfloat16)


def _online_softmax_step(s, V, m_ref, l_ref, acc_ref, rows):
    m_old = m_ref[rows]
    m_new = jnp.maximum(m_old, jnp.max(s, axis=1, keepdims=True))
    p = jnp.exp(s - jnp.concatenate([m_new] * (s.shape[1] // HD_C), axis=1))
    alpha = jnp.exp(m_old - m_new)
    l_ref[rows] = alpha * l_ref[rows] + jnp.sum(p, axis=1, keepdims=True)
    acc_ref[rows] = alpha * acc_ref[rows] + jnp.dot(_bf(p), V, preferred_element_type=jnp.float32)
    m_ref[rows] = m_new


def _fox_sample_body(pt_ref, q_ref, *rest, G, T_new):
    k_refs = rest[:G]
    v_refs = rest[G:2 * G]
    lf_refs = rest[2 * G:3 * G]
    knew_ref, vnew_ref, lfnew_ref, o_ref, m_ref, l_ref, acc_ref, carry_ref = rest[3 * G:]
    g = pl.program_id(1)
    n_groups = pl.num_programs(1) - 1
    R = G_C * T_new
    RQ = KVH_C * R

    @pl.when(g == 0)
    def _():
        m_ref[...] = jnp.full_like(m_ref, NEG_BIG)
        l_ref[...] = jnp.zeros_like(l_ref)
        acc_ref[...] = jnp.zeros_like(acc_ref)
        carry_ref[...] = jnp.zeros_like(carry_ref)

    q = _bf(q_ref[...] * (HD_C ** -0.5))
    tri = _upper_tri(PAGE_SIZE)
    er = lax.broadcasted_iota(jnp.int32, (RQ, H_C), 0)
    ec = lax.broadcasted_iota(jnp.int32, (RQ, H_C), 1)
    expand = jnp.where(er // T_new == ec, 1.0, 0.0).astype(jnp.bfloat16)

    def process(k_list, v_list, lf_list, mask):
        carry = carry_ref[...]
        Fs = []
        for lf_r in lf_list:
            Fp = _dot3(_split3(lf_r[...]), tri) + carry
            carry = Fp[:, PAGE_SIZE - 1:PAGE_SIZE]
            Fs.append(Fp)
        carry_ref[...] = carry
        F_all = Fs[0] if len(Fs) == 1 else jnp.concatenate(Fs, axis=1)
        bias = _dot3_left(expand, _split3(F_all))
        def head_rows(refs, kv):
            parts = [r[pl.ds(kv, PAGE_SIZE, stride=KVH_C), :] for r in refs]
            return _bf(parts[0] if len(parts) == 1 else jnp.concatenate(parts, axis=0))

        s = jnp.concatenate(
            [lax.dot_general(q[kv * R:(kv + 1) * R], head_rows(k_list, kv), (((1,), (1,)), ((), ())),
                             preferred_element_type=jnp.float32) for kv in range(KVH_C)], axis=0) - bias
        if mask is not None:
            s = jnp.where(mask, s, NEG_BIG)
        m_old = m_ref[...]
        m_new = jnp.maximum(m_old, jnp.max(s, axis=1, keepdims=True))
        p = jnp.exp(s - jnp.concatenate([m_new] * (s.shape[1] // HD_C), axis=1))
        alpha = jnp.exp(m_old - m_new)
        pb = _bf(p)
        pv = jnp.concatenate(
            [jnp.dot(pb[kv * R:(kv + 1) * R], head_rows(v_list, kv), preferred_element_type=jnp.float32)
             for kv in range(KVH_C)], axis=0)
        l_ref[...] = alpha * l_ref[...] + jnp.sum(p, axis=1, keepdims=True)
        acc_ref[...] = alpha * acc_ref[...] + pv
        m_ref[...] = m_new

    @pl.when(g < n_groups)
    def _():
        process(k_refs, v_refs, lf_refs, None)

    @pl.when(g == n_groups)
    def _():
        tq = lax.broadcasted_iota(jnp.int32, (RQ, PAGE_SIZE), 0) % T_new
        key = lax.broadcasted_iota(jnp.int32, (RQ, PAGE_SIZE), 1)
        process([knew_ref], [vnew_ref], [lfnew_ref], key <= tq)
        o_ref[...] = acc_ref[...] / l_ref[...]


def fox_sample(q, k, v, logf, cache_k, cache_v, cache_lf, layer, page_table):
    B, T = q.shape[:2]
    NP = page_table.shape[1]
    G = min(FOX_PAGES_PER_STEP, NP)
    n_groups = NP // G
    n_pool = cache_k.shape[1]
    rows = PAGE_SIZE * KVH_C
    ck = cache_k.reshape(cache_k.shape[0], n_pool, rows, HD_C)
    cv = cache_v.reshape(cache_v.shape[0], n_pool, rows, HD_C)
    clf = jnp.swapaxes(cache_lf, -1, -2)
    qr = jnp.swapaxes(q, 1, 2).reshape(B, H_C * T, HD_C)
    padt = lambda x: jnp.pad(x, ((0, 0), (0, PAGE_SIZE - T), (0, 0), (0, 0))).reshape(B, rows, HD_C)
    knew, vnew = padt(k), padt(v)
    lfnew = jnp.pad(jnp.swapaxes(logf, 1, 2), ((0, 0), (0, 0), (0, PAGE_SIZE - T)))

    def page_spec(i, shape):
        return pl.BlockSpec((None, None) + shape,
                            lambda b, g, pt: (layer, pt[b, jnp.minimum(g, n_groups - 1) * G + i], 0, 0))

    per_b = lambda shape: pl.BlockSpec((None,) + shape, lambda b, g, pt: (b, 0, 0))
    in_specs = ([per_b((H_C * T, HD_C))]
                + [page_spec(i, (rows, HD_C)) for i in range(G)]
                + [page_spec(i, (rows, HD_C)) for i in range(G)]
                + [page_spec(i, (H_C, PAGE_SIZE)) for i in range(G)]
                + [per_b((rows, HD_C)), per_b((rows, HD_C)), per_b((H_C, PAGE_SIZE))])
    RQ = H_C * T
    o = pl.pallas_call(
        functools.partial(_fox_sample_body, G=G, T_new=T),
        grid_spec=pltpu.PrefetchScalarGridSpec(
            num_scalar_prefetch=1,
            grid=(B, n_groups + 1),
            in_specs=in_specs,
            out_specs=per_b((RQ, HD_C)),
            scratch_shapes=[pltpu.VMEM((RQ, HD_C), jnp.float32), pltpu.VMEM((RQ, HD_C), jnp.float32),
                            pltpu.VMEM((RQ, HD_C), jnp.float32), pltpu.VMEM((H_C, 1), jnp.float32)]),
        out_shape=jax.ShapeDtypeStruct((B, RQ, HD_C), jnp.float32),
        compiler_params=pltpu.CompilerParams(dimension_semantics=("parallel", "arbitrary"),
                                             vmem_limit_bytes=VMEM_LIMIT_BYTES),
        name="fox_sample_paged_attention",
    )(page_table, qr, *([ck] * G), *([cv] * G), *([clf] * G), knew, vnew, lfnew)
    return jnp.swapaxes(o.reshape(B, H_C, T, HD_C), 1, 2).reshape(B, T, H_C * HD_C)


def _fox_cumsum_body(lf_ref, f_ref, *, blk):
    T = lf_ref.shape[-1]
    tri = _upper_tri(blk)
    carry = jnp.zeros((lf_ref.shape[0], 1), jnp.float32)
    for j in range(T // blk):
        sl = slice(j * blk, (j + 1) * blk)
        Fp = _dot3(_split3(lf_ref[:, sl]), tri) + carry
        f_ref[:, sl] = Fp
        carry = Fp[:, blk - 1:blk]


def _fox_prompt_body(q_ref, kv_ref, f_ref, o_ref, qs_ref, m_ref, l_ref, acc_ref, *, tq):
    qi = pl.program_id(1)
    kv = pl.program_id(2)
    head_cols = lambda g: pl.ds(pl.multiple_of((kv * G_C + g) * HD_C, HD_C), HD_C)
    k_cols = pl.ds(pl.multiple_of(kv * HD_C, HD_C), HD_C)
    v_cols = pl.ds(pl.multiple_of(KV_C + kv * HD_C, HD_C), HD_C)
    for g in range(G_C):
        qs_ref[g * tq:(g + 1) * tq, :] = _bf(q_ref[:, head_cols(g)] * (HD_C ** -0.5))
    m_ref[...] = jnp.full_like(m_ref, NEG_BIG)
    l_ref[...] = jnp.zeros_like(l_ref)
    acc_ref[...] = jnp.zeros_like(acc_ref)
    qpos = qi * tq + lax.broadcasted_iota(jnp.int32, (G_C * tq, tq), 0) % tq
    kcol = lax.broadcasted_iota(jnp.int32, (G_C * tq, tq), 1)

    def body(j, carry):
        ks = pl.multiple_of(j * tq, tq)
        K = _bf(kv_ref[pl.ds(ks, tq), k_cols])
        V = _bf(kv_ref[pl.ds(ks, tq), v_cols])
        s = lax.dot_general(qs_ref[...], K, (((1,), (1,)), ((), ())), preferred_element_type=jnp.float32)
        F = f_ref[:, pl.ds(ks, tq)]
        bias = jnp.concatenate([jnp.broadcast_to(F[g:g + 1, :], (tq, tq)) for g in range(G_C)], axis=0)
        s = jnp.where(kcol + ks <= qpos, s - bias, NEG_BIG)
        _online_softmax_step(s, V, m_ref, l_ref, acc_ref, slice(None))
        return carry

    lax.fori_loop(0, qi + 1, body, 0)
    o = acc_ref[...] / l_ref[...]
    for g in range(G_C):
        o_ref[:, head_cols(g)] = o[g * tq:(g + 1) * tq, :]


def fox_prompt(q, kv, logf):
    B, T = q.shape[:2]
    tq = min(FOX_Q_TILE, T)
    lfT = jnp.swapaxes(logf, 1, 2)
    F = pl.pallas_call(
        functools.partial(_fox_cumsum_body, blk=tq),
        grid=(B,),
        in_specs=[pl.BlockSpec((None, H_C, T), lambda b: (b, 0, 0))],
        out_specs=pl.BlockSpec((None, H_C, T), lambda b: (b, 0, 0)),
        out_shape=jax.ShapeDtypeStruct((B, H_C, T), jnp.float32),
        name="fox_logf_cumsum",
    )(lfT)
    o = pl.pallas_call(
        functools.partial(_fox_prompt_body, tq=tq),
        grid=(B, T // tq, KVH_C),
        in_specs=[pl.BlockSpec((None, tq, C_C), lambda b, qi, kv: (b, qi, 0)),
                  pl.BlockSpec((None, T, 2 * KV_C), lambda b, qi, kv: (b, 0, 0)),
                  pl.BlockSpec((None, None, G_C, T), lambda b, qi, kv: (b, kv, 0, 0))],
        out_specs=pl.BlockSpec((None, tq, C_C), lambda b, qi, kv: (b, qi, 0)),
        out_shape=jax.ShapeDtypeStruct((B, T, C_C), jnp.float32),
        scratch_shapes=[pltpu.VMEM((G_C * tq, HD_C), jnp.bfloat16),
                        pltpu.VMEM((G_C * tq, HD_C), jnp.float32), pltpu.VMEM((G_C * tq, HD_C), jnp.float32),
                        pltpu.VMEM((G_C * tq, HD_C), jnp.float32)],
        compiler_params=pltpu.CompilerParams(dimension_semantics=("parallel", "parallel", "arbitrary"),
                                             vmem_limit_bytes=VMEM_LIMIT_LARGE),
        name="fox_prompt_flash_attention",
    )(q, kv, F.reshape(B, KVH_C, G_C, T))
    return o


MOE_ROWS = 256
MOE_COMBINE_TOKENS = 64


def _moe_up_body(blk_exp_ref, nused_ref, row_tok_ref, roww_ref, x_hbm, wg_ref, wu_ref, act_ref,
                 xbuf, wg_bf, wu_bf, sem, *, RB):
    b = pl.program_id(0)
    nused = nused_ref[0]
    slot = b % 2

    def gather(blk, slot_):
        def body(i, carry):
            t = row_tok_ref[blk * RB + i]
            pltpu.make_async_copy(x_hbm.at[pl.ds(t, 1)], xbuf.at[slot_, pl.ds(i, 1)], sem.at[slot_]).start()
            return carry
        lax.fori_loop(0, RB, body, 0, unroll=8)

    @pl.when(b == 0)
    def _():
        gather(0, 0)

    @pl.when(b + 1 < nused)
    def _():
        gather(b + 1, 1 - slot)

    @pl.when(b < nused)
    def _():
        pltpu.make_async_copy(x_hbm.at[pl.ds(0, RB)], xbuf.at[slot], sem.at[slot]).wait()
        e = blk_exp_ref[b]
        e_prev = blk_exp_ref[jnp.maximum(b - 1, 0)]

        @pl.when((b == 0) | (e != e_prev))
        def _():
            wg_bf[...] = wg_ref[...].astype(jnp.bfloat16)
            wu_bf[...] = wu_ref[...].astype(jnp.bfloat16)

        x = xbuf[slot].astype(jnp.bfloat16)
        g = jnp.dot(x, wg_bf[...], preferred_element_type=jnp.float32)
        u = jnp.dot(x, wu_bf[...], preferred_element_type=jnp.float32)
        act_ref[...] = (g * jax.nn.sigmoid(g)) * u * roww_ref[...]

    @pl.when(b >= nused)
    def _():
        act_ref[...] = jnp.zeros_like(act_ref)


def _moe_down_body(blk_exp_ref, nused_ref, act_ref, wd_ref, y_ref, wd_bf):
    b = pl.program_id(0)
    nused = nused_ref[0]

    @pl.when(b < nused)
    def _():
        e = blk_exp_ref[b]
        e_prev = blk_exp_ref[jnp.maximum(b - 1, 0)]

        @pl.when((b == 0) | (e != e_prev))
        def _():
            wd_bf[...] = wd_ref[...].astype(jnp.bfloat16)

        y_ref[...] = jnp.dot(act_ref[...].astype(jnp.bfloat16), wd_bf[...], preferred_element_type=jnp.float32)

    @pl.when(b >= nused)
    def _():
        y_ref[...] = jnp.zeros_like(y_ref)


def _moe_combine_body(pos_ref, y_hbm, shared_ref, out_ref, buf, sem, *, TB):
    b = pl.program_id(0)
    nb = pl.num_programs(0)
    slot = b % 2

    def gather(blk, slot_):
        def body(i, carry):
            for kk in range(TOP_K):
                p = pos_ref[(blk * TB + i) * TOP_K + kk]
                pltpu.make_async_copy(y_hbm.at[pl.ds(p, 1)], buf.at[slot_, pl.ds(kk * TB + i, 1)],
                                      sem.at[slot_]).start()
            return carry
        lax.fori_loop(0, TB, body, 0, unroll=2)

    @pl.when(b == 0)
    def _():
        gather(0, 0)

    @pl.when(b + 1 < nb)
    def _():
        gather(b + 1, 1 - slot)

    pltpu.make_async_copy(y_hbm.at[pl.ds(0, TOP_K * TB)], buf.at[slot], sem.at[slot]).wait()
    acc = shared_ref[...]
    for kk in range(TOP_K):
        acc = acc + buf[slot, kk * TB:(kk + 1) * TB]
    out_ref[...] = acc


def _moe_invert_body(pos_ref, src_ref):
    def clear(i, carry):
        src_ref[i] = -1
        return carry
    lax.fori_loop(0, src_ref.shape[0], clear, 0, unroll=8)

    def put(i, carry):
        src_ref[pos_ref[i]] = i
        return carry
    lax.fori_loop(0, pos_ref.shape[0], put, 0, unroll=8)


ROUTE_TOKENS = 256
NEG_INF = float("-inf")


def _first_max(x, idx, axis, n):
    m = jnp.max(x, axis=axis, keepdims=True)
    first = jnp.min(jnp.where(x == m, idx, n), axis=axis, keepdims=True)
    return m, first


def _moe_router_body(wt_ref, h_ref, bias_ref, eidx_ref, gate_ref):
    tn = h_ref.shape[0]
    per_group = N_EXPERTS // N_GROUPS
    logits = lax.dot_general(_bf(wt_ref[...]), _bf(h_ref[...]), (((1,), (1,)), ((), ())),
                             preferred_element_type=jnp.float32)
    scores = jax.nn.sigmoid(logits)
    biased = scores + bias_ref[:, :1]
    b3 = biased.reshape(N_GROUPS, per_group, tn)
    i3 = lax.broadcasted_iota(jnp.int32, b3.shape, 1)
    m1, f1 = _first_max(b3, i3, 1, per_group)
    m2 = jnp.max(jnp.where(i3 == f1, NEG_INF, b3), axis=1, keepdims=True)
    grp = (m1 + m2).reshape(N_GROUPS, tn)
    gi = lax.broadcasted_iota(jnp.int32, grp.shape, 0)
    keep = jnp.zeros(grp.shape, jnp.float32)
    for _ in range(TOPK_GROUPS):
        _, f = _first_max(grp, gi, 0, N_GROUPS)
        keep = jnp.where(gi == f, 1.0, keep)
        grp = jnp.where(gi == f, NEG_INF, grp)
    cand = jnp.where(keep.reshape(N_GROUPS, 1, tn) > 0.5, b3, NEG_INF).reshape(N_EXPERTS, tn)
    ei = lax.broadcasted_iota(jnp.int32, cand.shape, 0)
    ids, gs = [], []
    for _ in range(TOP_K):
        _, f = _first_max(cand, ei, 0, N_EXPERTS)
        ids.append(f)
        gs.append(jnp.sum(jnp.where(ei == f, scores, 0.0), axis=0, keepdims=True))
        cand = jnp.where(ei == f, NEG_INF, cand)
    tot = gs[0]
    for x in gs[1:]:
        tot = tot + x
    pad = eidx_ref.shape[0] - TOP_K
    eidx_ref[...] = jnp.concatenate(ids + [jnp.zeros((pad, tn), jnp.int32)], axis=0)
    gate_ref[...] = jnp.concatenate([x / tot * ROUTED_SCALE for x in gs] + [jnp.zeros((pad, tn), jnp.float32)],
                                    axis=0)


def moe_route(h, w_router, b_router):
    T, D = h.shape
    tn = ROUTE_TOKENS
    rows = 8
    bias = jnp.broadcast_to(b_router[:, None], (N_EXPERTS, 128))
    eidx, gates = pl.pallas_call(
        _moe_router_body,
        grid=(T // tn,),
        in_specs=[pl.BlockSpec((N_EXPERTS, D), lambda i: (0, 0)),
                  pl.BlockSpec((tn, D), lambda i: (i, 0)),
                  pl.BlockSpec((N_EXPERTS, 128), lambda i: (0, 0))],
        out_specs=[pl.BlockSpec((rows, tn), lambda i: (0, i)), pl.BlockSpec((rows, tn), lambda i: (0, i))],
        out_shape=[jax.ShapeDtypeStruct((rows, T), jnp.int32), jax.ShapeDtypeStruct((rows, T), jnp.float32)],
        compiler_params=pltpu.CompilerParams(dimension_semantics=("parallel",)),
        name="moe_router_topk",
    )(w_router.T, h, bias)
    return eidx[:TOP_K].T, gates[:TOP_K].T


def moe_layout(eidx, gates, RB):
    T = eidx.shape[0]
    A = T * TOP_K
    E = N_EXPERTS
    e_flat = eidx.reshape(A).astype(jnp.int32)
    g_flat = gates.reshape(A)
    onehot = (e_flat[:, None] == jnp.arange(E, dtype=jnp.int32)[None, :]).astype(jnp.int32)
    rank_incl = jnp.cumsum(onehot, axis=0)
    rank = jnp.sum(onehot * rank_incl, axis=1) - 1
    counts = rank_incl[-1]
    padded = (counts + RB - 1) // RB * RB
    seg_start = jnp.cumsum(counts) - counts
    pad_end = jnp.cumsum(padded)
    pad_start = pad_end - padded
    pos = pad_start[e_flat] + rank
    n_blocks = -(-(A + E * (RB - 1)) // RB)
    n_rows = n_blocks * RB
    blk_exp = jnp.minimum(jnp.searchsorted(pad_end, jnp.arange(n_blocks, dtype=jnp.int32) * RB, side='right'),
                          E - 1).astype(jnp.int32)
    nused = (pad_end[-1] // RB).astype(jnp.int32).reshape(1)
    pos = pos.astype(jnp.int32)
    src = pl.pallas_call(
        _moe_invert_body,
        grid_spec=pltpu.PrefetchScalarGridSpec(
            num_scalar_prefetch=1, grid=(1,), in_specs=[],
            out_specs=pl.BlockSpec(memory_space=pltpu.SMEM)),
        out_shape=jax.ShapeDtypeStruct((n_rows,), jnp.int32),
        name="moe_invert_positions",
    )(pos)
    valid = src >= 0
    row_tok = jnp.where(valid, src // TOP_K, 0).astype(jnp.int32)
    row_w = jnp.where(valid, g_flat[jnp.maximum(src, 0)], 0.0).astype(jnp.float32)
    return pos, row_tok, row_w, blk_exp, nused, n_blocks


def moe_routed_plus_shared(h, layer, eidx, gates, w_gate, w_up, w_down, shared):
    T, D = h.shape
    F = w_gate.shape[-1]
    RB = MOE_ROWS
    TB = MOE_COMBINE_TOKENS
    pos, row_tok, row_w, blk_exp, nused, n_blocks = moe_layout(eidx, gates, RB)
    n_rows = n_blocks * RB
    wspec_up = pl.BlockSpec((None, None, D, F), lambda b, be, nu, rt: (layer, be[b], 0, 0))
    act = pl.pallas_call(
        functools.partial(_moe_up_body, RB=RB),
        grid_spec=pltpu.PrefetchScalarGridSpec(
            num_scalar_prefetch=3,
            grid=(n_blocks,),
            in_specs=[pl.BlockSpec((RB, 1), lambda b, be, nu, rt: (b, 0)),
                      pl.BlockSpec(memory_space=pl.ANY),
                      wspec_up, wspec_up],
            out_specs=pl.BlockSpec((RB, F), lambda b, be, nu, rt: (b, 0)),
            scratch_shapes=[pltpu.VMEM((2, RB, D), jnp.float32),
                            pltpu.VMEM((D, F), jnp.bfloat16),
                            pltpu.VMEM((D, F), jnp.bfloat16),
                            pltpu.SemaphoreType.DMA((2,))]),
        out_shape=jax.ShapeDtypeStruct((n_rows, F), jnp.float32),
        compiler_params=pltpu.CompilerParams(dimension_semantics=("arbitrary",),
                                             vmem_limit_bytes=VMEM_LIMIT_LARGE),
        name="moe_gate_up",
    )(blk_exp, nused, row_tok, row_w.reshape(n_rows, 1), h, w_gate, w_up)
    y = pl.pallas_call(
        _moe_down_body,
        grid_spec=pltpu.PrefetchScalarGridSpec(
            num_scalar_prefetch=2,
            grid=(n_blocks,),
            in_specs=[pl.BlockSpec((RB, F), lambda b, be, nu: (b, 0)),
                      pl.BlockSpec((None, None, F, D), lambda b, be, nu: (layer, be[b], 0, 0))],
            out_specs=pl.BlockSpec((RB, D), lambda b, be, nu: (b, 0)),
            scratch_shapes=[pltpu.VMEM((F, D), jnp.bfloat16)]),
        out_shape=jax.ShapeDtypeStruct((n_rows, D), jnp.float32),
        compiler_params=pltpu.CompilerParams(dimension_semantics=("arbitrary",),
                                             vmem_limit_bytes=VMEM_LIMIT_LARGE),
        name="moe_down",
    )(blk_exp, nused, act, w_down)
    return pl.pallas_call(
        functools.partial(_moe_combine_body, TB=TB),
        grid_spec=pltpu.PrefetchScalarGridSpec(
            num_scalar_prefetch=1,
            grid=(T // TB,),
            in_specs=[pl.BlockSpec(memory_space=pl.ANY),
                      pl.BlockSpec((TB, D), lambda b, ps: (b, 0))],
            out_specs=pl.BlockSpec((TB, D), lambda b, ps: (b, 0)),
            scratch_shapes=[pltpu.VMEM((2, TOP_K * TB, D), jnp.float32),
                            pltpu.SemaphoreType.DMA((2,))]),
        out_shape=jax.ShapeDtypeStruct((T, D), jnp.float32),
        compiler_params=pltpu.CompilerParams(dimension_semantics=("arbitrary",),
                                             vmem_limit_bytes=VMEM_LIMIT_LARGE),
        name="moe_combine",
    )(pos, y, shared)


def moe_ffn(h, layer, w_router, b_router, w_gate, w_up, w_down, ws_gate, ws_up, ws_down):
    eidx, gates = moe_route(h, w_router[layer], b_router[layer])
    shared = matmul(jax.nn.silu(matmul(h, ws_gate[layer])) * matmul(h, ws_up[layer]), ws_down[layer])
    return moe_routed_plus_shared(h, layer, eidx, gates, w_gate, w_up, w_down, shared)


def kernel(x_prompt, x_sample, c_prompt, c_sample, state_rwkv, state_rwkv_shift, cache_swa_k, cache_swa_v, cache_fox_k, cache_fox_v, cache_fox_logf, page_table, ada_w, ada_b, norm_w, final_norm_w, ab_w_in, ab_w_out, rwkv_mu, rwkv_w0, rwkv_w_decay_up, rwkv_a0, rwkv_w_aaa_up, rwkv_w_gate_up, rwkv_k_k, rwkv_k_a, rwkv_r_k, rwkv_lnx_w, rwkv_lnx_b, swa_sinks, fox_w_in, fox_b_f, fox_w_out, moe_w_router, moe_b_router, moe_w_gate, moe_w_up, moe_w_down, shared_w_gate, shared_w_up, shared_w_down):
    Bp, Tp, D = x_prompt.shape
    Bs, Ts, _ = x_sample.shape
    depth = ada_w.shape[0]
    past_len = page_table.shape[1] * PAGE_SIZE
    pos_p = jnp.arange(Tp)
    pos_s = past_len + jnp.arange(Ts)
    xp, xs = x_prompt, x_sample
    rw_S_p, rw_sh_p, sw_k_p, sw_v_p, fk_p, fv_p, flf_p = [], [], [], [], [], [], []
    rw_S_s, rw_sh_s, sw_k_s, sw_v_s, fk_s, fv_s, flf_s = [], [], [], [], [], [], []
    for l in range(depth):
        mods = ada_params(jnp.concatenate([c_prompt, c_sample], axis=0), ada_w, l, ada_b[l])
        sh1_p, sc1_p, g1_p, sh2_p, sc2_p, g2_p = [m[:Bp] for m in mods]
        sh1_s, sc1_s, g1_s, sh2_s, sc2_s, g2_s = [m[Bp:] for m in mods]
        if l == 0:
            hp = rmsnorm(xp, norm_w[l, 0], sc1_p, sh1_p, jnp.bfloat16)
            hs = rmsnorm(xs, norm_w[l, 0], sc1_s, sh1_s, jnp.bfloat16)
        else:
            xp, hp = rmsnorm(xp, norm_w[l, 0], sc1_p, sh1_p, jnp.bfloat16, y, gy_p, 0, True)
            xs, hs = rmsnorm(xs, norm_w[l, 0], sc1_s, sh1_s, jnp.bfloat16, y, gy_s, Bp * Tp, True)
        if l % 2 == 0:
            i = l // 2
            rw = (rwkv_mu[i], rwkv_w0[i], rwkv_w_decay_up[i], rwkv_a0[i], rwkv_w_aaa_up[i], rwkv_w_gate_up[i],
                  rwkv_k_k[i], rwkv_k_a[i], rwkv_r_k[i], rwkv_lnx_w[i], rwkv_lnx_b[i])
            o_p, S_p, row_p, kw_p, vw_p = ab_mixer(
                hp, pos_p, jnp.zeros((Bp, P_A), jnp.float32), jnp.zeros((Bp, H_A, HD_A, HD_A), jnp.float32),
                None, None, ab_w_in[i], ab_w_out[i], rw, swa_sinks[i])
            o_s, S_s, row_s, kw_s, vw_s = ab_mixer(
                hs, pos_s, state_rwkv_shift[i], state_rwkv[i], cache_swa_k[i], cache_swa_v[i],
                ab_w_in[i], ab_w_out[i], rw, swa_sinks[i])
            rw_S_p.append(S_p); rw_sh_p.append(row_p); sw_k_p.append(kw_p); sw_v_p.append(vw_p)
            rw_S_s.append(S_s); rw_sh_s.append(row_s); sw_k_s.append(kw_s); sw_v_s.append(vw_s)
        else:
            j = l // 2
            q_p, kv_p, lf_p = fox_project(hp, fox_w_in[j], fox_b_f[j])
            q_s, kv_s, lf_s = fox_project(hs, fox_w_in[j], fox_b_f[j])
            k_p, v_p = (kv_p[..., c:c + KV_C].reshape(Bp, Tp, KVH_C, HD_C) for c in (0, KV_C))
            k_s, v_s = (kv_s[..., c:c + KV_C].reshape(Bs, Ts, KVH_C, HD_C) for c in (0, KV_C))
            o_p = mm3(fox_prompt(q_p, kv_p, lf_p), fox_w_out[j])
            o_s = mm3(fox_sample(q_s.reshape(Bs, Ts, H_C, HD_C), k_s, v_s, lf_s,
                                 cache_fox_k, cache_fox_v, cache_fox_logf, j, page_table),
                      fox_w_out[j])
            fk_p.append(k_p); fv_p.append(v_p); flf_p.append(lf_p)
            fk_s.append(k_s); fv_s.append(v_s); flf_s.append(lf_s)
        xp, h2p = rmsnorm(xp, norm_w[l, 1], sc2_p, sh2_p, jnp.float32, o_p.reshape(Bp * Tp, D), g1_p, 0, True)
        xs, h2s = rmsnorm(xs, norm_w[l, 1], sc2_s, sh2_s, jnp.float32, o_s.reshape(Bs * Ts, D), g1_s, 0, True)
        tok = jnp.concatenate([h2p.reshape(Bp * Tp, D), h2s.reshape(Bs * Ts, D)], axis=0)
        y = moe_ffn(tok, l, moe_w_router, moe_b_router, moe_w_gate, moe_w_up, moe_w_down,
                    shared_w_gate, shared_w_up, shared_w_down)
        gy_p, gy_s = g2_p, g2_s
    y_prompt = rmsnorm(xp, final_norm_w, resid=y, gate=gy_p, resid_row0=0)
    y_sample = rmsnorm(xs, final_norm_w, resid=y, gate=gy_s, resid_row0=Bp * Tp)
    return (y_prompt, y_sample,
            jnp.stack(rw_S_p), jnp.stack(rw_sh_p), jnp.stack(sw_k_p), jnp.stack(sw_v_p),
            jnp.stack(fk_p), jnp.stack(fv_p), jnp.stack(flf_p),
            jnp.stack(rw_S_s), jnp.stack(rw_sh_s), jnp.stack(sw_k_s), jnp.stack(sw_v_s),
            jnp.stack(fk_s), jnp.stack(fv_s), jnp.stack(flf_s))
```
